```python
import math
import jax, jax.numpy as jnp
from jax import lax
import numpy as np

D_MODEL = 1024
BATCH = 8
SEQ = 4096
DEPTH = 4

GRID_W = 64
CTX_LEN = 256
N_MIXERS = 4
DEEPNORM_ALPHA = (2 * DEPTH) ** 0.25
DEEPNORM_BETA = (8 * DEPTH) ** -0.25
LN_EPS = 1e-5
RMS_EPS = 1e-6
NEG_INF = -1e30
ROPE_BASE = 10000.0
Q_BLOCK = 128
SC_WIDTH = 3
SWA_HEADS = 16
SWA_KV_HEADS = 4
SWA_GROUP = SWA_HEADS // SWA_KV_HEADS
SWA_HEAD_DIM = D_MODEL // SWA_HEADS
SWA_SCALE = SWA_HEAD_DIM ** -0.5
SWA_WINDOW = 128
SWA_BLOCK = 128
DIFF_HEADS = 8
DIFF_HEAD_DIM = D_MODEL // (2 * DIFF_HEADS)
DIFF_SCALE = DIFF_HEAD_DIM ** -0.5
DN_QK_HEADS = 8
DN_V_HEADS = 16
DN_HEAD_DIM = 128
DN_CONV = 5
DN_CHUNK = 64
N_EXPERTS = 32
TOP_K = 4
D_FF = D_MODEL
SWIGLU_LIMIT = 7.0
SWIGLU_ALPHA = 1.702
EXPERT_BLOCK = 128

kernel_name = 'hybrid_flow_trunk_conv_swa_diff_deltanet_moe'


def layer_norm(x, g, b):
    xf = x.astype(jnp.float32)
    mu = jnp.mean(xf, axis=-1, keepdims=True)
    var = jnp.mean(jnp.square(xf - mu), axis=-1, keepdims=True)
    y = (xf - mu) * lax.rsqrt(var + LN_EPS) * g.astype(jnp.float32) + b.astype(jnp.float32)
    return y.astype(x.dtype)


def rms_norm(x, g):
    xf = x.astype(jnp.float32)
    y = xf * lax.rsqrt(jnp.mean(jnp.square(xf), axis=-1, keepdims=True) + RMS_EPS)
    return (y * g.astype(jnp.float32)).astype(x.dtype)


def l2_normalize(x):
    xf = x.astype(jnp.float32)
    return (xf * lax.rsqrt(jnp.sum(jnp.square(xf), axis=-1, keepdims=True) + RMS_EPS)).astype(x.dtype)


def modulate(x, shift, scale):
    return x * (1.0 + scale) + shift


def centred_dwconv(x, w):
    pad = (w.shape[0] - 1) // 2
    return lax.conv_general_dilated(x, w[:, None, :].astype(x.dtype), window_strides=(1,),
                                    padding=[(pad, pad)], dimension_numbers=('NWC', 'WIO', 'NWC'),
                                    feature_group_count=x.shape[-1])


def axial_rope_tables(rows, head_dim):
    row = jnp.repeat(jnp.arange(rows, dtype=jnp.float32), GRID_W)
    col = jnp.tile(jnp.arange(GRID_W, dtype=jnp.float32), rows)
    half = head_dim // 2
    inv = ROPE_BASE ** (-jnp.arange(0, half, 2, dtype=jnp.float32) / half)
    ar = row[:, None] * inv
    ac = col[:, None] * inv
    ang = jnp.concatenate([ar, ar, ac, ac], axis=-1)
    return jnp.cos(ang), jnp.sin(ang)


def apply_rope(x, cos, sin):
    x1, x2, x3, x4 = jnp.split(x, 4, axis=-1)
    rot = jnp.concatenate([-x2, x1, -x4, x3], axis=-1)
    return (x * cos[:, None, :] + rot * sin[:, None, :]).astype(x.dtype)


def short_conv_mixer(hl, hc, w_in, w_conv, w_out, ctx_out):
    def mix(h):
        b_gate, c_gate, u = jnp.split(h @ w_in, 3, axis=-1)
        return (b_gate * centred_dwconv(c_gate * u, w_conv)) @ w_out
    return mix(hl), (mix(hc) if ctx_out else None)


def sink_attend(q, k, v, sink_logit, mask):
    s = jnp.einsum('bqkgd,bskd->bkgqs', q, k).astype(jnp.float32) * SWA_SCALE
    if mask is not None:
        s = jnp.where(mask, s, NEG_INF)
    sink_col = jnp.broadcast_to(sink_logit, s.shape[:-1] + (1,))
    p = jax.nn.softmax(jnp.concatenate([s, sink_col], axis=-1), axis=-1)[..., :-1]
    return jnp.einsum('bkgqs,bskd->bqkgd', p.astype(v.dtype), v)


def swa_mixer(hl, hc, w_qkv, b_qkv, sink, w_o, b_o, cos, sin, ctx_out):
    bsz, n_lat, _ = hl.shape
    nq = SWA_HEADS * SWA_HEAD_DIM
    nkv = SWA_KV_HEADS * SWA_HEAD_DIM

    def proj(h):
        qkv = h @ w_qkv + b_qkv
        sh = h.shape[:2]
        q = qkv[..., :nq].reshape(sh + (SWA_HEADS, SWA_HEAD_DIM))
        k = qkv[..., nq:nq + nkv].reshape(sh + (SWA_KV_HEADS, SWA_HEAD_DIM))
        v = qkv[..., nq + nkv:].reshape(sh + (SWA_KV_HEADS, SWA_HEAD_DIM))
        return q, k, v

    ql, kl, vl = proj(hl)
    qc, kc, vc = proj(hc)
    ql = apply_rope(ql, cos, sin)
    kl = apply_rope(kl, cos, sin)
    group = lambda q: q.reshape(q.shape[:2] + (SWA_KV_HEADS, SWA_GROUP, SWA_HEAD_DIM))
    ql, qc = group(ql), group(qc)
    sink_logit = sink.astype(jnp.float32).reshape(SWA_KV_HEADS, SWA_GROUP, 1, 1)

    nb = n_lat // SWA_BLOCK

    def windows(a):
        ap = jnp.pad(a, ((0, 0), (SWA_BLOCK, SWA_BLOCK), (0, 0), (0, 0)))
        ap = ap.reshape(bsz, nb + 2, SWA_BLOCK, SWA_KV_HEADS, SWA_HEAD_DIM)
        w = jnp.concatenate([ap[:, :-2], ap[:, 1:-1], ap[:, 2:]], axis=2)
        return jnp.moveaxis(w, 1, 0)

    kw, vw = windows(kl), windows(vl)
    qb = jnp.moveaxis(ql.reshape(bsz, nb, SWA_BLOCK, SWA_KV_HEADS, SWA_GROUP, SWA_HEAD_DIM), 1, 0)
    blk = jnp.arange(nb)[:, None, None]
    qpos = blk * SWA_BLOCK + jnp.arange(SWA_BLOCK)[None, :, None]
    kpos = (blk - 1) * SWA_BLOCK + jnp.arange(3 * SWA_BLOCK)[None, None, :]
    band = (jnp.abs(qpos - kpos) <= SWA_WINDOW) & (kpos >= 0) & (kpos < n_lat)
    band = jnp.concatenate([band, jnp.ones((nb, SWA_BLOCK, kc.shape[1]), bool)], axis=-1)

    def block(args):
        q_b, k_b, v_b, m_b = args
        return sink_attend(q_b, jnp.concatenate([k_b, kc], axis=1),
                           jnp.concatenate([v_b, vc], axis=1), sink_logit, m_b)

    ol = lax.map(block, (qb, kw, vw, band))
    ol = jnp.moveaxis(ol, 0, 1).reshape(bsz, n_lat, nq)
    yl = ol @ w_o + b_o
    yc = None
    if ctx_out:
        oc = sink_attend(qc, kc, vc, sink_logit, None).reshape(bsz, kc.shape[1], nq)
        yc = oc @ w_o + b_o
    return yl, yc


def diff_mixer(hl, hc, w_qkv, lam_p, subln_g, w_o, lam_init, cos, sin, ctx_out):
    bsz, n_lat, _ = hl.shape

    def proj(h):
        q, k, v = jnp.split(h @ w_qkv, 3, axis=-1)
        sh = h.shape[:2]
        return (q.reshape(sh + (2 * DIFF_HEADS, DIFF_HEAD_DIM)),
                k.reshape(sh + (2 * DIFF_HEADS, DIFF_HEAD_DIM)),
                v.reshape(sh + (DIFF_HEADS, 2 * DIFF_HEAD_DIM)))

    ql, kl, vl = proj(hl)
    qc, kc, vc = proj(hc)
    ql = apply_rope(ql, cos, sin)
    kl = apply_rope(kl, cos, sin)
    split2 = lambda a: a.reshape(a.shape[:2] + (DIFF_HEADS, 2, DIFF_HEAD_DIM))
    lp = lam_p.astype(jnp.float32)
    lam = jnp.exp(jnp.sum(lp[0] * lp[1])) - jnp.exp(jnp.sum(lp[2] * lp[3])) + lam_init

    def diff_attend(q, k, v):
        s = jnp.einsum('bqhid,bshid->bhiqs', q, k).astype(jnp.float32) * DIFF_SCALE
        p = jax.nn.softmax(s, axis=-1)
        a = p[:, :, 0] - lam * p[:, :, 1]
        return jnp.einsum('bhqs,bshe->bqhe', a.astype(v.dtype), v)

    def head_out(o):
        o = rms_norm(o, subln_g) * (1.0 - lam_init)
        return o.reshape(o.shape[:2] + (DIFF_HEADS * 2 * DIFF_HEAD_DIM,)) @ w_o

    kc2 = split2(kc)
    k_all = jnp.concatenate([split2(kl), kc2], axis=1)
    v_all = jnp.concatenate([vl, vc], axis=1)
    nb = n_lat // Q_BLOCK
    qb = jnp.moveaxis(split2(ql).reshape(bsz, nb, Q_BLOCK, DIFF_HEADS, 2, DIFF_HEAD_DIM), 1, 0)
    ol = lax.map(lambda q_b: diff_attend(q_b, k_all, v_all), qb)
    ol = jnp.moveaxis(ol, 0, 1).reshape(bsz, n_lat, DIFF_HEADS, 2 * DIFF_HEAD_DIM)
    yl = head_out(ol)
    yc = head_out(diff_attend(split2(qc), kc2, vc)) if ctx_out else None
    return yl, yc


def chunk_gated_delta(q, k, v, g, beta, s0):
    bsz, t, h, _ = q.shape
    dv = v.shape[-1]
    n = t // DN_CHUNK
    f32 = jnp.float32

    def chunks(a):
        a = a.astype(f32).reshape((bsz, n, DN_CHUNK, h) + a.shape[3:])
        return jnp.moveaxis(a, (1, 2), (0, 3))

    qc, kc, vc, gch, bc = chunks(q), chunks(k), chunks(v), chunks(g), chunks(beta)
    gcum = jnp.cumsum(gch, axis=-1)
    idx = jnp.arange(DN_CHUNK)
    incl = idx[:, None] >= idx[None, :]
    strict = idx[:, None] > idx[None, :]
    decay = jnp.exp(jnp.where(incl, gcum[..., :, None] - gcum[..., None, :], -jnp.inf))
    kb = kc * bc[..., None]
    vb = vc * bc[..., None]
    lower = jnp.where(strict, jnp.einsum('nbhid,nbhjd->nbhij', kb, kc) * decay, 0.0)
    u = lax.linalg.triangular_solve(lower, vb, left_side=True, lower=True, unit_diagonal=True)
    w = lax.linalg.triangular_solve(lower, kb * jnp.exp(gcum)[..., None], left_side=True,
                                    lower=True, unit_diagonal=True)
    intra = jnp.where(incl, jnp.einsum('nbhid,nbhjd->nbhij', qc, kc) * decay, 0.0)

    def step(s, inp):
        q_i, k_i, u_i, w_i, g_i, a_i = inp
        v_new = u_i - jnp.einsum('bhcd,bhde->bhce', w_i, s)
        o = (jnp.einsum('bhcd,bhde->bhce', q_i * jnp.exp(g_i)[..., None], s)
             + jnp.einsum('bhij,bhje->bhie', a_i, v_new))
        g_last = g_i[..., -1]
        s = (s * jnp.exp(g_last)[..., None, None]
             + jnp.einsum('bhcd,bhce->bhde', k_i * jnp.exp(g_last[..., None] - g_i)[..., None], v_new))
        return s, o

    s_fin, o = lax.scan(step, s0.astype(f32), (qc, kc, u, w, gcum, intra))
    o = jnp.moveaxis(o, (0, 3), (1, 2)).reshape(bsz, t, h, dv)
    return o.astype(v.dtype), s_fin


def delta_mixer(hl, hc, w_qkvz, w_ba, a_log, dt_bias, w_conv, norm_g, w_o, ctx_out):
    nqk = DN_QK_HEADS * DN_HEAD_DIM
    nv = DN_V_HEADS * DN_HEAD_DIM
    rep = DN_V_HEADS // DN_QK_HEADS

    def prep(h):
        bsz, t, _ = h.shape
        proj = h @ w_qkvz
        qkv = jax.nn.silu(centred_dwconv(proj[..., :2 * nqk + nv], w_conv))
        z = proj[..., 2 * nqk + nv:].reshape(bsz, t, DN_V_HEADS, DN_HEAD_DIM)
        q = qkv[..., :nqk].reshape(bsz, t, DN_QK_HEADS, DN_HEAD_DIM)
        k = qkv[..., nqk:2 * nqk].reshape(bsz, t, DN_QK_HEADS, DN_HEAD_DIM)
        v = qkv[..., 2 * nqk:].reshape(bsz, t, DN_V_HEADS, DN_HEAD_DIM)
        q = jnp.repeat(l2_normalize(q), rep, axis=2) * (DN_HEAD_DIM ** -0.5)
        k = jnp.repeat(l2_normalize(k), rep, axis=2)
        ba = (h @ w_ba).astype(jnp.float32).reshape(bsz, t, 2, 2, DN_V_HEADS)
        beta = jax.nn.sigmoid(ba[:, :, :, 0])
        g = -jnp.exp(a_log.astype(jnp.float32)) * jax.nn.softplus(ba[:, :, :, 1] + dt_bias.astype(jnp.float32))
        return q, k, v, z, beta, g

    qc, kc, vc, zc, bc, gc = prep(hc)
    ql, kl, vl, zl, bl, gl = prep(hl)
    rev = lambda a: a[:, ::-1]
    s0 = jnp.zeros((hc.shape[0], DN_V_HEADS, DN_HEAD_DIM, DN_HEAD_DIM), jnp.float32)
    oc_f, sc_f = chunk_gated_delta(qc, kc, vc, gc[:, :, 0], bc[:, :, 0], s0)
    ol_f, _ = chunk_gated_delta(ql, kl, vl, gl[:, :, 0], bl[:, :, 0], sc_f)
    oc_b, sc_b = chunk_gated_delta(rev(qc), rev(kc), rev(vc), rev(gc[:, :, 1]), rev(bc[:, :, 1]), s0)
    ol_b, _ = chunk_gated_delta(rev(ql), rev(kl), rev(vl), rev(gl[:, :, 1]), rev(bl[:, :, 1]), sc_b)

    def out(o, z):
        o = rms_norm(o, norm_g) * jax.nn.silu(z)
        return o.reshape(o.shape[:2] + (nv,)) @ w_o

    yl = out(ol_f + rev(ol_b), zl)
    yc = out(oc_f + rev(oc_b), zc) if ctx_out else None
    return yl, yc


def moe_ffn(h, w_r, b_r, w1, b1, w2, b2):
    n_tok, d = h.shape
    logits = (h @ w_r + b_r).astype(jnp.float32)
    top_logit, top_idx = lax.top_k(logits, TOP_K)
    top_w = jax.nn.softmax(top_logit, axis=-1).astype(h.dtype)
    n_assign = n_tok * TOP_K
    flat_e = top_idx.reshape(-1)
    flat_w = top_w.reshape(-1)
    flat_tok = jnp.arange(n_assign, dtype=jnp.int32) // TOP_K
    order = jnp.argsort(flat_e)
    sorted_e = flat_e[order]
    counts = jnp.bincount(flat_e, length=N_EXPERTS)
    starts = jnp.cumsum(counts) - counts
    padded = (counts + EXPERT_BLOCK - 1) // EXPERT_BLOCK * EXPERT_BLOCK
    pends = jnp.cumsum(padded)
    pstarts = pends - padded
    dest = pstarts[sorted_e] + jnp.arange(n_assign, dtype=jnp.int32) - starts[sorted_e]
    n_rows = (-(-n_assign // EXPERT_BLOCK) + N_EXPERTS) * EXPERT_BLOCK
    n_blocks = n_rows // EXPERT_BLOCK
    row_tok = jnp.full((n_rows,), n_tok, jnp.int32).at[dest].set(flat_tok[order])
    row_w = jnp.zeros((n_rows,), h.dtype).at[dest].set(flat_w[order])
    block_e = jnp.minimum(jnp.searchsorted(pends, jnp.arange(n_blocks) * EXPERT_BLOCK, side='right'),
                          N_EXPERTS - 1)
    h_pad = jnp.concatenate([h, jnp.zeros((1, d), h.dtype)], axis=0)

    def expert_block(args):
        tok, e = args
        hh = h_pad[tok] @ w1[e] + b1[e]
        gate = jnp.minimum(hh[:, :D_FF], SWIGLU_LIMIT)
        up = jnp.clip(hh[:, D_FF:], -SWIGLU_LIMIT, SWIGLU_LIMIT)
        act = (up + 1.0) * gate * jax.nn.sigmoid(SWIGLU_ALPHA * gate)
        return act @ w2[e] + b2[e]

    rows = lax.map(expert_block, (row_tok.reshape(n_blocks, EXPERT_BLOCK), block_e))
    rows = rows.reshape(n_rows, d) * row_w[:, None]
    return jnp.zeros((n_tok + 1, d), h.dtype).at[row_tok].add(rows)[:n_tok]


def setup_inputs(seed: int = 0) -> dict:
    key = jax.random.key(seed)
    ks = list(jax.random.split(key, 40))
    D = D_MODEL

    def nrm(i, shape, scale):
        return jax.random.normal(ks[i], shape, jnp.float32) * scale

    na, nb, nc, nd = [len(range(m, DEPTH, N_MIXERS)) for m in range(N_MIXERS)]
    nq = SWA_HEADS * SWA_HEAD_DIM
    n_swa = nq + 2 * SWA_KV_HEADS * SWA_HEAD_DIM
    nv = DN_V_HEADS * DN_HEAD_DIM
    n_qkvz = 2 * DN_QK_HEADS * DN_HEAD_DIM + 2 * nv
    n_conv = 2 * DN_QK_HEADS * DN_HEAD_DIM + nv
    a_log = jnp.log(jax.random.uniform(ks[33], (nd, 2, DN_V_HEADS), jnp.float32, minval=1.0, maxval=16.0))
    dt = jnp.exp(jax.random.uniform(ks[34], (nd, 2, DN_V_HEADS), jnp.float32,
                                    minval=math.log(1e-3), maxval=math.log(1e-1)))
    dt_bias = dt + jnp.log(-jnp.expm1(-dt))
    return {
        'x': nrm(0, (BATCH, SEQ, D), 1.0),
        'c': nrm(1, (BATCH, D), 1.0),
        'ctx': nrm(2, (BATCH, CTX_LEN, D), 1.0),
        'c_ctx': nrm(3, (D,), 1.0),
        'mod_w': nrm(4, (DEPTH, D, 6 * D), 0.5 * D ** -0.5),
        'mod_b': nrm(5, (DEPTH, 6 * D), 0.02),
        'ln1_g': 1.0 + nrm(6, (DEPTH, D), 0.02),
        'ln1_b': nrm(7, (DEPTH, D), 0.02),
        'ln2_g': 1.0 + nrm(8, (DEPTH, D), 0.02),
        'ln2_b': nrm(9, (DEPTH, D), 0.02),
        'router_w': nrm(10, (DEPTH, D, N_EXPERTS), D ** -0.5),
        'router_b': nrm(11, (DEPTH, N_EXPERTS), 0.01),
        'exp_w1': nrm(12, (DEPTH, N_EXPERTS, D, 2 * D_FF), D ** -0.5),
        'exp_b1': nrm(13, (DEPTH, N_EXPERTS, 2 * D_FF), 0.01),
        'exp_w2': nrm(14, (DEPTH, N_EXPERTS, D_FF, D), DEEPNORM_BETA * D_FF ** -0.5),
        'exp_b2': nrm(15, (DEPTH, N_EXPERTS, D), 0.01),
        'conv_in_w': nrm(16, (na, D, 3 * D), D ** -0.5),
        'conv_w': nrm(17, (na, SC_WIDTH, D), SC_WIDTH ** -0.5),
        'conv_out_w': nrm(18, (na, D, D), DEEPNORM_BETA * D ** -0.5),
        'swa_qkv_w': nrm(19, (nb, D, n_swa), D ** -0.5),
        'swa_qkv_b': nrm(20, (nb, n_swa), 0.01),
        'swa_sink': nrm(21, (nb, SWA_HEADS), 0.5),
        'swa_out_w': nrm(22, (nb, nq, D), DEEPNORM_BETA * nq ** -0.5),
        'swa_out_b': nrm(23, (nb, D), 0.01),
        'diff_qkv_w': nrm(24, (nc, D, 3 * D), D ** -0.5),
        'diff_lambda': nrm(25, (nc, 4, DIFF_HEAD_DIM), 0.1),
        'diff_subln_g': 1.0 + nrm(26, (nc, 2 * DIFF_HEAD_DIM), 0.02),
        'diff_out_w': nrm(27, (nc, D, D), DEEPNORM_BETA * D ** -0.5),
        'delta_qkvz_w': nrm(28, (nd, D, n_qkvz), D ** -0.5),
        'delta_ba_w': nrm(29, (nd, D, 4 * DN_V_HEADS), D ** -0.5),
        'delta_a_log': a_log,
        'delta_dt_bias': dt_bias,
        'delta_conv_w': nrm(30, (nd, DN_CONV, n_conv), DN_CONV ** -0.5),
        'delta_norm_g': 1.0 + nrm(31, (nd, DN_HEAD_DIM), 0.02),
        'delta_out_w': nrm(32, (nd, nv, D), DEEPNORM_BETA * nv ** -0.5),
    }


def reference(x, c, ctx, c_ctx, mod_w, mod_b, ln1_g, ln1_b, ln2_g, ln2_b,
              router_w, router_b, exp_w1, exp_b1, exp_w2, exp_b2,
              conv_in_w, conv_w, conv_out_w,
              swa_qkv_w, swa_qkv_b, swa_sink, swa_out_w, swa_out_b,
              diff_qkv_w, diff_lambda, diff_subln_g, diff_out_w,
              delta_qkvz_w, delta_ba_w, delta_a_log, delta_dt_bias, delta_conv_w, delta_norm_g, delta_out_w):
    n_lat = x.shape[1]
    rows = n_lat // GRID_W
    cos_swa, sin_swa = axial_rope_tables(rows, SWA_HEAD_DIM)
    cos_diff, sin_diff = axial_rope_tables(rows, DIFF_HEAD_DIM)
    silu_c = jax.nn.silu(c)
    silu_cc = jax.nn.silu(c_ctx)
    xl, xc = x, ctx
    for i in range(DEPTH):
        last = i == DEPTH - 1
        kind, j = i % N_MIXERS, i // N_MIXERS
        mod_l = jnp.split((silu_c @ mod_w[i] + mod_b[i])[:, None, :], 6, axis=-1)
        mod_c = jnp.split(silu_cc @ mod_w[i] + mod_b[i], 6, axis=-1)
        hl = modulate(xl, mod_l[0], mod_l[1])
        hc = modulate(xc, mod_c[0], mod_c[1])
        if kind == 0:
            yl, yc = short_conv_mixer(hl, hc, conv_in_w[j], conv_w[j], conv_out_w[j], not last)
        elif kind == 1:
            yl, yc = swa_mixer(hl, hc, swa_qkv_w[j], swa_qkv_b[j], swa_sink[j], swa_out_w[j], swa_out_b[j],
                               cos_swa, sin_swa, not last)
        elif kind == 2:
            lam_init = 0.8 - 0.6 * math.exp(-0.3 * i)
            yl, yc = diff_mixer(hl, hc, diff_qkv_w[j], diff_lambda[j], diff_subln_g[j], diff_out_w[j],
                                lam_init, cos_diff, sin_diff, not last)
        else:
            yl, yc = delta_mixer(hl, hc, delta_qkvz_w[j], delta_ba_w[j], delta_a_log[j], delta_dt_bias[j],
                                 delta_conv_w[j], delta_norm_g[j], delta_out_w[j], not last)
        xl = layer_norm(DEEPNORM_ALPHA * xl + mod_l[2] * yl, ln1_g[i], ln1_b[i])
        if not last:
            xc = layer_norm(DEEPNORM_ALPHA * xc + mod_c[2] * yc, ln1_g[i], ln1_b[i])
        moe_args = (router_w[i], router_b[i], exp_w1[i], exp_b1[i], exp_w2[i], exp_b2[i])
        hl = modulate(xl, mod_l[3], mod_l[4]).reshape(-1, D_MODEL)
        if last:
            fl = moe_ffn(hl, *moe_args)
        else:
            hc = modulate(xc, mod_c[3], mod_c[4]).reshape(-1, D_MODEL)
            f = moe_ffn(jnp.concatenate([hl, hc], axis=0), *moe_args)
            fl, fc = f[:hl.shape[0]], f[hl.shape[0]:]
            xc = layer_norm(DEEPNORM_ALPHA * xc + mod_c[5] * fc.reshape(xc.shape), ln2_g[i], ln2_b[i])
        xl = layer_norm(DEEPNORM_ALPHA * xl + mod_l[5] * fl.reshape(xl.shape), ln2_g[i], ln2_b[i])
    return xl
```

```python
import functools
import math

import jax
import jax.numpy as jnp
from jax import lax
from jax.experimental import pallas as pl
from jax.experimental.pallas import tpu as pltpu

F32 = jnp.float32
BF16 = jnp.bfloat16

GRID_W = 64
N_MIXERS = 4
LN_EPS = 1e-5
RMS_EPS = 1e-6
NEG_INF = -1e30
ROPE_BASE = 10000.0
SC_WIDTH = 3
SWA_HEADS = 16
SWA_KV_HEADS = 4
SWA_HEAD_DIM = 64
SWA_WINDOW = 128
DIFF_HEADS = 8
DIFF_HEAD_DIM = 64
DN_QK_HEADS = 8
DN_V_HEADS = 16
DN_HEAD_DIM = 128
DN_CONV = 5
N_EXPERTS = 32
TOP_K = 4
SWIGLU_LIMIT = 7.0
SWIGLU_ALPHA = 1.702

LANES = 128
HALO_ROWS = 16
VMEM_LIMIT = 56 * 1024 * 1024

ROW_TILE = 256
ATT_BLOCK = 128
DN_CHUNK = 128
EXPERT_TILE = 256
COL_CHUNK = 512


def _cparams(*sem):
    return pltpu.CompilerParams(dimension_semantics=sem, vmem_limit_bytes=VMEM_LIMIT)


def _split3(x):
    hi = x.astype(BF16)
    r1 = x - hi.astype(F32)
    mid = r1.astype(BF16)
    lo = (r1 - mid.astype(F32)).astype(BF16)
    return hi, mid, lo


def _dot(a, b):
    return jnp.dot(a, b, preferred_element_type=F32)


def _dot_nt(a, b):
    return lax.dot_general(a, b, (((1,), (1,)), ((), ())), preferred_element_type=F32)


def _dot_tn(a, b):
    return lax.dot_general(a, b, (((0,), (0,)), ((), ())), preferred_element_type=F32)


def _silu(x):
    return x * jax.nn.sigmoid(x)


def _seg_of_block(j, ctx_blocks):
    return jnp.where(j < ctx_blocks, 0, 1)


def _mods_kernel(c_ref, w_ref, b_ref, o_ref):
    s = _silu(c_ref[...]).astype(BF16)
    o_ref[0] = _dot(s, w_ref[0].astype(BF16)) + b_ref[0]


def _mod_vectors(cc, mod_w, mod_b):
    depth, d, n = mod_w.shape
    rows = cc.shape[0]
    tn = 1536
    return pl.pallas_call(
        _mods_kernel,
        grid=(depth, n // tn),
        in_specs=[pl.BlockSpec((rows, d), lambda l, j: (0, 0)),
                  pl.BlockSpec((1, d, tn), lambda l, j: (l, 0, j)),
                  pl.BlockSpec((1, 1, tn), lambda l, j: (l, 0, j))],
        out_specs=pl.BlockSpec((1, rows, tn), lambda l, j: (l, 0, j)),
        out_shape=jax.ShapeDtypeStruct((depth, rows, n), F32),
        compiler_params=_cparams("parallel", "parallel"),
    )(cc, mod_w, mod_b.reshape(depth, 1, n))


def _proj_kernel(*refs, plan, n_out, use_rope):
    x_ref, mod_ref, w_ref, b_ref = refs[:4]
    pos = 4
    if use_rope:
        cos_ref, sin_ref = refs[4:6]
        pos = 6
    outs = refs[pos:pos + n_out]
    shift = mod_ref[0, 0, 0:1, :]
    scale = mod_ref[0, 0, 1:2, :]
    h = (x_ref[0] * (1.0 + scale) + shift).astype(BF16)

    def acc(col, width):
        return _dot(h, w_ref[:, col:col + width]) + b_ref[:, col:col + width]

    for kind, oi, ocol, width, wcol, wcol2, mult in plan:
        for c0 in range(0, width, COL_CHUNK):
            cw = min(COL_CHUNK, width - c0)
            a = acc(wcol + c0, cw)
            if kind == "rope":
                reps = cw // LANES
                cos = jnp.tile(cos_ref[...], (1, reps))
                sin = jnp.tile(sin_ref[...], (1, reps))
                a = (a * cos + acc(wcol2 + c0, cw) * sin) * mult
            elif kind == "mul":
                a = a * acc(wcol2 + c0, cw)
            outs[oi][0, :, ocol + c0:ocol + c0 + cw] = a.astype(outs[oi].dtype)


def _project(x, modtab, w, b, plan, out_widths, out_dtypes, ctx_len, rope=None):
    bsz, t, d = x.shape
    n = w.shape[1]
    tm = ROW_TILE
    ctx_blocks = ctx_len // tm
    mt = modtab
    in_specs = [pl.BlockSpec((1, tm, d), lambda i, j: (i, j, 0)),
                pl.BlockSpec((1, 1, 8, d), lambda i, j: (i, _seg_of_block(j, ctx_blocks), 0, 0)),
                pl.BlockSpec((d, n), lambda i, j: (0, 0)),
                pl.BlockSpec((1, n), lambda i, j: (0, 0))]
    args = [x, mt, w, b]
    if rope is not None:
        in_specs += [pl.BlockSpec((tm, LANES), lambda i, j: (j, 0))] * 2
        args += list(rope)
    out_specs = [pl.BlockSpec((1, tm, ow), lambda i, j: (i, j, 0)) for ow in out_widths]
    out_shape = [jax.ShapeDtypeStruct((bsz, t, ow), dt) for ow, dt in zip(out_widths, out_dtypes)]
    return pl.pallas_call(
        functools.partial(_proj_kernel, plan=tuple(plan), n_out=len(out_widths), use_rope=rope is not None),
        grid=(bsz, t // tm),
        in_specs=in_specs, out_specs=out_specs, out_shape=out_shape,
        compiler_params=_cparams("parallel", "parallel"),
    )(*args)


def _resid_ln(x, y, gate, g, b, alpha):
    r = alpha * x + gate * y
    mu = jnp.mean(r, axis=-1, keepdims=True)
    rc = r - mu
    var = jnp.mean(rc * rc, axis=-1, keepdims=True)
    return rc * lax.rsqrt(var + LN_EPS) * g + b


def _shift_rows(p, k, edge_rows, valid):
    tm = p.shape[0]
    rolled = pltpu.roll(p, k % tm, axis=0)
    row = lax.broadcasted_iota(jnp.int32, p.shape, 0)
    out = rolled
    if k > 0:
        for r in range(k):
            fill = jnp.where(valid, edge_rows[edge_rows.shape[0] - k + r:edge_rows.shape[0] - k + r + 1, :], 0.0)
            out = jnp.where(row == r, fill, out)
    else:
        for r in range(-k):
            fill = jnp.where(valid, edge_rows[r:r + 1, :], 0.0)
            out = jnp.where(row == tm + k + r, fill, out)
    return out


def _centred_conv(p, prev, nxt, w_ref, width, has_prev, has_next):
    pad = (width - 1) // 2
    acc = p * w_ref[pad:pad + 1, :]
    for k in range(1, pad + 1):
        acc = acc + _shift_rows(p, k, prev, has_prev) * w_ref[pad - k:pad - k + 1, :]
        acc = acc + _shift_rows(p, -k, nxt, has_next) * w_ref[pad + k:pad + k + 1, :]
    return acc


def _seg_edges(j, ctx_blocks, n_blocks):
    has_prev = jnp.logical_and(j != 0, j != ctx_blocks)
    has_next = jnp.logical_and(j != ctx_blocks - 1, j != n_blocks - 1)
    return has_prev, has_next


def _out_plain_kernel(x_ref, mod_ref, o_ref, w_ref, b_ref, g_ref, bb_ref, out_ref, *, alpha):
    y = _dot(o_ref[0], w_ref[...]) + b_ref[...]
    out_ref[0] = _resid_ln(x_ref[0], y, mod_ref[0, 0, 2:3, :], g_ref[...], bb_ref[...], alpha)


def _out_conv_kernel(x_ref, mod_ref, bg_ref, p_ref, pp_ref, pn_ref, cw_ref, w_ref, b_ref, g_ref, bb_ref,
                     out_ref, *, alpha, ctx_blocks, n_blocks):
    j = pl.program_id(1)
    has_prev, has_next = _seg_edges(j, ctx_blocks, n_blocks)
    conv = _centred_conv(p_ref[0].astype(F32), pp_ref[0].astype(F32), pn_ref[0].astype(F32), cw_ref,
                         SC_WIDTH, has_prev, has_next)
    o = (bg_ref[0].astype(F32) * conv).astype(BF16)
    y = _dot(o, w_ref[...]) + b_ref[...]
    out_ref[0] = _resid_ln(x_ref[0], y, mod_ref[0, 0, 2:3, :], g_ref[...], bb_ref[...], alpha)


def _out_delta_kernel(x_ref, mod_ref, of_ref, ob_ref, z_ref, ng_ref, w_ref, b_ref, g_ref, bb_ref,
                      out_ref, *, alpha):
    hd = DN_HEAD_DIM
    ng = ng_ref[...]
    acc = None
    for h0 in range(0, of_ref.shape[2], hd):
        o = of_ref[0, :, h0:h0 + hd] + ob_ref[0, :, h0:h0 + hd]
        o = o * lax.rsqrt(jnp.mean(o * o, axis=-1, keepdims=True) + RMS_EPS) * ng
        o = (o * _silu(z_ref[0, :, h0:h0 + hd].astype(F32))).astype(BF16)
        part = _dot(o, w_ref[h0:h0 + hd, :])
        acc = part if acc is None else acc + part
    y = acc + b_ref[...]
    out_ref[0] = _resid_ln(x_ref[0], y, mod_ref[0, 0, 2:3, :], g_ref[...], bb_ref[...], alpha)


def _mixer_out(kind, x, modtab, acts, w_o, b_o, ln_g, ln_b, alpha, ctx_len, extra=None):
    bsz, t, d = x.shape
    tm = ROW_TILE
    nb = t // tm
    ctx_blocks = ctx_len // tm
    kin = w_o.shape[0]
    row = lambda i, j: (i, j, 0)
    const2 = lambda i, j: (0, 0)
    x_spec = pl.BlockSpec((1, tm, d), row)
    mod_spec = pl.BlockSpec((1, 1, 8, d), lambda i, j: (i, _seg_of_block(j, ctx_blocks), 0, 0))
    tail_specs = [pl.BlockSpec((kin, d), const2), pl.BlockSpec((1, d), const2),
                  pl.BlockSpec((1, d), const2), pl.BlockSpec((1, d), const2)]
    tail_args = [w_o, b_o, ln_g, ln_b]
    if kind == "plain":
        body = functools.partial(_out_plain_kernel, alpha=alpha)
        in_specs = [x_spec, mod_spec, pl.BlockSpec((1, tm, kin), row)] + tail_specs
        args = [x, modtab, acts[0]] + tail_args
    elif kind == "conv":
        body = functools.partial(_out_conv_kernel, alpha=alpha, ctx_blocks=ctx_blocks, n_blocks=nb)
        hb = tm // HALO_ROWS
        last_halo = t // HALO_ROWS - 1
        in_specs = [x_spec, mod_spec, pl.BlockSpec((1, tm, d), row), pl.BlockSpec((1, tm, d), row),
                    pl.BlockSpec((1, HALO_ROWS, d), lambda i, j: (i, jnp.maximum(j * hb - 1, 0), 0)),
                    pl.BlockSpec((1, HALO_ROWS, d), lambda i, j: (i, jnp.minimum((j + 1) * hb, last_halo), 0)),
                    pl.BlockSpec((8, d), const2)] + tail_specs
        args = [x, modtab, acts[0], acts[1], acts[1], acts[1], extra] + tail_args
    else:
        body = functools.partial(_out_delta_kernel, alpha=alpha)
        in_specs = [x_spec, mod_spec, pl.BlockSpec((1, tm, kin), row), pl.BlockSpec((1, tm, kin), row),
                    pl.BlockSpec((1, tm, kin), row), pl.BlockSpec((1, DN_HEAD_DIM), const2)] + tail_specs
        args = [x, modtab, acts[0], acts[1], acts[2], extra] + tail_args
    return pl.pallas_call(
        body, grid=(bsz, nb), in_specs=in_specs,
        out_specs=pl.BlockSpec((1, tm, d), row),
        out_shape=jax.ShapeDtypeStruct((bsz, t, d), F32),
        compiler_params=_cparams("parallel", "parallel"),
    )(*args)


def _swa_kernel(sink_ref, q_ref, kp_ref, ko_ref, kn_ref, kc_ref, vp_ref, vo_ref, vn_ref, vc_ref, o_ref,
                *, n_lat_blocks, ctx_blocks):
    j = pl.program_id(1)
    blk = ATT_BLOCK
    is_ctx = j < ctx_blocks
    lat = j - ctx_blocks
    r = lax.broadcasted_iota(jnp.int32, (blk, blk), 0)
    c = lax.broadcasted_iota(jnp.int32, (blk, blk), 1)
    lat_ok = jnp.logical_not(is_ctx)
    m_prev = jnp.logical_and(jnp.logical_and(c >= r, lat > 0), lat_ok)
    m_own = jnp.logical_and(r == r, lat_ok)
    m_next = jnp.logical_and(jnp.logical_and(c <= r, lat < n_lat_blocks - 1), lat_ok)
    lane = lax.broadcasted_iota(jnp.int32, (blk, LANES), 1)
    lo = lane < SWA_HEAD_DIM
    group = SWA_HEADS // SWA_KV_HEADS
    for pair in range(SWA_HEADS // 2):
        g = (2 * pair) // group
        gs = slice(g * LANES, (g + 1) * LANES)
        qp = q_ref[0, :, pair * LANES:(pair + 1) * LANES]
        zero = jnp.zeros_like(qp)
        halves = []
        for half in range(2):
            qh = jnp.where(lo, qp, zero) if half == 0 else jnp.where(lo, zero, qp)
            s_p = jnp.where(m_prev, _dot_nt(qh, kp_ref[0, :, gs]), NEG_INF)
            s_o = jnp.where(m_own, _dot_nt(qh, ko_ref[0, :, gs]), NEG_INF)
            s_n = jnp.where(m_next, _dot_nt(qh, kn_ref[0, :, gs]), NEG_INF)
            s_c = _dot_nt(qh, kc_ref[0, :, gs])
            sink = sink_ref[2 * pair + half]
            m = jnp.maximum(jnp.maximum(jnp.max(s_p, axis=-1, keepdims=True), jnp.max(s_o, axis=-1, keepdims=True)),
                            jnp.maximum(jnp.max(s_n, axis=-1, keepdims=True), jnp.max(s_c, axis=-1, keepdims=True)))
            m = jnp.maximum(m, sink)
            e_p, e_o, e_n, e_c = jnp.exp(s_p - m), jnp.exp(s_o - m), jnp.exp(s_n - m), jnp.exp(s_c - m)
            den = (jnp.sum(e_p, axis=-1, keepdims=True) + jnp.sum(e_o, axis=-1, keepdims=True)
                   + jnp.sum(e_n, axis=-1, keepdims=True) + jnp.sum(e_c, axis=-1, keepdims=True)
                   + jnp.exp(sink - m))
            inv = 1.0 / den
            acc = _dot((e_p * inv).astype(BF16), vp_ref[0, :, gs])
            acc = acc + _dot((e_o * inv).astype(BF16), vo_ref[0, :, gs])
            acc = acc + _dot((e_n * inv).astype(BF16), vn_ref[0, :, gs])
            acc = acc + _dot((e_c * inv).astype(BF16), vc_ref[0, :, gs])
            halves.append(acc)
        o_ref[0, :, pair * LANES:(pair + 1) * LANES] = jnp.where(lo, halves[0], halves[1]).astype(o_ref.dtype)


def _swa_attention(q, kk, vv, sink, ctx_len):
    bsz, t, dq = q.shape
    dk = kk.shape[2]
    blk = ATT_BLOCK
    nb = t // blk
    ctx_blocks = ctx_len // blk
    n_lat_blocks = nb - ctx_blocks

    def prev_map(i, j, s):
        return (i, jnp.clip(j - 1, ctx_blocks, nb - 1), 0)

    def own_map(i, j, s):
        return (i, j, 0)

    def next_map(i, j, s):
        return (i, jnp.clip(j + 1, ctx_blocks, nb - 1), 0)

    def ctx_map(i, j, s):
        return (i, 0, 0)

    kv_specs = [pl.BlockSpec((1, blk, dk), prev_map), pl.BlockSpec((1, blk, dk), own_map),
                pl.BlockSpec((1, blk, dk), next_map), pl.BlockSpec((1, ctx_len, dk), ctx_map)]
    return pl.pallas_call(
        functools.partial(_swa_kernel, n_lat_blocks=n_lat_blocks, ctx_blocks=ctx_blocks),
        grid_spec=pltpu.PrefetchScalarGridSpec(
            num_scalar_prefetch=1, grid=(bsz, nb),
            in_specs=[pl.BlockSpec((1, blk, dq), own_map)] + kv_specs + kv_specs,
            out_specs=pl.BlockSpec((1, blk, dq), own_map)),
        out_shape=jax.ShapeDtypeStruct((bsz, t, dq), BF16),
        compiler_params=_cparams("parallel", "parallel"),
    )(sink, q, kk, kk, kk, kk, vv, vv, vv, vv)


def _diff_kernel(lam_ref, q_ref, k_ref, v_ref, g_ref, o_ref, *, ctx_len, lam_init):
    j = pl.program_id(2)
    blk = ATT_BLOCK
    ctx_blocks = ctx_len // blk
    lp = lam_ref[...]
    lam = (jnp.exp(jnp.sum(lp[0:1, :] * lp[1:2, :], axis=-1, keepdims=True))
           - jnp.exp(jnp.sum(lp[2:3, :] * lp[3:4, :], axis=-1, keepdims=True)) + lam_init)
    lane = lax.broadcasted_iota(jnp.int32, (blk, LANES), 1)
    lo = lane < DIFF_HEAD_DIM
    qp = q_ref[0]
    zero = jnp.zeros_like(qp)

    def attend(n_keys):
        k = k_ref[0, 0:n_keys, :]
        v = v_ref[0, 0:n_keys, :]
        ps = []
        for half in range(2):
            qh = jnp.where(lo, qp, zero) if half == 0 else jnp.where(lo, zero, qp)
            s = _dot_nt(qh, k)
            m = jnp.max(s, axis=-1, keepdims=True)
            e = jnp.exp(s - m)
            ps.append(e * (1.0 / jnp.sum(e, axis=-1, keepdims=True)))
        a = (ps[0] - lam * ps[1]).astype(BF16)
        o = _dot(a, v)
        o = o * lax.rsqrt(jnp.mean(o * o, axis=-1, keepdims=True) + RMS_EPS) * g_ref[...]
        o_ref[0] = (o * (1.0 - lam_init)).astype(o_ref.dtype)

    @pl.when(j < ctx_blocks)
    def _():
        attend(ctx_len)

    @pl.when(j >= ctx_blocks)
    def _():
        attend(k_ref.shape[1])


def _diff_attention(q, k, v, lam_p, subln_g, ctx_len, lam_init):
    bsz, t, dq = q.shape
    blk = ATT_BLOCK
    nh = dq // LANES
    qmap = lambda i, h, j: (i, j, h)
    kmap = lambda i, h, j: (i, 0, h)
    const2 = lambda i, h, j: (0, 0)
    return pl.pallas_call(
        functools.partial(_diff_kernel, ctx_len=ctx_len, lam_init=lam_init),
        grid=(bsz, nh, t // blk),
        in_specs=[pl.BlockSpec((8, DIFF_HEAD_DIM), const2),
                  pl.BlockSpec((1, blk, LANES), qmap),
                  pl.BlockSpec((1, t, LANES), kmap),
                  pl.BlockSpec((1, t, LANES), kmap),
                  pl.BlockSpec((1, LANES), const2)],
        out_specs=pl.BlockSpec((1, blk, LANES), qmap),
        out_shape=jax.ShapeDtypeStruct((bsz, t, dq), BF16),
        compiler_params=_cparams("parallel", "parallel", "parallel"),
    )(jnp.pad(lam_p, ((0, 4), (0, 0))), q, k, v, subln_g.reshape(1, LANES))


def _delta_prep_kernel(x_ref, xp_ref, xn_ref, cw_ref, ba_ref, alog_ref, dtb_ref,
                       q_ref, k_ref, v_ref, gb_ref, *, ctx_blocks, n_blocks, nqk):
    j = pl.program_id(1)
    has_prev, has_next = _seg_edges(j, ctx_blocks, n_blocks)
    hd = DN_HEAD_DIM
    tm = x_ref.shape[1]
    for c0 in range(0, x_ref.shape[2], COL_CHUNK):
        cs = slice(c0, c0 + COL_CHUNK)
        conv = _centred_conv(x_ref[0, :, cs].astype(F32), xp_ref[0, :, cs].astype(F32),
                             xn_ref[0, :, cs].astype(F32), cw_ref.at[:, cs], DN_CONV, has_prev, has_next)
        a = _silu(conv)
        for h0 in range(0, COL_CHUNK, hd):
            col = c0 + h0
            ah = a[:, h0:h0 + hd]
            if col < 2 * nqk:
                ah = ah * lax.rsqrt(jnp.sum(ah * ah, axis=-1, keepdims=True) + RMS_EPS)
                if col < nqk:
                    q_ref[0, :, col:col + hd] = (ah * (hd ** -0.5)).astype(q_ref.dtype)
                else:
                    k_ref[0, :, col - nqk:col - nqk + hd] = ah.astype(k_ref.dtype)
            else:
                v_ref[0, :, col - 2 * nqk:col - 2 * nqk + hd] = ah.astype(v_ref.dtype)
    ba = ba_ref[0]
    lane = lax.broadcasted_iota(jnp.int32, ba.shape, 1)
    is_beta = (lane % 32) < DN_V_HEADS
    z = ba + dtb_ref[...]
    softplus = jnp.maximum(z, 0.0) + jnp.log(1.0 + jnp.exp(-jnp.abs(z)))
    g = -jnp.exp(alog_ref[...]) * softplus
    r = lax.broadcasted_iota(jnp.int32, (tm, tm), 0)
    c = lax.broadcasted_iota(jnp.int32, (tm, tm), 1)
    same = (r // DN_CHUNK) == (c // DN_CHUNK)
    tri_f = jnp.where(jnp.logical_and(same, c <= r), 1.0, 0.0).astype(BF16)
    tri_b = jnp.where(jnp.logical_and(same, c >= r), 1.0, 0.0).astype(BF16)
    parts = _split3(g)
    cum_f = _dot(tri_f, parts[0]) + _dot(tri_f, parts[1]) + _dot(tri_f, parts[2])
    cum_b = _dot(tri_b, parts[0]) + _dot(tri_b, parts[1]) + _dot(tri_b, parts[2])
    gcum = jnp.where(lane < 32, cum_f, cum_b)
    gb_ref[0] = jnp.where(is_beta, jax.nn.sigmoid(ba), gcum)


def _delta_prep(qkv_pre, conv_w, ba, alog_l, dtb_l, ctx_len, nqk, nv):
    bsz, t, nc = qkv_pre.shape
    tm = ROW_TILE
    nb = t // tm
    ctx_blocks = ctx_len // tm
    hb = tm // HALO_ROWS
    last_halo = t // HALO_ROWS - 1
    row = lambda i, j: (i, j, 0)
    const2 = lambda i, j: (0, 0)
    return pl.pallas_call(
        functools.partial(_delta_prep_kernel, ctx_blocks=ctx_blocks, n_blocks=nb, nqk=nqk),
        grid=(bsz, nb),
        in_specs=[pl.BlockSpec((1, tm, nc), row),
                  pl.BlockSpec((1, HALO_ROWS, nc), lambda i, j: (i, jnp.maximum(j * hb - 1, 0), 0)),
                  pl.BlockSpec((1, HALO_ROWS, nc), lambda i, j: (i, jnp.minimum((j + 1) * hb, last_halo), 0)),
                  pl.BlockSpec((8, nc), const2),
                  pl.BlockSpec((1, tm, LANES), row),
                  pl.BlockSpec((1, LANES), const2),
                  pl.BlockSpec((1, LANES), const2)],
        out_specs=[pl.BlockSpec((1, tm, nqk), row), pl.BlockSpec((1, tm, nqk), row),
                   pl.BlockSpec((1, tm, nv), row), pl.BlockSpec((1, tm, LANES), row)],
        out_shape=[jax.ShapeDtypeStruct((bsz, t, nqk), BF16), jax.ShapeDtypeStruct((bsz, t, nqk), BF16),
                   jax.ShapeDtypeStruct((bsz, t, nv), BF16), jax.ShapeDtypeStruct((bsz, t, LANES), F32)],
        compiler_params=_cparams("parallel", "parallel"),
    )(qkv_pre, qkv_pre, qkv_pre, conv_w, ba, alog_l, dtb_l)


DN_INV_BLOCK = 16


def _unit_lower_inverse(low, eye, r, c):
    n = low.shape[0]
    b = DN_INV_BLOCK
    diag = jnp.where((r // b) == (c // b), low, 0.0)
    x = eye - diag
    pw = diag
    for _ in range(int(math.log2(b)) - 1):
        pw_b = pw.astype(BF16)
        pw = _dot(pw_b, pw_b)
        x = x + _dot(x.astype(BF16), pw.astype(BF16))
    while b < n:
        off = jnp.where(jnp.logical_and((r // (2 * b)) == (c // (2 * b)), (r // b) != (c // b)), low, 0.0)
        xb = x.astype(BF16)
        x = x - _dot(xb, _dot(off.astype(BF16), xb).astype(BF16))
        b *= 2
    return x


def _delta_scan_kernel(q_ref, k_ref, v_ref, gb_ref, gt_ref, o_ref, s_ref, *, rep):
    d = pl.program_id(0)
    hq = pl.program_id(2)
    ck = DN_CHUNK
    n_chunks = q_ref.shape[2] // ck
    hd = DN_HEAD_DIM
    s_ref[...] = jnp.zeros_like(s_ref)
    r = lax.broadcasted_iota(jnp.int32, (ck, ck), 0)
    c = lax.broadcasted_iota(jnp.int32, (ck, ck), 1)
    incl = r >= c
    strict = r > c
    eye = jnp.where(r == c, 1.0, 0.0)
    lane = lax.broadcasted_iota(jnp.int32, (ck, LANES), 1)

    def chunk(ci, carry):
        rows = pl.ds(pl.multiple_of(ci * ck, ck), ck)
        q = q_ref[0, 0, rows, :]
        k = k_ref[0, 0, rows, :]
        kk = _dot_nt(k, k)
        qk = _dot_nt(q, k)
        gb = gb_ref[0, 0, rows, :]
        qf = q.astype(F32)
        kf = k.astype(F32)
        for jh in range(rep):
            head = hq * rep + jh
            bcol = jnp.sum(jnp.where(lane == d * 32 + head, gb, 0.0), axis=-1, keepdims=True)
            gcol = jnp.sum(jnp.where(lane == d * 32 + DN_V_HEADS + head, gb, 0.0), axis=-1, keepdims=True)
            grow = gt_ref[0, 0, 0, jh:jh + 1, rows]
            glast = grow[:, ck - 1:ck]
            decay = jnp.where(incl, jnp.exp(jnp.where(incl, gcol - grow, 0.0)), 0.0)
            low = jnp.where(strict, kk * decay, 0.0) * bcol
            intra = jnp.where(incl, qk * decay, 0.0)
            tinv = _unit_lower_inverse(low, eye, r, c)
            v = v_ref[0, 0, rows, jh * hd:(jh + 1) * hd].astype(F32)
            eg = jnp.exp(gcol)
            rhs = jnp.concatenate([v * bcol, kf * (bcol * eg)], axis=1).astype(BF16)
            uw = _dot(tinv.astype(BF16), rhs)
            u, w = uw[:, :hd], uw[:, hd:]
            s = s_ref[jh]
            lhs = jnp.concatenate([w, qf * eg], axis=0).astype(BF16)
            ws = _dot(lhs, s.astype(BF16))
            v_new = u - ws[:ck]
            vb = v_new.astype(BF16)
            o = ws[ck:] + _dot(intra.astype(BF16), vb)
            o_ref[0, 0, rows, jh * hd:(jh + 1) * hd] = o.astype(o_ref.dtype)
            kd = (kf * jnp.exp(glast - gcol)).astype(BF16)
            s_ref[jh] = s * jnp.exp(glast) + _dot_tn(kd, vb)
        return carry

    lax.fori_loop(0, n_chunks, chunk, 0)


def _delta_scan(q2, k2, v2, gb2, gt2):
    _, bsz, t, nqk = q2.shape
    nv = v2.shape[3]
    hd = DN_HEAD_DIM
    nh = nqk // hd
    rep = nv // nqk
    qmap = lambda d, i, h: (d, i, 0, h)
    return pl.pallas_call(
        functools.partial(_delta_scan_kernel, rep=rep),
        grid=(2, bsz, nh),
        in_specs=[pl.BlockSpec((1, 1, t, hd), qmap), pl.BlockSpec((1, 1, t, hd), qmap),
                  pl.BlockSpec((1, 1, t, rep * hd), qmap),
                  pl.BlockSpec((1, 1, t, LANES), lambda d, i, h: (d, i, 0, 0)),
                  pl.BlockSpec((1, 1, 1, 8, t), lambda d, i, h: (d, i, h, 0, 0))],
        out_specs=pl.BlockSpec((1, 1, t, rep * hd), qmap),
        out_shape=jax.ShapeDtypeStruct((2, bsz, t, nv), F32),
        scratch_shapes=[pltpu.VMEM((rep, hd, hd), F32)],
        compiler_params=_cparams("parallel", "parallel", "parallel"),
    )(q2, k2, v2, gb2, gt2)


def _router_kernel(x_ref, mod_ref, wr_ref, br_ref, h_ref, idx_ref, wt_ref):
    shift = mod_ref[0, 0, 3:4, :]
    scale = mod_ref[0, 0, 4:5, :]
    h = x_ref[0] * (1.0 + scale) + shift
    h_hi = h.astype(BF16)
    h_ref[0] = h_hi
    h_lo = (h - h_hi.astype(F32)).astype(BF16)
    w = wr_ref[...]
    w_hi = w.astype(BF16)
    w_lo = (w - w_hi.astype(F32)).astype(BF16)
    logits = _dot(h_hi, w_hi) + _dot(h_hi, w_lo) + _dot(h_lo, w_hi) + br_ref[...]
    lane = lax.broadcasted_iota(jnp.int32, logits.shape, 1)
    cur = jnp.where(lane < N_EXPERTS, logits, -jnp.inf)
    idx_out = jnp.zeros(logits.shape, jnp.int32)
    wt_out = jnp.zeros(logits.shape, F32)
    tops = []
    for kk in range(TOP_K):
        m = jnp.max(cur, axis=-1, keepdims=True)
        sel = jnp.min(jnp.where(cur == m, lane, LANES), axis=-1, keepdims=True)
        tops.append(m)
        idx_out = jnp.where(lane == kk, sel, idx_out)
        cur = jnp.where(lane == sel, -jnp.inf, cur)
    es = [jnp.exp(m - tops[0]) for m in tops]
    den = es[0] + es[1] + es[2] + es[3]
    for kk in range(TOP_K):
        wt_out = jnp.where(lane == kk, es[kk] / den, wt_out)
    idx_ref[0] = idx_out
    wt_ref[0] = wt_out


def _router(x, modtab, w_r, b_r, ctx_len, first_block):
    bsz, t, d = x.shape
    tm = ROW_TILE
    ctx_blocks = ctx_len // tm
    nb = t // tm - first_block
    row_in = lambda i, j: (i, j + first_block, 0)
    row = lambda i, j: (i, j, 0)
    const2 = lambda i, j: (0, 0)
    tout = nb * tm
    return pl.pallas_call(
        _router_kernel,
        grid=(bsz, nb),
        in_specs=[pl.BlockSpec((1, tm, d), row_in),
                  pl.BlockSpec((1, 1, 8, d), lambda i, j: (i, _seg_of_block(j + first_block, ctx_blocks), 0, 0)),
                  pl.BlockSpec((d, LANES), const2), pl.BlockSpec((1, LANES), const2)],
        out_specs=[pl.BlockSpec((1, tm, d), row), pl.BlockSpec((1, tm, LANES), row),
                   pl.BlockSpec((1, tm, LANES), row)],
        out_shape=[jax.ShapeDtypeStruct((bsz, tout, d), BF16), jax.ShapeDtypeStruct((bsz, tout, LANES), jnp.int32),
                   jax.ShapeDtypeStruct((bsz, tout, LANES), F32)],
        compiler_params=_cparams("parallel", "parallel"),
    )(x, modtab, w_r, b_r)


def _expert_kernel(be_ref, nused_ref, x_ref, w1_ref, b1_ref, w2_ref, b2_ref, y_ref):
    i = pl.program_id(0)
    dff = w2_ref.shape[1]

    @pl.when(i < nused_ref[0])
    def _():
        x = x_ref[...]
        acc = None
        for c0 in range(0, dff, COL_CHUNK):
            gate = _dot(x, w1_ref[0, :, c0:c0 + COL_CHUNK]) + b1_ref[0, :, c0:c0 + COL_CHUNK]
            up = _dot(x, w1_ref[0, :, dff + c0:dff + c0 + COL_CHUNK]) + b1_ref[0, :, dff + c0:dff + c0 + COL_CHUNK]
            gate = jnp.minimum(gate, SWIGLU_LIMIT)
            up = jnp.clip(up, -SWIGLU_LIMIT, SWIGLU_LIMIT)
            act = ((up + 1.0) * gate * jax.nn.sigmoid(SWIGLU_ALPHA * gate)).astype(BF16)
            part = _dot(act, w2_ref[0, c0:c0 + COL_CHUNK, :])
            acc = part if acc is None else acc + part
        y_ref[...] = (acc + b2_ref[0]).astype(y_ref.dtype)

    @pl.when(i >= nused_ref[0])
    def _():
        y_ref[...] = jnp.zeros_like(y_ref)


def _experts(xs, block_e, n_used, w1, b1, w2, b2):
    n_rows, d = xs.shape
    tm = EXPERT_TILE
    ne, _, dff2 = w1.shape
    dff = dff2 // 2
    return pl.pallas_call(
        _expert_kernel,
        grid_spec=pltpu.PrefetchScalarGridSpec(
            num_scalar_prefetch=2, grid=(n_rows // tm,),
            in_specs=[pl.BlockSpec((tm, d), lambda i, be, nu: (i, 0)),
                      pl.BlockSpec((1, d, dff2), lambda i, be, nu: (be[i], 0, 0)),
                      pl.BlockSpec((1, 1, dff2), lambda i, be, nu: (be[i], 0, 0)),
                      pl.BlockSpec((1, dff, d), lambda i, be, nu: (be[i], 0, 0)),
                      pl.BlockSpec((1, 1, d), lambda i, be, nu: (be[i], 0, 0))],
            out_specs=pl.BlockSpec((tm, d), lambda i, be, nu: (i, 0))),
        out_shape=jax.ShapeDtypeStruct((n_rows, d), BF16),
        compiler_params=_cparams("arbitrary"),
    )(block_e, n_used, xs, w1, b1.reshape(ne, 1, dff2), w2, b2.reshape(ne, 1, d))


def _combine_kernel(x_ref, mod_ref, z_ref, wt_ref, g_ref, b_ref, out_ref, *, alpha):
    wt = wt_ref[0]
    f = None
    for kk in range(TOP_K):
        term = z_ref[kk, 0].astype(F32) * wt[:, kk:kk + 1]
        f = term if f is None else f + term
    out_ref[0] = _resid_ln(x_ref[0], f, mod_ref[0, 0, 5:6, :], g_ref[...], b_ref[...], alpha)


def _combine(x, modtab, z, wt, ln_g, ln_b, alpha, ctx_len, first_block):
    bsz, t, d = x.shape
    tm = ROW_TILE
    ctx_blocks = ctx_len // tm
    nb = t // tm - first_block
    row_in = lambda i, j: (i, j + first_block, 0)
    row = lambda i, j: (i, j, 0)
    const2 = lambda i, j: (0, 0)
    return pl.pallas_call(
        functools.partial(_combine_kernel, alpha=alpha),
        grid=(bsz, nb),
        in_specs=[pl.BlockSpec((1, tm, d), row_in),
                  pl.BlockSpec((1, 1, 8, d), lambda i, j: (i, _seg_of_block(j + first_block, ctx_blocks), 0, 0)),
                  pl.BlockSpec((TOP_K, 1, tm, d), lambda i, j: (0, i, j, 0)),
                  pl.BlockSpec((1, tm, LANES), row),
                  pl.BlockSpec((1, d), const2), pl.BlockSpec((1, d), const2)],
        out_specs=pl.BlockSpec((1, tm, d), row),
        out_shape=jax.ShapeDtypeStruct((bsz, nb * tm, d), F32),
        compiler_params=_cparams("parallel", "parallel"),
    )(x, modtab, z, wt, ln_g, ln_b)


def _moe_layer(x, modtab, w_r, b_r, w1, b1, w2, b2, ln_g, ln_b, alpha, ctx_len, latent_only):
    bsz, t, d = x.shape
    first_block = ctx_len // ROW_TILE if latent_only else 0
    wr_pad = jnp.pad(w_r, ((0, 0), (0, LANES - N_EXPERTS)))
    br_pad = jnp.pad(b_r, (0, LANES - N_EXPERTS)).reshape(1, LANES)
    h, idx, wt = _router(x, modtab, wr_pad, br_pad, ctx_len, first_block)
    tr = h.shape[1]
    n_tok = bsz * tr
    n_assign = n_tok * TOP_K
    flat_e = idx[:, :, :TOP_K].reshape(n_assign)
    onehot = (flat_e[:, None] == jnp.arange(N_EXPERTS, dtype=jnp.int32)[None, :]).astype(jnp.int32)
    csum = jnp.cumsum(onehot, axis=0)
    counts = csum[-1]
    pos = jnp.sum(jnp.where(onehot > 0, csum, 0), axis=1) - 1
    tm = EXPERT_TILE
    padded = (counts + tm - 1) // tm * tm
    pends = jnp.cumsum(padded)
    pstarts = pends - padded
    dest = pstarts[flat_e] + pos
    n_rows = (-(-n_assign // tm) + N_EXPERTS) * tm
    n_blocks = n_rows // tm
    row_tok = jnp.zeros((n_rows,), jnp.int32).at[dest].set(
        jnp.arange(n_assign, dtype=jnp.int32) // TOP_K, unique_indices=True)
    block_e = jnp.minimum(jnp.searchsorted(pends, jnp.arange(n_blocks, dtype=jnp.int32) * tm, side="right"),
                          N_EXPERTS - 1).astype(jnp.int32)
    n_used = (pends[-1] // tm).astype(jnp.int32).reshape(1)
    xs = _gather_rows(h.reshape(n_tok, d), row_tok)
    ys = _experts(xs, block_e, n_used, w1, b1, w2, b2)
    dest_t = dest.reshape(n_tok, TOP_K).T.reshape(n_assign)
    z = _gather_rows(ys, dest_t).reshape(TOP_K, bsz, tr, d)
    return _combine(x, modtab, z, wt, ln_g, ln_b, alpha, ctx_len, first_block)


def _gather_rows(table, rows):
    return jnp.take(table, rows, axis=0)


def _rot_cols(w, head_dim):
    lead = w.shape[:-1]
    q = head_dim // 4
    wr = w.reshape(lead + (-1, 4, q))
    out = jnp.stack([-wr[..., 1, :], wr[..., 0, :], -wr[..., 3, :], wr[..., 2, :]], axis=-2)
    return out.reshape(w.shape)


def _dup_heads(w, head_dim):
    lead = w.shape[:-1]
    wr = w.reshape(lead + (-1, 1, head_dim))
    return jnp.concatenate([wr, wr], axis=-2).reshape(lead + (-1,))


def _rope_tables(n_lat, ctx_len, head_dim):
    rows = n_lat // GRID_W
    row = jnp.repeat(jnp.arange(rows, dtype=F32), GRID_W)
    col = jnp.tile(jnp.arange(GRID_W, dtype=F32), rows)
    half = head_dim // 2
    inv = ROPE_BASE ** (-jnp.arange(0, half, 2, dtype=F32) / half)
    ar = row[:, None] * inv
    ac = col[:, None] * inv
    ang = jnp.concatenate([ar, ar, ac, ac], axis=-1)
    cos = jnp.concatenate([jnp.ones((ctx_len, head_dim), F32), jnp.cos(ang)], axis=0)
    sin = jnp.concatenate([jnp.zeros((ctx_len, head_dim), F32), jnp.sin(ang)], axis=0)
    reps = LANES // head_dim
    return jnp.tile(cos, (1, reps)), jnp.tile(sin, (1, reps))


def _flip_segments(a, ctx_len, axis):
    ctx = lax.slice_in_dim(a, 0, ctx_len, axis=axis)
    lat = lax.slice_in_dim(a, ctx_len, a.shape[axis], axis=axis)
    return jnp.concatenate([jnp.flip(ctx, axis), jnp.flip(lat, axis)], axis=axis)


def _conv_mixer(x, modtab, w_in, w_conv, w_out, ln_g, ln_b, alpha, ctx_len):
    d = x.shape[2]
    plan = [("plain", 0, 0, d, 0, None, 1.0), ("mul", 1, 0, d, d, 2 * d, 1.0)]
    bg, p = _project(x, modtab, w_in.astype(BF16), jnp.zeros((1, 3 * d), F32), plan, (d, d), (BF16, BF16), ctx_len)
    cw = jnp.pad(w_conv, ((0, 8 - SC_WIDTH), (0, 0)))
    return _mixer_out("conv", x, modtab, (bg, p), w_out.astype(BF16), jnp.zeros((1, d), F32), ln_g, ln_b, alpha,
                      ctx_len, extra=cw)


def _swa_mixer(x, modtab, w_qkv, b_qkv, sink, w_o, b_o, ln_g, ln_b, alpha, ctx_len, rope):
    d = x.shape[2]
    nq = SWA_HEADS * SWA_HEAD_DIM
    nkv = SWA_KV_HEADS * SWA_HEAD_DIM
    hd = SWA_HEAD_DIM

    def arrange(a):
        q, k, v = a[..., :nq], a[..., nq:nq + nkv], a[..., nq + nkv:]
        kk = _dup_heads(k, hd)
        return jnp.concatenate([q, _rot_cols(q, hd), kk, _rot_cols(kk, hd), _dup_heads(v, hd)], axis=-1)

    w = arrange(w_qkv).astype(BF16)
    b = arrange(b_qkv.reshape(1, -1))
    plan = [("rope", 0, 0, nq, 0, nq, SWA_HEAD_DIM ** -0.5),
            ("rope", 1, 0, 2 * nkv, 2 * nq, 2 * nq + 2 * nkv, 1.0),
            ("plain", 2, 0, 2 * nkv, 2 * nq + 4 * nkv, None, 1.0)]
    q, kk, vv = _project(x, modtab, w, b, plan, (nq, 2 * nkv, 2 * nkv), (BF16, BF16, BF16), ctx_len, rope=rope)
    o = _swa_attention(q, kk, vv, sink.astype(F32), ctx_len)
    return _mixer_out("plain", x, modtab, (o,), w_o.astype(BF16), b_o.reshape(1, d), ln_g, ln_b, alpha, ctx_len)


def _diff_mixer(x, modtab, w_qkv, lam_p, subln_g, w_o, lam_init, ln_g, ln_b, alpha, ctx_len, rope):
    d = x.shape[2]
    hd = DIFF_HEAD_DIM
    wq, wk, wv = w_qkv[:, :d], w_qkv[:, d:2 * d], w_qkv[:, 2 * d:]
    w = jnp.concatenate([wq, _rot_cols(wq, hd), wk, _rot_cols(wk, hd), wv], axis=-1).astype(BF16)
    plan = [("rope", 0, 0, d, 0, d, DIFF_HEAD_DIM ** -0.5),
            ("rope", 1, 0, d, 2 * d, 3 * d, 1.0),
            ("plain", 2, 0, d, 4 * d, None, 1.0)]
    q, k, v = _project(x, modtab, w, jnp.zeros((1, 5 * d), F32), plan, (d, d, d), (BF16, BF16, BF16), ctx_len,
                       rope=rope)
    o = _diff_attention(q, k, v, lam_p.astype(F32), subln_g.astype(F32), ctx_len, lam_init)
    return _mixer_out("plain", x, modtab, (o,), w_o.astype(BF16), jnp.zeros((1, d), F32), ln_g, ln_b, alpha, ctx_len)


def _delta_mixer(x, modtab, w_qkvz, w_ba, a_log, dt_bias, w_conv, norm_g, w_o, ln_g, ln_b, alpha, ctx_len):
    bsz, t, d = x.shape
    nqk = DN_QK_HEADS * DN_HEAD_DIM
    nv = DN_V_HEADS * DN_HEAD_DIM
    nc = 2 * nqk + nv
    nba = w_ba.shape[1]
    w = jnp.concatenate([w_qkvz, jnp.pad(w_ba, ((0, 0), (0, LANES - nba)))], axis=-1).astype(BF16)
    ntot = w.shape[1]
    plan = [("plain", 0, 0, nc, 0, None, 1.0), ("plain", 1, 0, nv, nc, None, 1.0),
            ("plain", 2, 0, LANES, nc + nv, None, 1.0)]
    qkv_pre, z, ba = _project(x, modtab, w, jnp.zeros((1, ntot), F32), plan, (nc, nv, LANES), (BF16, BF16, F32),
                              ctx_len)
    zeros16 = jnp.zeros((DN_V_HEADS,), F32)
    lanes_of = lambda p: jnp.pad(jnp.concatenate([zeros16, p[0], zeros16, p[1]]), (0, LANES - 4 * DN_V_HEADS))
    alog_l = lanes_of(a_log.astype(F32)).reshape(1, LANES)
    dtb_l = lanes_of(dt_bias.astype(F32)).reshape(1, LANES)
    cw = jnp.pad(w_conv, ((0, 8 - DN_CONV), (0, 0)))
    q, k, v, gb = _delta_prep(qkv_pre, cw, ba, alog_l, dtb_l, ctx_len, nqk, nv)
    flip = lambda a: _flip_segments(a, ctx_len, 1)
    q2 = jnp.stack([q, flip(q)])
    k2 = jnp.stack([k, flip(k)])
    v2 = jnp.stack([v, flip(v)])
    gb2 = jnp.stack([gb, flip(gb)])
    rep = DN_V_HEADS // DN_QK_HEADS
    gcum = jnp.stack([gb2[0, :, :, DN_V_HEADS:2 * DN_V_HEADS], gb2[1, :, :, 3 * DN_V_HEADS:4 * DN_V_HEADS]])
    gt2 = jnp.transpose(gcum.reshape(2, bsz, t, DN_QK_HEADS, rep), (0, 1, 3, 4, 2))
    gt2 = jnp.pad(gt2, ((0, 0), (0, 0), (0, 0), (0, 8 - rep), (0, 0)))
    o2 = _delta_scan(q2, k2, v2, gb2, gt2)
    return _mixer_out("delta", x, modtab, (o2[0], flip(o2[1]), z), w_o.astype(BF16), jnp.zeros((1, d), F32),
                      ln_g, ln_b, alpha, ctx_len, extra=norm_g.astype(F32).reshape(1, DN_HEAD_DIM))


def kernel(x, c, ctx, c_ctx, mod_w, mod_b, ln1_g, ln1_b, ln2_g, ln2_b, router_w, router_b, exp_w1, exp_b1, exp_w2, exp_b2, conv_in_w, conv_w, conv_out_w, swa_qkv_w, swa_qkv_b, swa_sink, swa_out_w, swa_out_b, diff_qkv_w, diff_lambda, diff_subln_g, diff_out_w, delta_qkvz_w, delta_ba_w, delta_a_log, delta_dt_bias, delta_conv_w, delta_norm_g, delta_out_w):
    bsz, n_lat, d = x.shape
    ctx_len = ctx.shape[1]
    depth = mod_w.shape[0]
    alpha = (2 * depth) ** 0.25
    xs = jnp.concatenate([ctx, x], axis=1)
    cc = jnp.zeros((16, d), F32).at[:bsz].set(c).at[bsz].set(c_ctx)
    mods = _mod_vectors(cc, mod_w, mod_b).reshape(depth, 16, 6, d)
    mod_lat = mods[:, :bsz]
    mod_ctx = jnp.broadcast_to(mods[:, bsz:bsz + 1], mod_lat.shape)
    modtabs = jnp.pad(jnp.stack([mod_ctx, mod_lat], axis=2), ((0, 0), (0, 0), (0, 0), (0, 2), (0, 0)))
    rope = _rope_tables(n_lat, ctx_len, SWA_HEAD_DIM)
    row1 = lambda a: a.reshape(1, d)
    for i in range(depth):
        last = i == depth - 1
        kind, j = i % N_MIXERS, i // N_MIXERS
        mt = modtabs[i]
        g1, b1 = row1(ln1_g[i]), row1(ln1_b[i])
        if kind == 0:
            xs = _conv_mixer(xs, mt, conv_in_w[j], conv_w[j], conv_out_w[j], g1, b1, alpha, ctx_len)
        elif kind == 1:
            xs = _swa_mixer(xs, mt, swa_qkv_w[j], swa_qkv_b[j], swa_sink[j], swa_out_w[j], swa_out_b[j], g1, b1,
                            alpha, ctx_len, rope)
        elif kind == 2:
            lam_init = 0.8 - 0.6 * math.exp(-0.3 * i)
            xs = _diff_mixer(xs, mt, diff_qkv_w[j], diff_lambda[j], diff_subln_g[j], diff_out_w[j], lam_init,
                             g1, b1, alpha, ctx_len, rope)
        else:
            xs = _delta_mixer(xs, mt, delta_qkvz_w[j], delta_ba_w[j], delta_a_log[j], delta_dt_bias[j],
                              delta_conv_w[j], delta_norm_g[j], delta_out_w[j], g1, b1, alpha, ctx_len)
        xs = _moe_layer(xs, mt, router_w[i], router_b[i], exp_w1[i].astype(BF16), exp_b1[i],
                        exp_w2[i].astype(BF16), exp_b2[i], row1(ln2_g[i]), row1(ln2_b[i]), alpha, ctx_len,
                        latent_only=last)
    return xs
```

```python
import functools
import math

import jax
import jax.numpy as jnp
from jax import lax
from jax.experimental import pallas as pl
from jax.experimental.pallas import tpu as pltpu
from jax.experimental.pallas import tpu_sc as plsc

F32 = jnp.float32
BF16 = jnp.bfloat16

GRID_W = 64
N_MIXERS = 4
LN_EPS = 1e-5
RMS_EPS = 1e-6
NEG_INF = -1e30
ROPE_BASE = 10000.0
SC_WIDTH = 3
SWA_HEADS = 16
SWA_KV_HEADS = 4
SWA_HEAD_DIM = 64
SWA_WINDOW = 128
DIFF_HEADS = 8
DIFF_HEAD_DIM = 64
DN_QK_HEADS = 8
DN_V_HEADS = 16
DN_HEAD_DIM = 128
DN_CONV = 5
N_EXPERTS = 32
TOP_K = 4
SWIGLU_LIMIT = 7.0
SWIGLU_ALPHA = 1.702

LANES = 128
HALO_ROWS = 16
VMEM_LIMIT = 56 * 1024 * 1024

ROW_TILE = 256
ATT_BLOCK = 128
DN_CHUNK = 128
EXPERT_TILE = 256
COL_CHUNK = 512


def _cparams(*sem):
    return pltpu.CompilerParams(dimension_semantics=sem, vmem_limit_bytes=VMEM_LIMIT)


def _split3(x):
    hi = x.astype(BF16)
    r1 = x - hi.astype(F32)
    mid = r1.astype(BF16)
    lo = (r1 - mid.astype(F32)).astype(BF16)
    return hi, mid, lo


def _dot(a, b):
    return jnp.dot(a, b, preferred_element_type=F32)


def _dot_nt(a, b):
    return lax.dot_general(a, b, (((1,), (1,)), ((), ())), preferred_element_type=F32)


def _dot_tn(a, b):
    return lax.dot_general(a, b, (((0,), (0,)), ((), ())), preferred_element_type=F32)


def _silu(x):
    return x * jax.nn.sigmoid(x)


def _seg_of_block(j, ctx_blocks):
    return jnp.where(j < ctx_blocks, 0, 1)


def _mods_kernel(c_ref, w_ref, b_ref, o_ref):
    s = _silu(c_ref[...]).astype(BF16)
    o_ref[0] = _dot(s, w_ref[0].astype(BF16)) + b_ref[0]


def _mod_vectors(cc, mod_w, mod_b):
    depth, d, n = mod_w.shape
    rows = cc.shape[0]
    tn = 1536
    return pl.pallas_call(
        _mods_kernel,
        grid=(depth, n // tn),
        in_specs=[pl.BlockSpec((rows, d), lambda l, j: (0, 0)),
                  pl.BlockSpec((1, d, tn), lambda l, j: (l, 0, j)),
                  pl.BlockSpec((1, 1, tn), lambda l, j: (l, 0, j))],
        out_specs=pl.BlockSpec((1, rows, tn), lambda l, j: (l, 0, j)),
        out_shape=jax.ShapeDtypeStruct((depth, rows, n), F32),
        compiler_params=_cparams("parallel", "parallel"),
    )(cc, mod_w, mod_b.reshape(depth, 1, n))


def _proj_kernel(*refs, plan, n_out, use_rope):
    x_ref, mod_ref, w_ref, b_ref = refs[:4]
    pos = 4
    if use_rope:
        cos_ref, sin_ref = refs[4:6]
        pos = 6
    outs = refs[pos:pos + n_out]
    shift = mod_ref[0, 0, 0:1, :]
    scale = mod_ref[0, 0, 1:2, :]
    h = (x_ref[0] * (1.0 + scale) + shift).astype(BF16)

    def acc(col, width):
        return _dot(h, w_ref[:, col:col + width]) + b_ref[:, col:col + width]

    for kind, oi, ocol, width, wcol, wcol2, mult in plan:
        for c0 in range(0, width, COL_CHUNK):
            cw = min(COL_CHUNK, width - c0)
            a = acc(wcol + c0, cw)
            if kind == "rope":
                reps = cw // LANES
                cos = jnp.tile(cos_ref[...], (1, reps))
                sin = jnp.tile(sin_ref[...], (1, reps))
                a = (a * cos + acc(wcol2 + c0, cw) * sin) * mult
            elif kind == "mul":
                a = a * acc(wcol2 + c0, cw)
            outs[oi][0, :, ocol + c0:ocol + c0 + cw] = a.astype(outs[oi].dtype)


def _project(x, modtab, w, b, plan, out_widths, out_dtypes, ctx_len, rope=None):
    bsz, t, d = x.shape
    n = w.shape[1]
    tm = ROW_TILE
    ctx_blocks = ctx_len // tm
    mt = modtab
    in_specs = [pl.BlockSpec((1, tm, d), lambda i, j: (i, j, 0)),
                pl.BlockSpec((1, 1, 8, d), lambda i, j: (i, _seg_of_block(j, ctx_blocks), 0, 0)),
                pl.BlockSpec((d, n), lambda i, j: (0, 0)),
                pl.BlockSpec((1, n), lambda i, j: (0, 0))]
    args = [x, mt, w, b]
    if rope is not None:
        in_specs += [pl.BlockSpec((tm, LANES), lambda i, j: (j, 0))] * 2
        args += list(rope)
    out_specs = [pl.BlockSpec((1, tm, ow), lambda i, j: (i, j, 0)) for ow in out_widths]
    out_shape = [jax.ShapeDtypeStruct((bsz, t, ow), dt) for ow, dt in zip(out_widths, out_dtypes)]
    return pl.pallas_call(
        functools.partial(_proj_kernel, plan=tuple(plan), n_out=len(out_widths), use_rope=rope is not None),
        grid=(bsz, t // tm),
        in_specs=in_specs, out_specs=out_specs, out_shape=out_shape,
        compiler_params=_cparams("parallel", "parallel"),
    )(*args)


def _resid_ln(x, y, gate, g, b, alpha):
    r = alpha * x + gate * y
    mu = jnp.mean(r, axis=-1, keepdims=True)
    rc = r - mu
    var = jnp.mean(rc * rc, axis=-1, keepdims=True)
    return rc * lax.rsqrt(var + LN_EPS) * g + b


def _shift_rows(p, k, edge_rows, valid):
    tm = p.shape[0]
    rolled = pltpu.roll(p, k % tm, axis=0)
    row = lax.broadcasted_iota(jnp.int32, p.shape, 0)
    out = rolled
    if k > 0:
        for r in range(k):
            fill = jnp.where(valid, edge_rows[edge_rows.shape[0] - k + r:edge_rows.shape[0] - k + r + 1, :], 0.0)
            out = jnp.where(row == r, fill, out)
    else:
        for r in range(-k):
            fill = jnp.where(valid, edge_rows[r:r + 1, :], 0.0)
            out = jnp.where(row == tm + k + r, fill, out)
    return out


def _centred_conv(p, prev, nxt, w_ref, width, has_prev, has_next):
    pad = (width - 1) // 2
    acc = p * w_ref[pad:pad + 1, :]
    for k in range(1, pad + 1):
        acc = acc + _shift_rows(p, k, prev, has_prev) * w_ref[pad - k:pad - k + 1, :]
        acc = acc + _shift_rows(p, -k, nxt, has_next) * w_ref[pad + k:pad + k + 1, :]
    return acc


def _seg_edges(j, ctx_blocks, n_blocks):
    has_prev = jnp.logical_and(j != 0, j != ctx_blocks)
    has_next = jnp.logical_and(j != ctx_blocks - 1, j != n_blocks - 1)
    return has_prev, has_next


def _out_plain_kernel(x_ref, mod_ref, o_ref, w_ref, b_ref, g_ref, bb_ref, out_ref, *, alpha):
    y = _dot(o_ref[0], w_ref[...]) + b_ref[...]
    out_ref[0] = _resid_ln(x_ref[0], y, mod_ref[0, 0, 2:3, :], g_ref[...], bb_ref[...], alpha)


def _out_conv_kernel(x_ref, mod_ref, bg_ref, p_ref, pp_ref, pn_ref, cw_ref, w_ref, b_ref, g_ref, bb_ref,
                     out_ref, *, alpha, ctx_blocks, n_blocks):
    j = pl.program_id(1)
    has_prev, has_next = _seg_edges(j, ctx_blocks, n_blocks)
    conv = _centred_conv(p_ref[0].astype(F32), pp_ref[0].astype(F32), pn_ref[0].astype(F32), cw_ref,
                         SC_WIDTH, has_prev, has_next)
    o = (bg_ref[0].astype(F32) * conv).astype(BF16)
    y = _dot(o, w_ref[...]) + b_ref[...]
    out_ref[0] = _resid_ln(x_ref[0], y, mod_ref[0, 0, 2:3, :], g_ref[...], bb_ref[...], alpha)


def _out_delta_kernel(x_ref, mod_ref, of_ref, ob_ref, z_ref, ng_ref, w_ref, b_ref, g_ref, bb_ref,
                      out_ref, *, alpha):
    hd = DN_HEAD_DIM
    ng = ng_ref[...]
    acc = None
    for h0 in range(0, of_ref.shape[2], hd):
        o = of_ref[0, :, h0:h0 + hd].astype(F32) + ob_ref[0, :, h0:h0 + hd].astype(F32)
        o = o * lax.rsqrt(jnp.mean(o * o, axis=-1, keepdims=True) + RMS_EPS) * ng
        o = (o * _silu(z_ref[0, :, h0:h0 + hd].astype(F32))).astype(BF16)
        part = _dot(o, w_ref[h0:h0 + hd, :])
        acc = part if acc is None else acc + part
    y = acc + b_ref[...]
    out_ref[0] = _resid_ln(x_ref[0], y, mod_ref[0, 0, 2:3, :], g_ref[...], bb_ref[...], alpha)


def _mixer_out(kind, x, modtab, acts, w_o, b_o, ln_g, ln_b, alpha, ctx_len, extra=None):
    bsz, t, d = x.shape
    tm = ROW_TILE
    nb = t // tm
    ctx_blocks = ctx_len // tm
    kin = w_o.shape[0]
    row = lambda i, j: (i, j, 0)
    const2 = lambda i, j: (0, 0)
    x_spec = pl.BlockSpec((1, tm, d), row)
    mod_spec = pl.BlockSpec((1, 1, 8, d), lambda i, j: (i, _seg_of_block(j, ctx_blocks), 0, 0))
    tail_specs = [pl.BlockSpec((kin, d), const2), pl.BlockSpec((1, d), const2),
                  pl.BlockSpec((1, d), const2), pl.BlockSpec((1, d), const2)]
    tail_args = [w_o, b_o, ln_g, ln_b]
    if kind == "plain":
        body = functools.partial(_out_plain_kernel, alpha=alpha)
        in_specs = [x_spec, mod_spec, pl.BlockSpec((1, tm, kin), row)] + tail_specs
        args = [x, modtab, acts[0]] + tail_args
    elif kind == "conv":
        body = functools.partial(_out_conv_kernel, alpha=alpha, ctx_blocks=ctx_blocks, n_blocks=nb)
        hb = tm // HALO_ROWS
        last_halo = t // HALO_ROWS - 1
        in_specs = [x_spec, mod_spec, pl.BlockSpec((1, tm, d), row), pl.BlockSpec((1, tm, d), row),
                    pl.BlockSpec((1, HALO_ROWS, d), lambda i, j: (i, jnp.maximum(j * hb - 1, 0), 0)),
                    pl.BlockSpec((1, HALO_ROWS, d), lambda i, j: (i, jnp.minimum((j + 1) * hb, last_halo), 0)),
                    pl.BlockSpec((8, d), const2)] + tail_specs
        args = [x, modtab, acts[0], acts[1], acts[1], acts[1], extra] + tail_args
    else:
        body = functools.partial(_out_delta_kernel, alpha=alpha)
        in_specs = [x_spec, mod_spec, pl.BlockSpec((1, tm, kin), row), pl.BlockSpec((1, tm, kin), row),
                    pl.BlockSpec((1, tm, kin), row), pl.BlockSpec((1, DN_HEAD_DIM), const2)] + tail_specs
        args = [x, modtab, acts[0], acts[1], acts[2], extra] + tail_args
    return pl.pallas_call(
        body, grid=(bsz, nb), in_specs=in_specs,
        out_specs=pl.BlockSpec((1, tm, d), row),
        out_shape=jax.ShapeDtypeStruct((bsz, t, d), F32),
        compiler_params=_cparams("parallel", "parallel"),
    )(*args)


def _swa_kernel(sink_ref, q_ref, kp_ref, ko_ref, kn_ref, kc_ref, vp_ref, vo_ref, vn_ref, vc_ref, o_ref,
                *, n_lat_blocks, ctx_blocks):
    j = pl.program_id(1)
    blk = ATT_BLOCK
    is_ctx = j < ctx_blocks
    lat = j - ctx_blocks
    r = lax.broadcasted_iota(jnp.int32, (blk, blk), 0)
    c = lax.broadcasted_iota(jnp.int32, (blk, blk), 1)
    lat_ok = jnp.logical_not(is_ctx)
    m_prev = jnp.logical_and(jnp.logical_and(c >= r, lat > 0), lat_ok)
    m_own = jnp.logical_and(r == r, lat_ok)
    m_next = jnp.logical_and(jnp.logical_and(c <= r, lat < n_lat_blocks - 1), lat_ok)
    lane = lax.broadcasted_iota(jnp.int32, (blk, LANES), 1)
    lo = lane < SWA_HEAD_DIM
    group = SWA_HEADS // SWA_KV_HEADS
    for pair in range(SWA_HEADS // 2):
        g = (2 * pair) // group
        gs = slice(g * LANES, (g + 1) * LANES)
        qp = q_ref[0, :, pair * LANES:(pair + 1) * LANES]
        zero = jnp.zeros_like(qp)
        halves = []
        for half in range(2):
            qh = jnp.where(lo, qp, zero) if half == 0 else jnp.where(lo, zero, qp)
            s_p = jnp.where(m_prev, _dot_nt(qh, kp_ref[0, :, gs]), NEG_INF)
            s_o = jnp.where(m_own, _dot_nt(qh, ko_ref[0, :, gs]), NEG_INF)
            s_n = jnp.where(m_next, _dot_nt(qh, kn_ref[0, :, gs]), NEG_INF)
            s_c = _dot_nt(qh, kc_ref[0, :, gs])
            sink = sink_ref[2 * pair + half]
            m = jnp.maximum(jnp.maximum(jnp.max(s_p, axis=-1, keepdims=True), jnp.max(s_o, axis=-1, keepdims=True)),
                            jnp.maximum(jnp.max(s_n, axis=-1, keepdims=True), jnp.max(s_c, axis=-1, keepdims=True)))
            m = jnp.maximum(m, sink)
            e_p, e_o, e_n, e_c = jnp.exp(s_p - m), jnp.exp(s_o - m), jnp.exp(s_n - m), jnp.exp(s_c - m)
            den = (jnp.sum(e_p, axis=-1, keepdims=True) + jnp.sum(e_o, axis=-1, keepdims=True)
                   + jnp.sum(e_n, axis=-1, keepdims=True) + jnp.sum(e_c, axis=-1, keepdims=True)
                   + jnp.exp(sink - m))
            inv = 1.0 / den
            acc = _dot((e_p * inv).astype(BF16), vp_ref[0, :, gs])
            acc = acc + _dot((e_o * inv).astype(BF16), vo_ref[0, :, gs])
            acc = acc + _dot((e_n * inv).astype(BF16), vn_ref[0, :, gs])
            acc = acc + _dot((e_c * inv).astype(BF16), vc_ref[0, :, gs])
            halves.append(acc)
        o_ref[0, :, pair * LANES:(pair + 1) * LANES] = jnp.where(lo, halves[0], halves[1]).astype(o_ref.dtype)


def _swa_attention(q, kk, vv, sink, ctx_len):
    bsz, t, dq = q.shape
    dk = kk.shape[2]
    blk = ATT_BLOCK
    nb = t // blk
    ctx_blocks = ctx_len // blk
    n_lat_blocks = nb - ctx_blocks

    def prev_map(i, j, s):
        return (i, jnp.clip(j - 1, ctx_blocks, nb - 1), 0)

    def own_map(i, j, s):
        return (i, j, 0)

    def next_map(i, j, s):
        return (i, jnp.clip(j + 1, ctx_blocks, nb - 1), 0)

    def ctx_map(i, j, s):
        return (i, 0, 0)

    kv_specs = [pl.BlockSpec((1, blk, dk), prev_map), pl.BlockSpec((1, blk, dk), own_map),
                pl.BlockSpec((1, blk, dk), next_map), pl.BlockSpec((1, ctx_len, dk), ctx_map)]
    return pl.pallas_call(
        functools.partial(_swa_kernel, n_lat_blocks=n_lat_blocks, ctx_blocks=ctx_blocks),
        grid_spec=pltpu.PrefetchScalarGridSpec(
            num_scalar_prefetch=1, grid=(bsz, nb),
            in_specs=[pl.BlockSpec((1, blk, dq), own_map)] + kv_specs + kv_specs,
            out_specs=pl.BlockSpec((1, blk, dq), own_map)),
        out_shape=jax.ShapeDtypeStruct((bsz, t, dq), BF16),
        compiler_params=_cparams("parallel", "parallel"),
    )(sink, q, kk, kk, kk, kk, vv, vv, vv, vv)


def _diff_kernel(lam_ref, q_ref, k_ref, v_ref, g_ref, o_ref, *, ctx_len, lam_init):
    j = pl.program_id(2)
    blk = ATT_BLOCK
    ctx_blocks = ctx_len // blk
    lp = lam_ref[...]
    lam = (jnp.exp(jnp.sum(lp[0:1, :] * lp[1:2, :], axis=-1, keepdims=True))
           - jnp.exp(jnp.sum(lp[2:3, :] * lp[3:4, :], axis=-1, keepdims=True)) + lam_init)
    lane = lax.broadcasted_iota(jnp.int32, (blk, LANES), 1)
    lo = lane < DIFF_HEAD_DIM
    qp = q_ref[0]
    zero = jnp.zeros_like(qp)

    def attend(n_keys):
        k = k_ref[0, 0:n_keys, :]
        v = v_ref[0, 0:n_keys, :]
        ps = []
        for half in range(2):
            qh = jnp.where(lo, qp, zero) if half == 0 else jnp.where(lo, zero, qp)
            s = _dot_nt(qh, k)
            m = jnp.max(s, axis=-1, keepdims=True)
            e = jnp.exp(s - m)
            ps.append(e * (1.0 / jnp.sum(e, axis=-1, keepdims=True)))
        a = (ps[0] - lam * ps[1]).astype(BF16)
        o = _dot(a, v)
        o = o * lax.rsqrt(jnp.mean(o * o, axis=-1, keepdims=True) + RMS_EPS) * g_ref[...]
        o_ref[0] = (o * (1.0 - lam_init)).astype(o_ref.dtype)

    @pl.when(j < ctx_blocks)
    def _():
        attend(ctx_len)

    @pl.when(j >= ctx_blocks)
    def _():
        attend(k_ref.shape[1])


def _diff_attention(q, k, v, lam_p, subln_g, ctx_len, lam_init):
    bsz, t, dq = q.shape
    blk = ATT_BLOCK
    nh = dq // LANES
    qmap = lambda i, h, j: (i, j, h)
    kmap = lambda i, h, j: (i, 0, h)
    const2 = lambda i, h, j: (0, 0)
    return pl.pallas_call(
        functools.partial(_diff_kernel, ctx_len=ctx_len, lam_init=lam_init),
        grid=(bsz, nh, t // blk),
        in_specs=[pl.BlockSpec((8, DIFF_HEAD_DIM), const2),
                  pl.BlockSpec((1, blk, LANES), qmap),
                  pl.BlockSpec((1, t, LANES), kmap),
                  pl.BlockSpec((1, t, LANES), kmap),
                  pl.BlockSpec((1, LANES), const2)],
        out_specs=pl.BlockSpec((1, blk, LANES), qmap),
        out_shape=jax.ShapeDtypeStruct((bsz, t, dq), BF16),
        compiler_params=_cparams("parallel", "parallel", "parallel"),
    )(jnp.pad(lam_p, ((0, 4), (0, 0))), q, k, v, subln_g.reshape(1, LANES))


def _delta_prep_kernel(x_ref, xp_ref, xn_ref, cw_ref, ba_ref, alog_ref, dtb_ref,
                       q_ref, k_ref, v_ref, gb_ref, *, ctx_blocks, n_blocks, nqk):
    j = pl.program_id(1)
    has_prev, has_next = _seg_edges(j, ctx_blocks, n_blocks)
    hd = DN_HEAD_DIM
    tm = x_ref.shape[1]
    for c0 in range(0, x_ref.shape[2], COL_CHUNK):
        cs = slice(c0, c0 + COL_CHUNK)
        conv = _centred_conv(x_ref[0, :, cs].astype(F32), xp_ref[0, :, cs].astype(F32),
                             xn_ref[0, :, cs].astype(F32), cw_ref.at[:, cs], DN_CONV, has_prev, has_next)
        a = _silu(conv)
        for h0 in range(0, COL_CHUNK, hd):
            col = c0 + h0
            ah = a[:, h0:h0 + hd]
            if col < 2 * nqk:
                ah = ah * lax.rsqrt(jnp.sum(ah * ah, axis=-1, keepdims=True) + RMS_EPS)
                if col < nqk:
                    q_ref[0, :, col:col + hd] = (ah * (hd ** -0.5)).astype(q_ref.dtype)
                else:
                    k_ref[0, :, col - nqk:col - nqk + hd] = ah.astype(k_ref.dtype)
            else:
                v_ref[0, :, col - 2 * nqk:col - 2 * nqk + hd] = ah.astype(v_ref.dtype)
    ba = ba_ref[0]
    lane = lax.broadcasted_iota(jnp.int32, ba.shape, 1)
    is_beta = (lane % 32) < DN_V_HEADS
    z = ba + dtb_ref[...]
    softplus = jnp.maximum(z, 0.0) + jnp.log(1.0 + jnp.exp(-jnp.abs(z)))
    g = -jnp.exp(alog_ref[...]) * softplus
    r = lax.broadcasted_iota(jnp.int32, (tm, tm), 0)
    c = lax.broadcasted_iota(jnp.int32, (tm, tm), 1)
    same = (r // DN_CHUNK) == (c // DN_CHUNK)
    tri_f = jnp.where(jnp.logical_and(same, c <= r), 1.0, 0.0).astype(BF16)
    tri_b = jnp.where(jnp.logical_and(same, c >= r), 1.0, 0.0).astype(BF16)
    parts = _split3(g)
    cum_f = _dot(tri_f, parts[0]) + _dot(tri_f, parts[1]) + _dot(tri_f, parts[2])
    cum_b = _dot(tri_b, parts[0]) + _dot(tri_b, parts[1]) + _dot(tri_b, parts[2])
    gcum = jnp.where(lane < 32, cum_f, cum_b)
    gb_ref[0] = jnp.where(is_beta, jax.nn.sigmoid(ba), gcum)


def _delta_prep(qkv_pre, conv_w, ba, alog_l, dtb_l, ctx_len, nqk, nv):
    bsz, t, nc = qkv_pre.shape
    tm = ROW_TILE
    nb = t // tm
    ctx_blocks = ctx_len // tm
    hb = tm // HALO_ROWS
    last_halo = t // HALO_ROWS - 1
    row = lambda i, j: (i, j, 0)
    const2 = lambda i, j: (0, 0)
    return pl.pallas_call(
        functools.partial(_delta_prep_kernel, ctx_blocks=ctx_blocks, n_blocks=nb, nqk=nqk),
        grid=(bsz, nb),
        in_specs=[pl.BlockSpec((1, tm, nc), row),
                  pl.BlockSpec((1, HALO_ROWS, nc), lambda i, j: (i, jnp.maximum(j * hb - 1, 0), 0)),
                  pl.BlockSpec((1, HALO_ROWS, nc), lambda i, j: (i, jnp.minimum((j + 1) * hb, last_halo), 0)),
                  pl.BlockSpec((8, nc), const2),
                  pl.BlockSpec((1, tm, LANES), row),
                  pl.BlockSpec((1, LANES), const2),
                  pl.BlockSpec((1, LANES), const2)],
        out_specs=[pl.BlockSpec((1, tm, nqk), row), pl.BlockSpec((1, tm, nqk), row),
                   pl.BlockSpec((1, tm, nv), row), pl.BlockSpec((1, tm, LANES), row)],
        out_shape=[jax.ShapeDtypeStruct((bsz, t, nqk), BF16), jax.ShapeDtypeStruct((bsz, t, nqk), BF16),
                   jax.ShapeDtypeStruct((bsz, t, nv), BF16), jax.ShapeDtypeStruct((bsz, t, LANES), F32)],
        compiler_params=_cparams("parallel", "parallel"),
    )(qkv_pre, qkv_pre, qkv_pre, conv_w, ba, alog_l, dtb_l)


DN_INV_BLOCK = 16


def _unit_lower_inverse(low, eye, r, c):
    n = low.shape[0]
    b = DN_INV_BLOCK
    diag = jnp.where((r // b) == (c // b), low, 0.0)
    x = eye - diag
    pw = diag
    for _ in range(int(math.log2(b)) - 1):
        pw_b = pw.astype(BF16)
        pw = _dot(pw_b, pw_b)
        x = x + _dot(x.astype(BF16), pw.astype(BF16))
    while b < n:
        off = jnp.where(jnp.logical_and((r // (2 * b)) == (c // (2 * b)), (r // b) != (c // b)), low, 0.0)
        xb = x.astype(BF16)
        x = x - _dot(xb, _dot(off.astype(BF16), xb).astype(BF16))
        b *= 2
    return x


DN_SCAN_HEADS = 8


def _delta_chunk_kernel(q_ref, k_ref, v_ref, gb_ref, gt_ref, u_ref, w_ref, qg_ref, kd_ref, a_ref, *, rep):
    hq = pl.program_id(1)
    ck = DN_CHUNK
    hd = DN_HEAD_DIM
    r = lax.broadcasted_iota(jnp.int32, (ck, ck), 0)
    c = lax.broadcasted_iota(jnp.int32, (ck, ck), 1)
    eye = jnp.where(r == c, 1.0, 0.0)
    lane = lax.broadcasted_iota(jnp.int32, (ck, LANES), 1)
    for sub in range(q_ref.shape[1] // ck):
        rows = slice(sub * ck, (sub + 1) * ck)
        q = q_ref[0, rows, :]
        k = k_ref[0, rows, :]
        kk = _dot_nt(k, k)
        qk = _dot_nt(q, k)
        gb = gb_ref[0, rows, :]
        qf = q.astype(F32)
        kf = k.astype(F32)
        for d in range(2):
            incl = (r >= c) if d == 0 else (r <= c)
            strict = (r > c) if d == 0 else (r < c)
            for jh in range(rep):
                head = hq * rep + jh
                cols = slice(jh * hd, (jh + 1) * hd)
                bcol = jnp.sum(jnp.where(lane == d * 32 + head, gb, 0.0), axis=-1, keepdims=True)
                gcol = jnp.sum(jnp.where(lane == d * 32 + DN_V_HEADS + head, gb, 0.0), axis=-1, keepdims=True)
                grow = gt_ref[d, 0, 0, jh:jh + 1, rows]
                glast = grow[:, ck - 1:ck] if d == 0 else grow[:, 0:1]
                decay = jnp.where(incl, jnp.exp(jnp.where(incl, gcol - grow, 0.0)), 0.0)
                low = jnp.where(strict, kk * decay, 0.0) * bcol
                tinv = _unit_lower_inverse(low, eye, r, c)
                v = v_ref[0, rows, cols].astype(F32)
                eg = jnp.exp(gcol)
                rhs = jnp.concatenate([v * bcol, kf * (bcol * eg)], axis=1).astype(BF16)
                uw = _dot(tinv.astype(BF16), rhs)
                u_ref[d, 0, rows, cols] = uw[:, :hd]
                w_ref[d, 0, rows, cols] = uw[:, hd:].astype(w_ref.dtype)
                qg_ref[d, 0, rows, cols] = (qf * eg).astype(qg_ref.dtype)
                kd_ref[d, 0, rows, cols] = (kf * jnp.exp(glast - gcol)).astype(kd_ref.dtype)
                a_ref[d, 0, rows, cols] = jnp.where(incl, qk * decay, 0.0).astype(a_ref.dtype)


def _delta_chunks(q, k, v, gb, gt_a):
    bsz, t, nqk = q.shape
    nv = v.shape[2]
    hd = DN_HEAD_DIM
    nh = nqk // hd
    rep = nv // nqk
    tm = ROW_TILE
    qmap = lambda i, h, j: (i, j, h)
    omap = lambda i, h, j: (0, i, j, h)
    oshape = lambda dt: jax.ShapeDtypeStruct((2, bsz, t, nv), dt)
    ospec = pl.BlockSpec((2, 1, tm, rep * hd), omap)
    return pl.pallas_call(
        functools.partial(_delta_chunk_kernel, rep=rep),
        grid=(bsz, nh, t // tm),
        in_specs=[pl.BlockSpec((1, tm, hd), qmap), pl.BlockSpec((1, tm, hd), qmap),
                  pl.BlockSpec((1, tm, rep * hd), qmap),
                  pl.BlockSpec((1, tm, LANES), lambda i, h, j: (i, j, 0)),
                  pl.BlockSpec((2, 1, 1, 8, tm), lambda i, h, j: (0, i, h, 0, j))],
        out_specs=[ospec] * 5,
        out_shape=[oshape(F32), oshape(BF16), oshape(BF16), oshape(BF16), oshape(BF16)],
        compiler_params=_cparams("parallel", "parallel", "parallel"),
    )(q, k, v, gb, gt_a)


def _scan_chunk_index(d, i, ctx_chunks, n_chunks):
    back = jnp.where(i < ctx_chunks, ctx_chunks - 1 - i, n_chunks - 1 - (i - ctx_chunks))
    return jnp.where(d == 0, i, back)


def _delta_scan_kernel(u_ref, w_ref, qg_ref, kd_ref, a_ref, gt_ref, o_ref, s_ref, *, n_heads):
    d = pl.program_id(0)
    i = pl.program_id(3)
    ck = DN_CHUNK
    hd = DN_HEAD_DIM

    @pl.when(i == 0)
    def _():
        s_ref[...] = jnp.zeros_like(s_ref)

    for jh in range(n_heads):
        cols = slice(jh * hd, (jh + 1) * hd)
        grow = gt_ref[0, 0, 0, jh:jh + 1, :]
        glast = jnp.where(d == 0, grow[:, ck - 1:ck], grow[:, 0:1])
        s = s_ref[jh]
        lhs = jnp.concatenate([w_ref[0, 0, :, cols], qg_ref[0, 0, :, cols]], axis=0)
        ws = _dot(lhs, s.astype(BF16))
        v_new = (u_ref[0, 0, :, cols] - ws[:ck]).astype(BF16)
        o = ws[ck:] + _dot(a_ref[0, 0, :, cols], v_new)
        o_ref[0, 0, :, cols] = o.astype(o_ref.dtype)
        s_ref[jh] = s * jnp.exp(glast) + _dot_tn(kd_ref[0, 0, :, cols], v_new)


def _delta_scan(u, w, qg, kd, a, gt_b, ctx_len):
    _, bsz, t, nv = u.shape
    ck = DN_CHUNK
    hd = DN_HEAD_DIM
    nh = DN_SCAN_HEADS
    groups = nv // (nh * hd)
    n_chunks = t // ck
    ctx_chunks = ctx_len // ck
    cmap = lambda d, b, g, i: (d, b, _scan_chunk_index(d, i, ctx_chunks, n_chunks), g)
    spec = pl.BlockSpec((1, 1, ck, nh * hd), cmap)
    return pl.pallas_call(
        functools.partial(_delta_scan_kernel, n_heads=nh),
        grid=(2, bsz, groups, n_chunks),
        in_specs=[spec] * 5 + [pl.BlockSpec((1, 1, 1, 8, ck),
                                            lambda d, b, g, i: (d, b, g, 0, _scan_chunk_index(d, i, ctx_chunks, n_chunks)))],
        out_specs=spec,
        out_shape=jax.ShapeDtypeStruct((2, bsz, t, nv), BF16),
        scratch_shapes=[pltpu.VMEM((nh, hd, hd), F32)],
        compiler_params=_cparams("parallel", "parallel", "parallel", "arbitrary"),
    )(u, w, qg, kd, a, gt_b)


def _pack_bf16_pairs(a):
    n = a.shape[1] // 2
    bits = lax.bitcast_convert_type(a, jnp.uint32)
    return (bits[:, :n] & jnp.uint32(0xFFFF0000)) | (bits[:, n:] >> 16)


def _unpack_bf16_pairs(p):
    hi = lax.bitcast_convert_type(p & jnp.uint32(0xFFFF0000), F32)
    lo = lax.bitcast_convert_type(p << 16, F32)
    return jnp.concatenate([hi, lo], axis=1)


def _router_kernel(x_ref, mod_ref, wr_ref, br_ref, ha_ref, hb_ref, idx_ref, wt_ref, cnt_ref, run_ref):
    first = jnp.logical_and(pl.program_id(0) == 0, pl.program_id(1) == 0)

    @pl.when(first)
    def _():
        run_ref[...] = jnp.zeros_like(run_ref)

    shift = mod_ref[0, 0, 3:4, :]
    scale = mod_ref[0, 0, 4:5, :]
    h = x_ref[0] * (1.0 + scale) + shift
    h_hi = h.astype(BF16)
    packed = _pack_bf16_pairs(h_hi.astype(F32))
    ha_ref[0] = packed[:, :SC_ROW_WORDS]
    hb_ref[0] = packed[:, SC_ROW_WORDS:]
    h_lo = (h - h_hi.astype(F32)).astype(BF16)
    w = wr_ref[...]
    w_hi = w.astype(BF16)
    w_lo = (w - w_hi.astype(F32)).astype(BF16)
    logits = _dot(h_hi, w_hi) + _dot(h_hi, w_lo) + _dot(h_lo, w_hi) + br_ref[...]
    lane = lax.broadcasted_iota(jnp.int32, logits.shape, 1)
    cur = jnp.where(lane < N_EXPERTS, logits, -jnp.inf)
    idx_out = jnp.zeros(logits.shape, jnp.int32)
    wt_out = jnp.zeros(logits.shape, F32)
    tops, sels = [], []
    hits = jnp.zeros(logits.shape, F32)
    for kk in range(TOP_K):
        m = jnp.max(cur, axis=-1, keepdims=True)
        sel = jnp.min(jnp.where(cur == m, lane, LANES), axis=-1, keepdims=True)
        tops.append(m)
        sels.append(sel)
        idx_out = jnp.where(lane == kk, sel, idx_out)
        hits = jnp.where(lane == sel, 1.0, hits)
        cur = jnp.where(lane == sel, -jnp.inf, cur)
    es = [jnp.exp(m - tops[0]) for m in tops]
    den = es[0] + es[1] + es[2] + es[3]
    for kk in range(TOP_K):
        wt_out = jnp.where(lane == kk, es[kk] / den, wt_out)
    tm = logits.shape[0]
    r = lax.broadcasted_iota(jnp.int32, (tm, tm), 0)
    c = lax.broadcasted_iota(jnp.int32, (tm, tm), 1)
    before = _dot(jnp.where(c < r, 1.0, 0.0).astype(BF16), hits.astype(BF16)) + run_ref[...]
    for kk in range(TOP_K):
        rank = jnp.sum(jnp.where(lane == sels[kk], before, 0.0), axis=-1, keepdims=True)
        idx_out = jnp.where(lane == TOP_K + kk, rank.astype(jnp.int32), idx_out)
    total = run_ref[...] + jnp.sum(hits, axis=0, keepdims=True)
    run_ref[...] = total
    cnt_ref[...] = jnp.broadcast_to(total, cnt_ref.shape)
    idx_ref[0] = idx_out
    wt_ref[0] = wt_out


def _router(x, modtab, w_r, b_r, ctx_len, first_block):
    bsz, t, d = x.shape
    tm = ROW_TILE
    ctx_blocks = ctx_len // tm
    nb = t // tm - first_block
    row_in = lambda i, j: (i, j + first_block, 0)
    row = lambda i, j: (i, j, 0)
    const2 = lambda i, j: (0, 0)
    tout = nb * tm
    return pl.pallas_call(
        _router_kernel,
        grid=(bsz, nb),
        in_specs=[pl.BlockSpec((1, tm, d), row_in),
                  pl.BlockSpec((1, 1, 8, d), lambda i, j: (i, _seg_of_block(j + first_block, ctx_blocks), 0, 0)),
                  pl.BlockSpec((d, LANES), const2), pl.BlockSpec((1, LANES), const2)],
        out_specs=[pl.BlockSpec((1, tm, SC_ROW_WORDS), row), pl.BlockSpec((1, tm, SC_ROW_WORDS), row),
                   pl.BlockSpec((1, tm, LANES), row),
                   pl.BlockSpec((1, tm, LANES), row), pl.BlockSpec((8, LANES), const2)],
        out_shape=[jax.ShapeDtypeStruct((bsz, tout, SC_ROW_WORDS), jnp.uint32),
                   jax.ShapeDtypeStruct((bsz, tout, SC_ROW_WORDS), jnp.uint32),
                   jax.ShapeDtypeStruct((bsz, tout, LANES), jnp.int32),
                   jax.ShapeDtypeStruct((bsz, tout, LANES), F32),
                   jax.ShapeDtypeStruct((8, LANES), F32)],
        scratch_shapes=[pltpu.VMEM((1, LANES), F32)],
        compiler_params=_cparams("arbitrary", "arbitrary"),
    )(x, modtab, w_r, b_r)


def _expert_kernel(be_ref, nused_ref, xa_ref, xb_ref, w1_ref, b1_ref, w2_ref, b2_ref, ya_ref, yb_ref,
                   w1b_ref, w2b_ref):
    i = pl.program_id(0)
    dff = w2_ref.shape[1]
    prev = be_ref[jnp.maximum(i - 1, 0)]
    new_expert = jnp.logical_or(i == 0, be_ref[i] != prev)

    @pl.when(jnp.logical_and(new_expert, i < nused_ref[0]))
    def _():
        for r0 in range(0, w1_ref.shape[1], LANES):
            w1b_ref[r0:r0 + LANES, :] = w1_ref[0, r0:r0 + LANES, :].astype(BF16)
        for r0 in range(0, dff, LANES):
            w2b_ref[r0:r0 + LANES, :] = w2_ref[0, r0:r0 + LANES, :].astype(BF16)

    @pl.when(i < nused_ref[0])
    def _():
        x = _unpack_bf16_pairs(jnp.concatenate([xa_ref[...], xb_ref[...]], axis=1)).astype(BF16)
        acc = None
        for c0 in range(0, dff, COL_CHUNK):
            gate = _dot(x, w1b_ref[:, c0:c0 + COL_CHUNK]) + b1_ref[0, :, c0:c0 + COL_CHUNK]
            up = _dot(x, w1b_ref[:, dff + c0:dff + c0 + COL_CHUNK]) + b1_ref[0, :, dff + c0:dff + c0 + COL_CHUNK]
            gate = jnp.minimum(gate, SWIGLU_LIMIT)
            up = jnp.clip(up, -SWIGLU_LIMIT, SWIGLU_LIMIT)
            act = ((up + 1.0) * gate * jax.nn.sigmoid(SWIGLU_ALPHA * gate)).astype(BF16)
            part = _dot(act, w2b_ref[c0:c0 + COL_CHUNK, :])
            acc = part if acc is None else acc + part
        y = (acc + b2_ref[0]).astype(BF16).astype(F32)
        packed = _pack_bf16_pairs(y)
        ya_ref[...] = packed[:, :SC_ROW_WORDS]
        yb_ref[...] = packed[:, SC_ROW_WORDS:]

    @pl.when(i >= nused_ref[0])
    def _():
        ya_ref[...] = jnp.zeros_like(ya_ref)
        yb_ref[...] = jnp.zeros_like(yb_ref)


def _experts(xa, xb, block_e, n_used, w1, b1, w2, b2, layer):
    n_rows, dq = xa.shape
    d = 4 * dq
    tm = EXPERT_TILE
    depth, ne, _, dff2 = w1.shape
    dff = dff2 // 2
    w1 = w1.reshape(depth * ne, d, dff2)
    w2 = w2.reshape(depth * ne, dff, d)
    b1 = b1.reshape(depth * ne, 1, dff2)
    b2 = b2.reshape(depth * ne, 1, d)
    emap = lambda i, be, nu: (layer * ne + be[i], 0, 0)
    return pl.pallas_call(
        _expert_kernel,
        grid_spec=pltpu.PrefetchScalarGridSpec(
            num_scalar_prefetch=2, grid=(n_rows // tm,),
            in_specs=[pl.BlockSpec((tm, dq), lambda i, be, nu: (i, 0)),
                      pl.BlockSpec((tm, dq), lambda i, be, nu: (i, 0)),
                      pl.BlockSpec((1, d, dff2), emap),
                      pl.BlockSpec((1, 1, dff2), emap),
                      pl.BlockSpec((1, dff, d), emap),
                      pl.BlockSpec((1, 1, d), emap)],
            out_specs=[pl.BlockSpec((tm, dq), lambda i, be, nu: (i, 0))] * 2,
            scratch_shapes=[pltpu.VMEM((d, dff2), BF16), pltpu.VMEM((dff, d), BF16)]),
        out_shape=[jax.ShapeDtypeStruct((n_rows, dq), jnp.uint32)] * 2,
        compiler_params=_cparams("arbitrary"),
    )(block_e, n_used, xa, xb, w1, b1, w2, b2)


def _combine_kernel(x_ref, mod_ref, za_ref, zb_ref, wt_ref, g_ref, b_ref, out_ref, *, alpha):
    wt = wt_ref[0]
    f = None
    for kk in range(TOP_K):
        term = _unpack_bf16_pairs(jnp.concatenate([za_ref[kk, 0], zb_ref[kk, 0]], axis=1)) * wt[:, kk:kk + 1]
        f = term if f is None else f + term
    out_ref[0] = _resid_ln(x_ref[0], f, mod_ref[0, 0, 5:6, :], g_ref[...], b_ref[...], alpha)


def _combine(x, modtab, za, zb, wt, ln_g, ln_b, alpha, ctx_len, first_block):
    bsz, t, d = x.shape
    tm = ROW_TILE
    ctx_blocks = ctx_len // tm
    nb = t // tm - first_block
    row_in = lambda i, j: (i, j + first_block, 0)
    row = lambda i, j: (i, j, 0)
    const2 = lambda i, j: (0, 0)
    return pl.pallas_call(
        functools.partial(_combine_kernel, alpha=alpha),
        grid=(bsz, nb),
        in_specs=[pl.BlockSpec((1, tm, d), row_in),
                  pl.BlockSpec((1, 1, 8, d), lambda i, j: (i, _seg_of_block(j + first_block, ctx_blocks), 0, 0)),
                  pl.BlockSpec((TOP_K, 1, tm, SC_ROW_WORDS), lambda i, j: (0, i, j, 0)),
                  pl.BlockSpec((TOP_K, 1, tm, SC_ROW_WORDS), lambda i, j: (0, i, j, 0)),
                  pl.BlockSpec((1, tm, LANES), row),
                  pl.BlockSpec((1, d), const2), pl.BlockSpec((1, d), const2)],
        out_specs=pl.BlockSpec((1, tm, d), row),
        out_shape=jax.ShapeDtypeStruct((bsz, nb * tm, d), F32),
        compiler_params=_cparams("parallel", "parallel"),
    )(x, modtab, za, zb, wt, ln_g, ln_b)


def _moe_layer(x, modtab, w_r, b_r, expert_params, ln_g, ln_b, alpha, ctx_len, latent_only):
    bsz, t, d = x.shape
    first_block = ctx_len // ROW_TILE if latent_only else 0
    wr_pad = jnp.pad(w_r, ((0, 0), (0, LANES - N_EXPERTS)))
    br_pad = jnp.pad(b_r, (0, LANES - N_EXPERTS)).reshape(1, LANES)
    ha, hb, idx, wt, cnt = _router(x, modtab, wr_pad, br_pad, ctx_len, first_block)
    tr = ha.shape[1]
    dq = ha.shape[2]
    n_tok = bsz * tr
    n_assign = n_tok * TOP_K
    tm = EXPERT_TILE
    experts = jnp.arange(N_EXPERTS, dtype=jnp.int32)
    counts = cnt[0, :N_EXPERTS].astype(jnp.int32)
    padded = (counts + tm - 1) // tm * tm
    pends = jnp.cumsum(padded)
    pstarts = pends - padded
    e_tk = idx[:, :, :TOP_K].reshape(n_tok, TOP_K)
    rank = idx[:, :, TOP_K:2 * TOP_K].reshape(n_tok, TOP_K)
    dest = jnp.sum(jnp.where(e_tk[:, :, None] == experts, pstarts, 0), axis=-1) + rank
    n_rows = (-(-n_assign // tm) + N_EXPERTS) * tm
    n_blocks = n_rows // tm
    starts = jnp.arange(n_blocks, dtype=jnp.int32) * tm
    block_e = jnp.minimum(jnp.sum((pends[None, :] <= starts[:, None]).astype(jnp.int32), axis=1), N_EXPERTS - 1)
    n_used = (pends[-1] // tm).astype(jnp.int32).reshape(1)
    dest_t = dest.T
    xa = _sc_scatter_rows(ha.reshape(n_tok, dq), dest_t, n_rows)
    xb = _sc_scatter_rows(hb.reshape(n_tok, dq), dest_t, n_rows)
    ya, yb = _experts(xa, xb, block_e, n_used, *expert_params)
    flat = dest_t.reshape(1, n_assign)
    za = _sc_gather_rows(ya, flat).reshape(TOP_K, bsz, tr, dq)
    zb = _sc_gather_rows(yb, flat).reshape(TOP_K, bsz, tr, dq)
    return _combine(x, modtab, za, zb, wt, ln_g, ln_b, alpha, ctx_len, first_block)


SC_WINDOW = 128
SC_ROW_WORDS = 256


def _sc_mesh():
    return plsc.VectorSubcoreMesh(core_axis_name="c", subcore_axis_name="s")


def _sc_scatter_rows(src, idx, n_out):
    n_src, width = src.shape
    n_k = idx.shape[0]
    per_k = n_src // SC_WINDOW

    @functools.partial(pl.kernel, out_type=jax.ShapeDtypeStruct((n_out, width), src.dtype), mesh=_sc_mesh())
    def scatter(x_hbm, i_hbm, o_hbm):
        def body(x_vmem, i_vmem):
            pltpu.sync_copy(x_vmem, o_hbm.at[i_vmem.at[0]])

        pltpu.emit_pipeline(
            body, grid=(n_k * per_k,),
            in_specs=[pl.BlockSpec((SC_WINDOW, width), lambda i: (i % per_k, 0)),
                      pl.BlockSpec((1, SC_WINDOW), lambda i: (i // per_k, i % per_k))],
            out_specs=[],
            core_axis_name=("c", "s"),
            dimension_semantics=(pltpu.PARALLEL,),
        )(x_hbm, i_hbm)

    return scatter(src, idx)


def _sc_gather_rows(table, idx):
    width = table.shape[1]
    n_idx = idx.shape[1]

    @functools.partial(pl.kernel, out_type=jax.ShapeDtypeStruct((n_idx, width), table.dtype), mesh=_sc_mesh())
    def gather(x_hbm, i_hbm, o_hbm):
        def body(i_vmem, o_vmem):
            pltpu.sync_copy(x_hbm.at[i_vmem.at[0]], o_vmem)

        pltpu.emit_pipeline(
            body, grid=(n_idx // SC_WINDOW,),
            in_specs=[pl.BlockSpec((1, SC_WINDOW), lambda i: (0, i))],
            out_specs=[pl.BlockSpec((SC_WINDOW, width), lambda i: (i, 0))],
            core_axis_name=("c", "s"),
            dimension_semantics=(pltpu.PARALLEL,),
        )(i_hbm, o_hbm)

    return gather(table, idx)


def _rot_cols(w, head_dim):
    lead = w.shape[:-1]
    q = head_dim // 4
    wr = w.reshape(lead + (-1, 4, q))
    out = jnp.stack([-wr[..., 1, :], wr[..., 0, :], -wr[..., 3, :], wr[..., 2, :]], axis=-2)
    return out.reshape(w.shape)


def _dup_heads(w, head_dim):
    lead = w.shape[:-1]
    wr = w.reshape(lead + (-1, 1, head_dim))
    return jnp.concatenate([wr, wr], axis=-2).reshape(lead + (-1,))


def _rope_tables(n_lat, ctx_len, head_dim):
    rows = n_lat // GRID_W
    row = jnp.repeat(jnp.arange(rows, dtype=F32), GRID_W)
    col = jnp.tile(jnp.arange(GRID_W, dtype=F32), rows)
    half = head_dim // 2
    inv = ROPE_BASE ** (-jnp.arange(0, half, 2, dtype=F32) / half)
    ar = row[:, None] * inv
    ac = col[:, None] * inv
    ang = jnp.concatenate([ar, ar, ac, ac], axis=-1)
    cos = jnp.concatenate([jnp.ones((ctx_len, head_dim), F32), jnp.cos(ang)], axis=0)
    sin = jnp.concatenate([jnp.zeros((ctx_len, head_dim), F32), jnp.sin(ang)], axis=0)
    reps = LANES // head_dim
    return jnp.tile(cos, (1, reps)), jnp.tile(sin, (1, reps))


def _conv_mixer(x, modtab, w_in, w_conv, w_out, ln_g, ln_b, alpha, ctx_len):
    d = x.shape[2]
    plan = [("plain", 0, 0, d, 0, None, 1.0), ("mul", 1, 0, d, d, 2 * d, 1.0)]
    bg, p = _project(x, modtab, w_in.astype(BF16), jnp.zeros((1, 3 * d), F32), plan, (d, d), (BF16, BF16), ctx_len)
    cw = jnp.pad(w_conv, ((0, 8 - SC_WIDTH), (0, 0)))
    return _mixer_out("conv", x, modtab, (bg, p), w_out.astype(BF16), jnp.zeros((1, d), F32), ln_g, ln_b, alpha,
                      ctx_len, extra=cw)


def _swa_mixer(x, modtab, w_qkv, b_qkv, sink, w_o, b_o, ln_g, ln_b, alpha, ctx_len, rope):
    d = x.shape[2]
    nq = SWA_HEADS * SWA_HEAD_DIM
    nkv = SWA_KV_HEADS * SWA_HEAD_DIM
    hd = SWA_HEAD_DIM

    def arrange(a):
        q, k, v = a[..., :nq], a[..., nq:nq + nkv], a[..., nq + nkv:]
        kk = _dup_heads(k, hd)
        return jnp.concatenate([q, _rot_cols(q, hd), kk, _rot_cols(kk, hd), _dup_heads(v, hd)], axis=-1)

    w = arrange(w_qkv).astype(BF16)
    b = arrange(b_qkv.reshape(1, -1))
    plan = [("rope", 0, 0, nq, 0, nq, SWA_HEAD_DIM ** -0.5),
            ("rope", 1, 0, 2 * nkv, 2 * nq, 2 * nq + 2 * nkv, 1.0),
            ("plain", 2, 0, 2 * nkv, 2 * nq + 4 * nkv, None, 1.0)]
    q, kk, vv = _project(x, modtab, w, b, plan, (nq, 2 * nkv, 2 * nkv), (BF16, BF16, BF16), ctx_len, rope=rope)
    o = _swa_attention(q, kk, vv, sink.astype(F32), ctx_len)
    return _mixer_out("plain", x, modtab, (o,), w_o.astype(BF16), b_o.reshape(1, d), ln_g, ln_b, alpha, ctx_len)


def _diff_mixer(x, modtab, w_qkv, lam_p, subln_g, w_o, lam_init, ln_g, ln_b, alpha, ctx_len, rope):
    d = x.shape[2]
    hd = DIFF_HEAD_DIM
    wq, wk, wv = w_qkv[:, :d], w_qkv[:, d:2 * d], w_qkv[:, 2 * d:]
    w = jnp.concatenate([wq, _rot_cols(wq, hd), wk, _rot_cols(wk, hd), wv], axis=-1).astype(BF16)
    plan = [("rope", 0, 0, d, 0, d, DIFF_HEAD_DIM ** -0.5),
            ("rope", 1, 0, d, 2 * d, 3 * d, 1.0),
            ("plain", 2, 0, d, 4 * d, None, 1.0)]
    q, k, v = _project(x, modtab, w, jnp.zeros((1, 5 * d), F32), plan, (d, d, d), (BF16, BF16, BF16), ctx_len,
                       rope=rope)
    o = _diff_attention(q, k, v, lam_p.astype(F32), subln_g.astype(F32), ctx_len, lam_init)
    return _mixer_out("plain", x, modtab, (o,), w_o.astype(BF16), jnp.zeros((1, d), F32), ln_g, ln_b, alpha, ctx_len)


def _delta_mixer(x, modtab, w_qkvz, w_ba, a_log, dt_bias, w_conv, norm_g, w_o, ln_g, ln_b, alpha, ctx_len):
    bsz, t, d = x.shape
    nqk = DN_QK_HEADS * DN_HEAD_DIM
    nv = DN_V_HEADS * DN_HEAD_DIM
    nc = 2 * nqk + nv
    nba = w_ba.shape[1]
    w = jnp.concatenate([w_qkvz, jnp.pad(w_ba, ((0, 0), (0, LANES - nba)))], axis=-1).astype(BF16)
    ntot = w.shape[1]
    plan = [("plain", 0, 0, nc, 0, None, 1.0), ("plain", 1, 0, nv, nc, None, 1.0),
            ("plain", 2, 0, LANES, nc + nv, None, 1.0)]
    qkv_pre, z, ba = _project(x, modtab, w, jnp.zeros((1, ntot), F32), plan, (nc, nv, LANES), (BF16, BF16, F32),
                              ctx_len)
    zeros16 = jnp.zeros((DN_V_HEADS,), F32)
    lanes_of = lambda p: jnp.pad(jnp.concatenate([zeros16, p[0], zeros16, p[1]]), (0, LANES - 4 * DN_V_HEADS))
    alog_l = lanes_of(a_log.astype(F32)).reshape(1, LANES)
    dtb_l = lanes_of(dt_bias.astype(F32)).reshape(1, LANES)
    cw = jnp.pad(w_conv, ((0, 8 - DN_CONV), (0, 0)))
    q, k, v, gb = _delta_prep(qkv_pre, cw, ba, alog_l, dtb_l, ctx_len, nqk, nv)
    rep = DN_V_HEADS // DN_QK_HEADS
    gcum = jnp.stack([gb[:, :, DN_V_HEADS:2 * DN_V_HEADS], gb[:, :, 3 * DN_V_HEADS:4 * DN_V_HEADS]])
    gt = jnp.transpose(gcum, (0, 1, 3, 2))
    gt_a = jnp.pad(gt.reshape(2, bsz, DN_QK_HEADS, rep, t), ((0, 0), (0, 0), (0, 0), (0, 8 - rep), (0, 0)))
    gt_b = gt.reshape(2, bsz, DN_V_HEADS // DN_SCAN_HEADS, DN_SCAN_HEADS, t)
    u, w_, qg, kd, a = _delta_chunks(q, k, v, gb, gt_a)
    o2 = _delta_scan(u, w_, qg, kd, a, gt_b, ctx_len)
    return _mixer_out("delta", x, modtab, (o2[0], o2[1], z), w_o.astype(BF16), jnp.zeros((1, d), F32),
                      ln_g, ln_b, alpha, ctx_len, extra=norm_g.astype(F32).reshape(1, DN_HEAD_DIM))


def kernel(x, c, ctx, c_ctx, mod_w, mod_b, ln1_g, ln1_b, ln2_g, ln2_b, router_w, router_b, exp_w1, exp_b1, exp_w2, exp_b2, conv_in_w, conv_w, conv_out_w, swa_qkv_w, swa_qkv_b, swa_sink, swa_out_w, swa_out_b, diff_qkv_w, diff_lambda, diff_subln_g, diff_out_w, delta_qkvz_w, delta_ba_w, delta_a_log, delta_dt_bias, delta_conv_w, delta_norm_g, delta_out_w):
    bsz, n_lat, d = x.shape
    ctx_len = ctx.shape[1]
    depth = mod_w.shape[0]
    alpha = (2 * depth) ** 0.25
    xs = jnp.concatenate([ctx, x], axis=1)
    cc = jnp.zeros((16, d), F32).at[:bsz].set(c).at[bsz].set(c_ctx)
    mods = _mod_vectors(cc, mod_w, mod_b).reshape(depth, 16, 6, d)
    mod_lat = mods[:, :bsz]
    mod_ctx = jnp.broadcast_to(mods[:, bsz:bsz + 1], mod_lat.shape)
    modtabs = jnp.pad(jnp.stack([mod_ctx, mod_lat], axis=2), ((0, 0), (0, 0), (0, 0), (0, 2), (0, 0)))
    rope = _rope_tables(n_lat, ctx_len, SWA_HEAD_DIM)
    row1 = lambda a: a.reshape(1, d)
    for i in range(depth):
        last = i == depth - 1
        kind, j = i % N_MIXERS, i // N_MIXERS
        mt = modtabs[i]
        g1, b1 = row1(ln1_g[i]), row1(ln1_b[i])
        if kind == 0:
            xs = _conv_mixer(xs, mt, conv_in_w[j], conv_w[j], conv_out_w[j], g1, b1, alpha, ctx_len)
        elif kind == 1:
            xs = _swa_mixer(xs, mt, swa_qkv_w[j], swa_qkv_b[j], swa_sink[j], swa_out_w[j], swa_out_b[j], g1, b1,
                            alpha, ctx_len, rope)
        elif kind == 2:
            lam_init = 0.8 - 0.6 * math.exp(-0.3 * i)
            xs = _diff_mixer(xs, mt, diff_qkv_w[j], diff_lambda[j], diff_subln_g[j], diff_out_w[j], lam_init,
                             g1, b1, alpha, ctx_len, rope)
        else:
            xs = _delta_mixer(xs, mt, delta_qkvz_w[j], delta_ba_w[j], delta_a_log[j], delta_dt_bias[j],
                              delta_conv_w[j], delta_norm_g[j], delta_out_w[j], g1, b1, alpha, ctx_len)
        xs = _moe_layer(xs, mt, router_w[i], router_b[i], (exp_w1, exp_b1, exp_w2, exp_b2, i),
                        row1(ln2_g[i]), row1(ln2_b[i]), alpha, ctx_len, latent_only=last)
    return xs
```

```python
import functools
import math

import jax
import jax.numpy as jnp
from jax import lax
from jax.experimental import pallas as pl
from jax.experimental.pallas import tpu as pltpu
from jax.experimental.pallas import tpu_sc as plsc

F32 = jnp.float32
BF16 = jnp.bfloat16

GRID_W = 64
N_MIXERS = 4
LN_EPS = 1e-5
RMS_EPS = 1e-6
NEG_INF = -1e30
ROPE_BASE = 10000.0
SC_WIDTH = 3
SWA_HEADS = 16
SWA_KV_HEADS = 4
SWA_HEAD_DIM = 64
SWA_WINDOW = 128
DIFF_HEADS = 8
DIFF_HEAD_DIM = 64
DN_QK_HEADS = 8
DN_V_HEADS = 16
DN_HEAD_DIM = 128
DN_CONV = 5
N_EXPERTS = 32
TOP_K = 4
SWIGLU_LIMIT = 7.0
SWIGLU_ALPHA = 1.702

LANES = 128
HALO_ROWS = 16
VMEM_LIMIT = 56 * 1024 * 1024

ROW_TILE = 256
ATT_BLOCK = 128
DIFF_Q_BLOCK = 256
DN_CHUNK = 128
EXPERT_TILE = 512
COL_CHUNK = 512


def _cparams(*sem):
    return pltpu.CompilerParams(dimension_semantics=sem, vmem_limit_bytes=VMEM_LIMIT)


def _split3(x):
    hi = x.astype(BF16)
    r1 = x - hi.astype(F32)
    mid = r1.astype(BF16)
    lo = (r1 - mid.astype(F32)).astype(BF16)
    return hi, mid, lo


def _dot(a, b):
    return jnp.dot(a, b, preferred_element_type=F32)


def _dot_nt(a, b):
    return lax.dot_general(a, b, (((1,), (1,)), ((), ())), preferred_element_type=F32)


def _dot_tn(a, b):
    return lax.dot_general(a, b, (((0,), (0,)), ((), ())), preferred_element_type=F32)


def _silu(x):
    return x * jax.nn.sigmoid(x)


def _seg_of_block(j, ctx_blocks):
    return jnp.where(j < ctx_blocks, 0, 1)


def _mods_kernel(c_ref, w_ref, b_ref, o_ref):
    s = _silu(c_ref[...]).astype(BF16)
    o_ref[0] = _dot(s, w_ref[0].astype(BF16)) + b_ref[0]


def _mod_vectors(cc, mod_w, mod_b):
    depth, d, n = mod_w.shape
    rows = cc.shape[0]
    tn = 1536
    return pl.pallas_call(
        _mods_kernel,
        grid=(depth, n // tn),
        in_specs=[pl.BlockSpec((rows, d), lambda l, j: (0, 0)),
                  pl.BlockSpec((1, d, tn), lambda l, j: (l, 0, j)),
                  pl.BlockSpec((1, 1, tn), lambda l, j: (l, 0, j))],
        out_specs=pl.BlockSpec((1, rows, tn), lambda l, j: (l, 0, j)),
        out_shape=jax.ShapeDtypeStruct((depth, rows, n), F32),
        compiler_params=_cparams("parallel", "parallel"),
    )(cc, mod_w, mod_b.reshape(depth, 1, n))


def _proj_kernel(*refs, plan, n_out, use_rope):
    x_ref, mod_ref, w_ref, b_ref = refs[:4]
    pos = 4
    if use_rope:
        cos_ref, sin_ref = refs[4:6]
        pos = 6
    outs = refs[pos:pos + n_out]
    shift = mod_ref[0, 0, 0:1, :]
    scale = mod_ref[0, 0, 1:2, :]
    h = (x_ref[0] * (1.0 + scale) + shift).astype(BF16)

    def acc(col, width):
        return _dot(h, w_ref[:, col:col + width]) + b_ref[:, col:col + width]

    for kind, oi, ocol, width, wcol, wcol2, mult in plan:
        for c0 in range(0, width, COL_CHUNK):
            cw = min(COL_CHUNK, width - c0)
            a = acc(wcol + c0, cw)
            if kind == "rope":
                reps = cw // LANES
                cos = jnp.tile(cos_ref[...], (1, reps))
                sin = jnp.tile(sin_ref[...], (1, reps))
                a = (a * cos + acc(wcol2 + c0, cw) * sin) * mult
            elif kind == "mul":
                a = a * acc(wcol2 + c0, cw)
            outs[oi][0, :, ocol + c0:ocol + c0 + cw] = a.astype(outs[oi].dtype)


def _project(x, modtab, w, b, plan, out_widths, out_dtypes, ctx_len, rope=None):
    bsz, t, d = x.shape
    n = w.shape[1]
    tm = ROW_TILE
    ctx_blocks = ctx_len // tm
    mt = modtab
    in_specs = [pl.BlockSpec((1, tm, d), lambda i, j: (i, j, 0)),
                pl.BlockSpec((1, 1, 8, d), lambda i, j: (i, _seg_of_block(j, ctx_blocks), 0, 0)),
                pl.BlockSpec((d, n), lambda i, j: (0, 0)),
                pl.BlockSpec((1, n), lambda i, j: (0, 0))]
    args = [x, mt, w, b]
    if rope is not None:
        in_specs += [pl.BlockSpec((tm, LANES), lambda i, j: (j, 0))] * 2
        args += list(rope)
    out_specs = [pl.BlockSpec((1, tm, ow), lambda i, j: (i, j, 0)) for ow in out_widths]
    out_shape = [jax.ShapeDtypeStruct((bsz, t, ow), dt) for ow, dt in zip(out_widths, out_dtypes)]
    return pl.pallas_call(
        functools.partial(_proj_kernel, plan=tuple(plan), n_out=len(out_widths), use_rope=rope is not None),
        grid=(bsz, t // tm),
        in_specs=in_specs, out_specs=out_specs, out_shape=out_shape,
        compiler_params=_cparams("parallel", "parallel"),
    )(*args)


def _resid_ln(x, y, gate, g, b, alpha):
    r = alpha * x + gate * y
    mu = jnp.mean(r, axis=-1, keepdims=True)
    rc = r - mu
    var = jnp.mean(rc * rc, axis=-1, keepdims=True)
    return rc * lax.rsqrt(var + LN_EPS) * g + b


def _centred_conv(ext_ref, p, prev, nxt, w_ref, width, has_prev, has_next):
    tm = p.shape[0]
    h = HALO_ROWS
    pad = (width - 1) // 2
    ext_ref[0:h, :] = jnp.where(has_prev, prev, 0.0)
    ext_ref[h:h + tm, :] = p
    ext_ref[h + tm:h + tm + h, :] = jnp.where(has_next, nxt, 0.0)
    acc = None
    for k in range(width):
        term = ext_ref[h - pad + k:h - pad + k + tm, :] * w_ref[k:k + 1, :]
        acc = term if acc is None else acc + term
    return acc


def _seg_edges(j, ctx_blocks, n_blocks):
    has_prev = jnp.logical_and(j != 0, j != ctx_blocks)
    has_next = jnp.logical_and(j != ctx_blocks - 1, j != n_blocks - 1)
    return has_prev, has_next


def _out_plain_kernel(x_ref, mod_ref, o_ref, w_ref, b_ref, g_ref, bb_ref, out_ref, *, alpha):
    y = _dot(o_ref[0], w_ref[...]) + b_ref[...]
    out_ref[0] = _resid_ln(x_ref[0], y, mod_ref[0, 0, 2:3, :], g_ref[...], bb_ref[...], alpha)


def _out_conv_kernel(x_ref, mod_ref, bg_ref, p_ref, pp_ref, pn_ref, cw_ref, w_ref, b_ref, g_ref, bb_ref,
                     out_ref, ext_ref, *, alpha, ctx_blocks, n_blocks):
    j = pl.program_id(1)
    has_prev, has_next = _seg_edges(j, ctx_blocks, n_blocks)
    conv = _centred_conv(ext_ref, p_ref[0].astype(F32), pp_ref[0].astype(F32), pn_ref[0].astype(F32), cw_ref,
                         SC_WIDTH, has_prev, has_next)
    o = (bg_ref[0].astype(F32) * conv).astype(BF16)
    y = _dot(o, w_ref[...]) + b_ref[...]
    out_ref[0] = _resid_ln(x_ref[0], y, mod_ref[0, 0, 2:3, :], g_ref[...], bb_ref[...], alpha)


def _out_delta_kernel(x_ref, mod_ref, of_ref, ob_ref, z_ref, ng_ref, w_ref, b_ref, g_ref, bb_ref,
                      out_ref, *, alpha):
    hd = DN_HEAD_DIM
    ng = ng_ref[...]
    acc = None
    for h0 in range(0, of_ref.shape[2], hd):
        o = of_ref[0, :, h0:h0 + hd].astype(F32) + ob_ref[0, :, h0:h0 + hd].astype(F32)
        o = o * lax.rsqrt(jnp.mean(o * o, axis=-1, keepdims=True) + RMS_EPS) * ng
        o = (o * _silu(z_ref[0, :, h0:h0 + hd].astype(F32))).astype(BF16)
        part = _dot(o, w_ref[h0:h0 + hd, :])
        acc = part if acc is None else acc + part
    y = acc + b_ref[...]
    out_ref[0] = _resid_ln(x_ref[0], y, mod_ref[0, 0, 2:3, :], g_ref[...], bb_ref[...], alpha)


def _mixer_out(kind, x, modtab, acts, w_o, b_o, ln_g, ln_b, alpha, ctx_len, extra=None):
    bsz, t, d = x.shape
    tm = ROW_TILE
    nb = t // tm
    ctx_blocks = ctx_len // tm
    kin = w_o.shape[0]
    row = lambda i, j: (i, j, 0)
    const2 = lambda i, j: (0, 0)
    x_spec = pl.BlockSpec((1, tm, d), row)
    mod_spec = pl.BlockSpec((1, 1, 8, d), lambda i, j: (i, _seg_of_block(j, ctx_blocks), 0, 0))
    tail_specs = [pl.BlockSpec((kin, d), const2), pl.BlockSpec((1, d), const2),
                  pl.BlockSpec((1, d), const2), pl.BlockSpec((1, d), const2)]
    tail_args = [w_o, b_o, ln_g, ln_b]
    scratch = []
    if kind == "plain":
        body = functools.partial(_out_plain_kernel, alpha=alpha)
        in_specs = [x_spec, mod_spec, pl.BlockSpec((1, tm, kin), row)] + tail_specs
        args = [x, modtab, acts[0]] + tail_args
    elif kind == "conv":
        body = functools.partial(_out_conv_kernel, alpha=alpha, ctx_blocks=ctx_blocks, n_blocks=nb)
        hb = tm // HALO_ROWS
        last_halo = t // HALO_ROWS - 1
        in_specs = [x_spec, mod_spec, pl.BlockSpec((1, tm, d), row), pl.BlockSpec((1, tm, d), row),
                    pl.BlockSpec((1, HALO_ROWS, d), lambda i, j: (i, jnp.maximum(j * hb - 1, 0), 0)),
                    pl.BlockSpec((1, HALO_ROWS, d), lambda i, j: (i, jnp.minimum((j + 1) * hb, last_halo), 0)),
                    pl.BlockSpec((8, d), const2)] + tail_specs
        args = [x, modtab, acts[0], acts[1], acts[1], acts[1], extra] + tail_args
        scratch = [pltpu.VMEM((tm + 2 * HALO_ROWS, d), F32)]
    else:
        body = functools.partial(_out_delta_kernel, alpha=alpha)
        in_specs = [x_spec, mod_spec, pl.BlockSpec((1, tm, kin), row), pl.BlockSpec((1, tm, kin), row),
                    pl.BlockSpec((1, tm, kin), row), pl.BlockSpec((1, DN_HEAD_DIM), const2)] + tail_specs
        args = [x, modtab, acts[0], acts[1], acts[2], extra] + tail_args
    return pl.pallas_call(
        body, grid=(bsz, nb), in_specs=in_specs,
        out_specs=pl.BlockSpec((1, tm, d), row),
        out_shape=jax.ShapeDtypeStruct((bsz, t, d), F32),
        scratch_shapes=scratch,
        compiler_params=_cparams("parallel", "parallel"),
    )(*args)


def _swa_kernel(sink_ref, q_ref, kp_ref, ko_ref, kn_ref, kc_ref, vp_ref, vo_ref, vn_ref, vc_ref, o_ref,
                *, n_lat_blocks, ctx_blocks):
    j = pl.program_id(1)
    blk = ATT_BLOCK
    lat = j - ctx_blocks
    is_lat = j >= ctx_blocks
    n_keys = 3 * blk + kc_ref.shape[1]
    r = lax.broadcasted_iota(jnp.int32, (blk, n_keys), 0)
    c = lax.broadcasted_iota(jnp.int32, (blk, n_keys), 1)
    ok_prev = jnp.logical_and(is_lat, lat > 0)
    ok_next = jnp.logical_and(is_lat, lat < n_lat_blocks - 1)
    r_max = jnp.where(c < blk, jnp.where(ok_prev, c, -1),
                      jnp.where(c < 2 * blk, jnp.where(is_lat, blk, -1),
                                jnp.where(c < 3 * blk, jnp.where(ok_next, blk, -1), blk)))
    r_min = jnp.where(c < 2 * blk, 0, jnp.where(c < 3 * blk, c - 2 * blk, 0))
    valid = jnp.logical_and(r >= r_min, r <= r_max)
    lane = lax.broadcasted_iota(jnp.int32, (blk, LANES), 1)
    lo = lane < SWA_HEAD_DIM
    group = SWA_HEADS // SWA_KV_HEADS
    for g in range(SWA_KV_HEADS):
        gs = slice(g * LANES, (g + 1) * LANES)
        kcat = jnp.concatenate([kp_ref[0, :, gs], ko_ref[0, :, gs], kn_ref[0, :, gs], kc_ref[0, :, gs]], axis=0)
        vcat = jnp.concatenate([vp_ref[0, :, gs], vo_ref[0, :, gs], vn_ref[0, :, gs], vc_ref[0, :, gs]], axis=0)
        pairs = range(g * group // 2, (g + 1) * group // 2)
        qhs, sinks = [], []
        for pair in pairs:
            qp = q_ref[0, :, pair * LANES:(pair + 1) * LANES]
            zero = jnp.zeros_like(qp)
            qhs += [jnp.where(lo, qp, zero), jnp.where(lo, zero, qp)]
            sinks += [sink_ref[2 * pair], sink_ref[2 * pair + 1]]
        ss = [jnp.where(valid, _dot_nt(qh, kcat), NEG_INF) for qh in qhs]
        ms = [jnp.maximum(jnp.max(s, axis=-1, keepdims=True), sink) for s, sink in zip(ss, sinks)]
        es = [jnp.exp(s - m) for s, m in zip(ss, ms)]
        dens = [jnp.sum(e, axis=-1, keepdims=True) + jnp.exp(sink - m) for e, m, sink in zip(es, ms, sinks)]
        ps = [(e * (1.0 / den)).astype(BF16) for e, den in zip(es, dens)]
        accs = [_dot(p, vcat) for p in ps]
        for n, pair in enumerate(pairs):
            o_ref[0, :, pair * LANES:(pair + 1) * LANES] = jnp.where(lo, accs[2 * n], accs[2 * n + 1]).astype(o_ref.dtype)


def _swa_attention(q, kk, vv, sink, ctx_len):
    bsz, t, dq = q.shape
    dk = kk.shape[2]
    blk = ATT_BLOCK
    nb = t // blk
    ctx_blocks = ctx_len // blk
    n_lat_blocks = nb - ctx_blocks

    def prev_map(i, j, s):
        return (i, jnp.clip(j - 1, ctx_blocks, nb - 1), 0)

    def own_map(i, j, s):
        return (i, j, 0)

    def next_map(i, j, s):
        return (i, jnp.clip(j + 1, ctx_blocks, nb - 1), 0)

    def ctx_map(i, j, s):
        return (i, 0, 0)

    kv_specs = [pl.BlockSpec((1, blk, dk), prev_map), pl.BlockSpec((1, blk, dk), own_map),
                pl.BlockSpec((1, blk, dk), next_map), pl.BlockSpec((1, ctx_len, dk), ctx_map)]
    return pl.pallas_call(
        functools.partial(_swa_kernel, n_lat_blocks=n_lat_blocks, ctx_blocks=ctx_blocks),
        grid_spec=pltpu.PrefetchScalarGridSpec(
            num_scalar_prefetch=1, grid=(bsz, nb),
            in_specs=[pl.BlockSpec((1, blk, dq), own_map)] + kv_specs + kv_specs,
            out_specs=pl.BlockSpec((1, blk, dq), own_map)),
        out_shape=jax.ShapeDtypeStruct((bsz, t, dq), BF16),
        compiler_params=_cparams("parallel", "parallel"),
    )(sink, q, kk, kk, kk, kk, vv, vv, vv, vv)


def _diff_kernel(lam_ref, q_ref, k_ref, v_ref, g_ref, o_ref, *, ctx_len, lam_init):
    j = pl.program_id(2)
    blk = DIFF_Q_BLOCK
    ctx_blocks = ctx_len // blk
    lp = lam_ref[...]
    lam = (jnp.exp(jnp.sum(lp[0:1, :] * lp[1:2, :], axis=-1, keepdims=True))
           - jnp.exp(jnp.sum(lp[2:3, :] * lp[3:4, :], axis=-1, keepdims=True)) + lam_init)
    lane = lax.broadcasted_iota(jnp.int32, (blk, LANES), 1)
    lo = lane < DIFF_HEAD_DIM
    qp = q_ref[0]
    zero = jnp.zeros_like(qp)

    def attend(n_keys):
        k = k_ref[0, 0:n_keys, :]
        v = v_ref[0, 0:n_keys, :]
        qhs = [jnp.where(lo, qp, zero), jnp.where(lo, zero, qp)]
        ss = [_dot_nt(qh, k) for qh in qhs]
        es = [jnp.exp(s - jnp.max(s, axis=-1, keepdims=True)) for s in ss]
        invs = [1.0 / jnp.sum(e, axis=-1, keepdims=True) for e in es]
        pvs = [_dot(e.astype(BF16), v) for e in es]
        o = pvs[0] * invs[0] - (lam * invs[1]) * pvs[1]
        o = o * lax.rsqrt(jnp.mean(o * o, axis=-1, keepdims=True) + RMS_EPS) * g_ref[...]
        o_ref[0] = (o * (1.0 - lam_init)).astype(o_ref.dtype)

    @pl.when(j < ctx_blocks)
    def _():
        attend(ctx_len)

    @pl.when(j >= ctx_blocks)
    def _():
        attend(k_ref.shape[1])


def _diff_attention(q, k, v, lam_p, subln_g, ctx_len, lam_init):
    bsz, t, dq = q.shape
    blk = DIFF_Q_BLOCK
    nh = dq // LANES
    qmap = lambda i, h, j: (i, j, h)
    kmap = lambda i, h, j: (i, 0, h)
    const2 = lambda i, h, j: (0, 0)
    return pl.pallas_call(
        functools.partial(_diff_kernel, ctx_len=ctx_len, lam_init=lam_init),
        grid=(bsz, nh, t // blk),
        in_specs=[pl.BlockSpec((8, DIFF_HEAD_DIM), const2),
                  pl.BlockSpec((1, blk, LANES), qmap),
                  pl.BlockSpec((1, t, LANES), kmap),
                  pl.BlockSpec((1, t, LANES), kmap),
                  pl.BlockSpec((1, LANES), const2)],
        out_specs=pl.BlockSpec((1, blk, LANES), qmap),
        out_shape=jax.ShapeDtypeStruct((bsz, t, dq), BF16),
        compiler_params=_cparams("parallel", "parallel", "parallel"),
    )(jnp.pad(lam_p, ((0, 4), (0, 0))), q, k, v, subln_g.reshape(1, LANES))


def _delta_prep_kernel(x_ref, xp_ref, xn_ref, cw_ref, ba_ref, alog_ref, dtb_ref,
                       q_ref, k_ref, v_ref, gb_ref, ext_ref, *, ctx_blocks, n_blocks, nqk):
    j = pl.program_id(1)
    has_prev, has_next = _seg_edges(j, ctx_blocks, n_blocks)
    hd = DN_HEAD_DIM
    tm = x_ref.shape[1]
    for c0 in range(0, x_ref.shape[2], COL_CHUNK):
        cs = slice(c0, c0 + COL_CHUNK)
        conv = _centred_conv(ext_ref, x_ref[0, :, cs].astype(F32), xp_ref[0, :, cs].astype(F32),
                             xn_ref[0, :, cs].astype(F32), cw_ref.at[:, cs], DN_CONV, has_prev, has_next)
        a = _silu(conv)
        for h0 in range(0, COL_CHUNK, hd):
            col = c0 + h0
            ah = a[:, h0:h0 + hd]
            if col < 2 * nqk:
                ah = ah * lax.rsqrt(jnp.sum(ah * ah, axis=-1, keepdims=True) + RMS_EPS)
                if col < nqk:
                    q_ref[0, :, col:col + hd] = (ah * (hd ** -0.5)).astype(q_ref.dtype)
                else:
                    k_ref[0, :, col - nqk:col - nqk + hd] = ah.astype(k_ref.dtype)
            else:
                v_ref[0, :, col - 2 * nqk:col - 2 * nqk + hd] = ah.astype(v_ref.dtype)
    ba = ba_ref[0]
    lane = lax.broadcasted_iota(jnp.int32, ba.shape, 1)
    is_beta = (lane % 32) < DN_V_HEADS
    z = ba + dtb_ref[...]
    softplus = jnp.maximum(z, 0.0) + jnp.log(1.0 + jnp.exp(-jnp.abs(z)))
    g = -jnp.exp(alog_ref[...]) * softplus
    r = lax.broadcasted_iota(jnp.int32, (tm, tm), 0)
    c = lax.broadcasted_iota(jnp.int32, (tm, tm), 1)
    same = (r // DN_CHUNK) == (c // DN_CHUNK)
    tri_f = jnp.where(jnp.logical_and(same, c <= r), 1.0, 0.0).astype(BF16)
    tri_b = jnp.where(jnp.logical_and(same, c >= r), 1.0, 0.0).astype(BF16)
    parts = _split3(g)
    cum_f = _dot(tri_f, parts[0]) + _dot(tri_f, parts[1]) + _dot(tri_f, parts[2])
    cum_b = _dot(tri_b, parts[0]) + _dot(tri_b, parts[1]) + _dot(tri_b, parts[2])
    gcum = jnp.where(lane < 32, cum_f, cum_b)
    gb_ref[0] = jnp.where(is_beta, jax.nn.sigmoid(ba), gcum)


def _delta_prep(qkv_pre, conv_w, ba, alog_l, dtb_l, ctx_len, nqk, nv):
    bsz, t, nc = qkv_pre.shape
    tm = ROW_TILE
    nb = t // tm
    ctx_blocks = ctx_len // tm
    hb = tm // HALO_ROWS
    last_halo = t // HALO_ROWS - 1
    row = lambda i, j: (i, j, 0)
    const2 = lambda i, j: (0, 0)
    return pl.pallas_call(
        functools.partial(_delta_prep_kernel, ctx_blocks=ctx_blocks, n_blocks=nb, nqk=nqk),
        grid=(bsz, nb),
        in_specs=[pl.BlockSpec((1, tm, nc), row),
                  pl.BlockSpec((1, HALO_ROWS, nc), lambda i, j: (i, jnp.maximum(j * hb - 1, 0), 0)),
                  pl.BlockSpec((1, HALO_ROWS, nc), lambda i, j: (i, jnp.minimum((j + 1) * hb, last_halo), 0)),
                  pl.BlockSpec((8, nc), const2),
                  pl.BlockSpec((1, tm, LANES), row),
                  pl.BlockSpec((1, LANES), const2),
                  pl.BlockSpec((1, LANES), const2)],
        out_specs=[pl.BlockSpec((1, tm, nqk), row), pl.BlockSpec((1, tm, nqk), row),
                   pl.BlockSpec((1, tm, nv), row), pl.BlockSpec((1, tm, LANES), row)],
        out_shape=[jax.ShapeDtypeStruct((bsz, t, nqk), BF16), jax.ShapeDtypeStruct((bsz, t, nqk), BF16),
                   jax.ShapeDtypeStruct((bsz, t, nv), BF16), jax.ShapeDtypeStruct((bsz, t, LANES), F32)],
        scratch_shapes=[pltpu.VMEM((tm + 2 * HALO_ROWS, COL_CHUNK), F32)],
        compiler_params=_cparams("parallel", "parallel"),
    )(qkv_pre, qkv_pre, qkv_pre, conv_w, ba, alog_l, dtb_l)


DN_INV_BLOCK = 16


def _unit_triangular_inverses(lows, eye, r, c):
    n = lows[0].shape[0]
    b = DN_INV_BLOCK
    same = (r // b) == (c // b)
    pws = [jnp.where(same, low, 0.0) for low in lows]
    xs = [eye - pw for pw in pws]
    for _ in range(int(math.log2(b)) - 1):
        pbs = [pw.astype(BF16) for pw in pws]
        pws = [_dot(pb, pb) for pb in pbs]
        xs = [x + _dot(x.astype(BF16), pw.astype(BF16)) for x, pw in zip(xs, pws)]
    while b < n:
        pair = jnp.logical_and((r // (2 * b)) == (c // (2 * b)), (r // b) != (c // b))
        xbs = [x.astype(BF16) for x in xs]
        ys = [_dot(jnp.where(pair, low, 0.0).astype(BF16), xb).astype(BF16) for low, xb in zip(lows, xbs)]
        xs = [x - _dot(xb, y) for x, xb, y in zip(xs, xbs, ys)]
        b *= 2
    return xs


DN_SCAN_HEADS = 8


def _delta_chunk_kernel(q_ref, k_ref, v_ref, gb_ref, gt_ref, u_ref, w_ref, qg_ref, kd_ref, a_ref, *, rep):
    hq = pl.program_id(1)
    ck = DN_CHUNK
    hd = DN_HEAD_DIM
    r = lax.broadcasted_iota(jnp.int32, (ck, ck), 0)
    c = lax.broadcasted_iota(jnp.int32, (ck, ck), 1)
    eye = jnp.where(r == c, 1.0, 0.0)
    lane = lax.broadcasted_iota(jnp.int32, (ck, LANES), 1)
    lows, rhss, dests = [], [], []
    for sub in range(q_ref.shape[1] // ck):
        rows = slice(sub * ck, (sub + 1) * ck)
        q = q_ref[0, rows, :]
        k = k_ref[0, rows, :]
        kk = _dot_nt(k, k)
        qk = _dot_nt(q, k)
        gb = gb_ref[0, rows, :]
        qf = q.astype(F32)
        kf = k.astype(F32)
        for d in range(2):
            incl = (r >= c) if d == 0 else (r <= c)
            strict = (r > c) if d == 0 else (r < c)
            for jh in range(rep):
                head = hq * rep + jh
                cols = slice(jh * hd, (jh + 1) * hd)
                bcol = jnp.sum(jnp.where(lane == d * 32 + head, gb, 0.0), axis=-1, keepdims=True)
                gcol = jnp.sum(jnp.where(lane == d * 32 + DN_V_HEADS + head, gb, 0.0), axis=-1, keepdims=True)
                grow = gt_ref[d, 0, 0, jh:jh + 1, rows]
                glast = grow[:, ck - 1:ck] if d == 0 else grow[:, 0:1]
                decay = jnp.where(incl, jnp.exp(jnp.where(incl, gcol - grow, 0.0)), 0.0)
                lows.append(jnp.where(strict, kk * decay, 0.0) * bcol)
                v = v_ref[0, rows, cols].astype(F32)
                eg = jnp.exp(gcol)
                rhss.append(jnp.concatenate([v * bcol, kf * (bcol * eg)], axis=1).astype(BF16))
                dests.append((d, rows, cols))
                qg_ref[d, 0, rows, cols] = (qf * eg).astype(qg_ref.dtype)
                kd_ref[d, 0, rows, cols] = (kf * jnp.exp(glast - gcol)).astype(kd_ref.dtype)
                a_ref[d, 0, rows, cols] = jnp.where(incl, qk * decay, 0.0).astype(a_ref.dtype)
    tinvs = _unit_triangular_inverses(lows, eye, r, c)
    uws = [_dot(tinv.astype(BF16), rhs) for tinv, rhs in zip(tinvs, rhss)]
    for uw, (d, rows, cols) in zip(uws, dests):
        u_ref[d, 0, rows, cols] = uw[:, :hd]
        w_ref[d, 0, rows, cols] = uw[:, hd:].astype(w_ref.dtype)


def _delta_chunks(q, k, v, gb, gt_a):
    bsz, t, nqk = q.shape
    nv = v.shape[2]
    hd = DN_HEAD_DIM
    nh = nqk // hd
    rep = nv // nqk
    tm = ROW_TILE
    qmap = lambda i, h, j: (i, j, h)
    omap = lambda i, h, j: (0, i, j, h)
    oshape = lambda dt: jax.ShapeDtypeStruct((2, bsz, t, nv), dt)
    ospec = pl.BlockSpec((2, 1, tm, rep * hd), omap)
    return pl.pallas_call(
        functools.partial(_delta_chunk_kernel, rep=rep),
        grid=(bsz, nh, t // tm),
        in_specs=[pl.BlockSpec((1, tm, hd), qmap), pl.BlockSpec((1, tm, hd), qmap),
                  pl.BlockSpec((1, tm, rep * hd), qmap),
                  pl.BlockSpec((1, tm, LANES), lambda i, h, j: (i, j, 0)),
                  pl.BlockSpec((2, 1, 1, 8, tm), lambda i, h, j: (0, i, h, 0, j))],
        out_specs=[ospec] * 5,
        out_shape=[oshape(F32), oshape(BF16), oshape(BF16), oshape(BF16), oshape(BF16)],
        compiler_params=_cparams("parallel", "parallel", "parallel"),
    )(q, k, v, gb, gt_a)


def _scan_chunk_index(d, i, ctx_chunks, n_chunks):
    back = jnp.where(i < ctx_chunks, ctx_chunks - 1 - i, n_chunks - 1 - (i - ctx_chunks))
    return jnp.where(d == 0, i, back)


def _delta_scan_kernel(u_ref, w_ref, qg_ref, kd_ref, a_ref, gt_ref, o_ref, s_ref, *, n_heads):
    d = pl.program_id(0)
    i = pl.program_id(3)
    ck = DN_CHUNK
    hd = DN_HEAD_DIM

    @pl.when(i == 0)
    def _():
        s_ref[...] = jnp.zeros_like(s_ref)

    heads = range(n_heads)
    cols = [slice(jh * hd, (jh + 1) * hd) for jh in heads]
    grows = gt_ref[0, 0, 0]
    glast = jnp.where(d == 0, grows[:, ck - 1:ck], grows[:, 0:1])
    carry = jnp.exp(glast)
    ss = [s_ref[jh] for jh in heads]
    wss = [_dot(jnp.concatenate([w_ref[0, 0, :, cs], qg_ref[0, 0, :, cs]], axis=0), s.astype(BF16))
           for cs, s in zip(cols, ss)]
    v_news = [(u_ref[0, 0, :, cs] - ws[:ck]).astype(BF16) for cs, ws in zip(cols, wss)]
    outs = [ws[ck:] + _dot(a_ref[0, 0, :, cs], vn) for cs, ws, vn in zip(cols, wss, v_news)]
    upds = [_dot_tn(kd_ref[0, 0, :, cs], vn) for cs, vn in zip(cols, v_news)]
    for jh in heads:
        o_ref[0, 0, :, cols[jh]] = outs[jh].astype(o_ref.dtype)
        s_ref[jh] = ss[jh] * carry[jh:jh + 1, :] + upds[jh]


def _delta_scan(u, w, qg, kd, a, gt_b, ctx_len):
    _, bsz, t, nv = u.shape
    ck = DN_CHUNK
    hd = DN_HEAD_DIM
    nh = DN_SCAN_HEADS
    groups = nv // (nh * hd)
    n_chunks = t // ck
    ctx_chunks = ctx_len // ck
    cmap = lambda d, b, g, i: (d, b, _scan_chunk_index(d, i, ctx_chunks, n_chunks), g)
    spec = pl.BlockSpec((1, 1, ck, nh * hd), cmap)
    return pl.pallas_call(
        functools.partial(_delta_scan_kernel, n_heads=nh),
        grid=(2, bsz, groups, n_chunks),
        in_specs=[spec] * 5 + [pl.BlockSpec((1, 1, 1, 8, ck),
                                            lambda d, b, g, i: (d, b, g, 0, _scan_chunk_index(d, i, ctx_chunks, n_chunks)))],
        out_specs=spec,
        out_shape=jax.ShapeDtypeStruct((2, bsz, t, nv), BF16),
        scratch_shapes=[pltpu.VMEM((nh, hd, hd), F32)],
        compiler_params=_cparams("parallel", "parallel", "parallel", "arbitrary"),
    )(u, w, qg, kd, a, gt_b)


def _pack_bf16_pairs(a):
    n = a.shape[1] // 2
    bits = lax.bitcast_convert_type(a, jnp.uint32)
    return (bits[:, :n] & jnp.uint32(0xFFFF0000)) | (bits[:, n:] >> 16)


def _unpack_bf16_pairs(p):
    hi = lax.bitcast_convert_type(p & jnp.uint32(0xFFFF0000), F32)
    lo = lax.bitcast_convert_type(p << 16, F32)
    return jnp.concatenate([hi, lo], axis=1)


def _router_kernel(x_ref, mod_ref, wr_ref, br_ref, ha_ref, hb_ref, idx_ref, wt_ref, cnt_ref, run_ref):
    first = jnp.logical_and(pl.program_id(0) == 0, pl.program_id(1) == 0)

    @pl.when(first)
    def _():
        run_ref[...] = jnp.zeros_like(run_ref)

    shift = mod_ref[0, 0, 3:4, :]
    scale = mod_ref[0, 0, 4:5, :]
    h = x_ref[0] * (1.0 + scale) + shift
    h_hi = h.astype(BF16)
    packed = _pack_bf16_pairs(h_hi.astype(F32))
    ha_ref[0] = packed[:, :SC_ROW_WORDS]
    hb_ref[0] = packed[:, SC_ROW_WORDS:]
    h_lo = (h - h_hi.astype(F32)).astype(BF16)
    w = wr_ref[...]
    w_hi = w.astype(BF16)
    w_lo = (w - w_hi.astype(F32)).astype(BF16)
    logits = _dot(h_hi, w_hi) + _dot(h_hi, w_lo) + _dot(h_lo, w_hi) + br_ref[...]
    lane = lax.broadcasted_iota(jnp.int32, logits.shape, 1)
    cur = jnp.where(lane < N_EXPERTS, logits, -jnp.inf)
    idx_out = jnp.zeros(logits.shape, jnp.int32)
    wt_out = jnp.zeros(logits.shape, F32)
    tops, sels = [], []
    hits = jnp.zeros(logits.shape, F32)
    for kk in range(TOP_K):
        m = jnp.max(cur, axis=-1, keepdims=True)
        sel = jnp.min(jnp.where(cur == m, lane, LANES), axis=-1, keepdims=True)
        tops.append(m)
        sels.append(sel)
        idx_out = jnp.where(lane == kk, sel, idx_out)
        hits = jnp.where(lane == sel, 1.0, hits)
        cur = jnp.where(lane == sel, -jnp.inf, cur)
    es = [jnp.exp(m - tops[0]) for m in tops]
    den = es[0] + es[1] + es[2] + es[3]
    for kk in range(TOP_K):
        wt_out = jnp.where(lane == kk, es[kk] / den, wt_out)
    tm = logits.shape[0]
    r = lax.broadcasted_iota(jnp.int32, (tm, tm), 0)
    c = lax.broadcasted_iota(jnp.int32, (tm, tm), 1)
    before = _dot(jnp.where(c < r, 1.0, 0.0).astype(BF16), hits.astype(BF16)) + run_ref[...]
    for kk in range(TOP_K):
        rank = jnp.sum(jnp.where(lane == sels[kk], before, 0.0), axis=-1, keepdims=True)
        idx_out = jnp.where(lane == TOP_K + kk, rank.astype(jnp.int32), idx_out)
    total = run_ref[...] + jnp.sum(hits, axis=0, keepdims=True)
    run_ref[...] = total
    cnt_ref[...] = jnp.broadcast_to(total, cnt_ref.shape)
    idx_ref[0] = idx_out
    wt_ref[0] = wt_out


def _router(x, modtab, w_r, b_r, ctx_len, first_block):
    bsz, t, d = x.shape
    tm = ROW_TILE
    ctx_blocks = ctx_len // tm
    nb = t // tm - first_block
    row_in = lambda i, j: (i, j + first_block, 0)
    row = lambda i, j: (i, j, 0)
    const2 = lambda i, j: (0, 0)
    tout = nb * tm
    return pl.pallas_call(
        _router_kernel,
        grid=(bsz, nb),
        in_specs=[pl.BlockSpec((1, tm, d), row_in),
                  pl.BlockSpec((1, 1, 8, d), lambda i, j: (i, _seg_of_block(j + first_block, ctx_blocks), 0, 0)),
                  pl.BlockSpec((d, LANES), const2), pl.BlockSpec((1, LANES), const2)],
        out_specs=[pl.BlockSpec((1, tm, SC_ROW_WORDS), row), pl.BlockSpec((1, tm, SC_ROW_WORDS), row),
                   pl.BlockSpec((1, tm, LANES), row),
                   pl.BlockSpec((1, tm, LANES), row), pl.BlockSpec((8, LANES), const2)],
        out_shape=[jax.ShapeDtypeStruct((bsz, tout, SC_ROW_WORDS), jnp.uint32),
                   jax.ShapeDtypeStruct((bsz, tout, SC_ROW_WORDS), jnp.uint32),
                   jax.ShapeDtypeStruct((bsz, tout, LANES), jnp.int32),
                   jax.ShapeDtypeStruct((bsz, tout, LANES), F32),
                   jax.ShapeDtypeStruct((8, LANES), F32)],
        scratch_shapes=[pltpu.VMEM((1, LANES), F32)],
        compiler_params=_cparams("arbitrary", "arbitrary"),
    )(x, modtab, w_r, b_r)


def _expert_kernel(be_ref, nused_ref, xa_ref, xb_ref, w1_ref, b1_ref, w2_ref, b2_ref, ya_ref, yb_ref,
                   w1b_ref, w2b_ref):
    i = pl.program_id(0)
    dff = w2_ref.shape[1]
    prev = be_ref[jnp.maximum(i - 1, 0)]
    new_expert = jnp.logical_or(i == 0, be_ref[i] != prev)

    @pl.when(jnp.logical_and(new_expert, i < nused_ref[0]))
    def _():
        for r0 in range(0, w1_ref.shape[1], LANES):
            w1b_ref[r0:r0 + LANES, :] = w1_ref[0, r0:r0 + LANES, :].astype(BF16)
        for r0 in range(0, dff, LANES):
            w2b_ref[r0:r0 + LANES, :] = w2_ref[0, r0:r0 + LANES, :].astype(BF16)

    @pl.when(i < nused_ref[0])
    def _():
        x = _unpack_bf16_pairs(jnp.concatenate([xa_ref[...], xb_ref[...]], axis=1)).astype(BF16)
        chunks = range(0, dff, COL_CHUNK)
        gates = [_dot(x, w1b_ref[:, c0:c0 + COL_CHUNK]) + b1_ref[0, :, c0:c0 + COL_CHUNK] for c0 in chunks]
        ups = [_dot(x, w1b_ref[:, dff + c0:dff + c0 + COL_CHUNK]) + b1_ref[0, :, dff + c0:dff + c0 + COL_CHUNK]
               for c0 in chunks]
        gates = [jnp.minimum(gate, SWIGLU_LIMIT) for gate in gates]
        ups = [jnp.clip(up, -SWIGLU_LIMIT, SWIGLU_LIMIT) for up in ups]
        acts = [((up + 1.0) * gate * jax.nn.sigmoid(SWIGLU_ALPHA * gate)).astype(BF16) for gate, up in zip(gates, ups)]
        parts = [_dot(act, w2b_ref[c0:c0 + COL_CHUNK, :]) for act, c0 in zip(acts, chunks)]
        acc = parts[0]
        for part in parts[1:]:
            acc = acc + part
        y = (acc + b2_ref[0]).astype(BF16).astype(F32)
        packed = _pack_bf16_pairs(y)
        ya_ref[...] = packed[:, :SC_ROW_WORDS]
        yb_ref[...] = packed[:, SC_ROW_WORDS:]

    @pl.when(i >= nused_ref[0])
    def _():
        ya_ref[...] = jnp.zeros_like(ya_ref)
        yb_ref[...] = jnp.zeros_like(yb_ref)


def _experts(xa, xb, block_e, n_used, w1, b1, w2, b2, layer):
    n_rows, dq = xa.shape
    d = 4 * dq
    tm = EXPERT_TILE
    depth, ne, _, dff2 = w1.shape
    dff = dff2 // 2
    w1 = w1.reshape(depth * ne, d, dff2)
    w2 = w2.reshape(depth * ne, dff, d)
    b1 = b1.reshape(depth * ne, 1, dff2)
    b2 = b2.reshape(depth * ne, 1, d)
    emap = lambda i, be, nu: (layer * ne + be[i], 0, 0)
    return pl.pallas_call(
        _expert_kernel,
        grid_spec=pltpu.PrefetchScalarGridSpec(
            num_scalar_prefetch=2, grid=(n_rows // tm,),
            in_specs=[pl.BlockSpec((tm, dq), lambda i, be, nu: (i, 0)),
                      pl.BlockSpec((tm, dq), lambda i, be, nu: (i, 0)),
                      pl.BlockSpec((1, d, dff2), emap),
                      pl.BlockSpec((1, 1, dff2), emap),
                      pl.BlockSpec((1, dff, d), emap),
                      pl.BlockSpec((1, 1, d), emap)],
            out_specs=[pl.BlockSpec((tm, dq), lambda i, be, nu: (i, 0))] * 2,
            scratch_shapes=[pltpu.VMEM((d, dff2), BF16), pltpu.VMEM((dff, d), BF16)]),
        out_shape=[jax.ShapeDtypeStruct((n_rows, dq), jnp.uint32)] * 2,
        compiler_params=_cparams("arbitrary"),
    )(block_e, n_used, xa, xb, w1, b1, w2, b2)


def _combine_kernel(x_ref, mod_ref, za_ref, zb_ref, wt_ref, g_ref, b_ref, out_ref, *, alpha):
    wt = wt_ref[0]
    f = None
    for kk in range(TOP_K):
        term = _unpack_bf16_pairs(jnp.concatenate([za_ref[kk, 0], zb_ref[kk, 0]], axis=1)) * wt[:, kk:kk + 1]
        f = term if f is None else f + term
    out_ref[0] = _resid_ln(x_ref[0], f, mod_ref[0, 0, 5:6, :], g_ref[...], b_ref[...], alpha)


def _combine(x, modtab, za, zb, wt, ln_g, ln_b, alpha, ctx_len, first_block):
    bsz, t, d = x.shape
    tm = ROW_TILE
    ctx_blocks = ctx_len // tm
    nb = t // tm - first_block
    row_in = lambda i, j: (i, j + first_block, 0)
    row = lambda i, j: (i, j, 0)
    const2 = lambda i, j: (0, 0)
    return pl.pallas_call(
        functools.partial(_combine_kernel, alpha=alpha),
        grid=(bsz, nb),
        in_specs=[pl.BlockSpec((1, tm, d), row_in),
                  pl.BlockSpec((1, 1, 8, d), lambda i, j: (i, _seg_of_block(j + first_block, ctx_blocks), 0, 0)),
                  pl.BlockSpec((TOP_K, 1, tm, SC_ROW_WORDS), lambda i, j: (0, i, j, 0)),
                  pl.BlockSpec((TOP_K, 1, tm, SC_ROW_WORDS), lambda i, j: (0, i, j, 0)),
                  pl.BlockSpec((1, tm, LANES), row),
                  pl.BlockSpec((1, d), const2), pl.BlockSpec((1, d), const2)],
        out_specs=pl.BlockSpec((1, tm, d), row),
        out_shape=jax.ShapeDtypeStruct((bsz, nb * tm, d), F32),
        compiler_params=_cparams("parallel", "parallel"),
    )(x, modtab, za, zb, wt, ln_g, ln_b)


def _moe_layer(x, modtab, w_r, b_r, expert_params, ln_g, ln_b, alpha, ctx_len, latent_only):
    bsz, t, d = x.shape
    first_block = ctx_len // ROW_TILE if latent_only else 0
    wr_pad = jnp.pad(w_r, ((0, 0), (0, LANES - N_EXPERTS)))
    br_pad = jnp.pad(b_r, (0, LANES - N_EXPERTS)).reshape(1, LANES)
    ha, hb, idx, wt, cnt = _router(x, modtab, wr_pad, br_pad, ctx_len, first_block)
    tr = ha.shape[1]
    dq = ha.shape[2]
    n_tok = bsz * tr
    n_assign = n_tok * TOP_K
    tm = EXPERT_TILE
    experts = jnp.arange(N_EXPERTS, dtype=jnp.int32)
    counts = cnt[0, :N_EXPERTS].astype(jnp.int32)
    padded = (counts + tm - 1) // tm * tm
    pends = jnp.cumsum(padded)
    pstarts = pends - padded
    e_tk = idx[:, :, :TOP_K].reshape(n_tok, TOP_K)
    rank = idx[:, :, TOP_K:2 * TOP_K].reshape(n_tok, TOP_K)
    dest = jnp.sum(jnp.where(e_tk[:, :, None] == experts, pstarts, 0), axis=-1) + rank
    n_rows = (-(-n_assign // tm) + N_EXPERTS) * tm
    n_blocks = n_rows // tm
    starts = jnp.arange(n_blocks, dtype=jnp.int32) * tm
    block_e = jnp.minimum(jnp.sum((pends[None, :] <= starts[:, None]).astype(jnp.int32), axis=1), N_EXPERTS - 1)
    n_used = (pends[-1] // tm).astype(jnp.int32).reshape(1)
    dest_t = dest.T
    xa = _sc_scatter_rows(ha.reshape(n_tok, dq), dest_t, n_rows)
    xb = _sc_scatter_rows(hb.reshape(n_tok, dq), dest_t, n_rows)
    ya, yb = _experts(xa, xb, block_e, n_used, *expert_params)
    flat = dest_t.reshape(1, n_assign)
    za = _sc_gather_rows(ya, flat).reshape(TOP_K, bsz, tr, dq)
    zb = _sc_gather_rows(yb, flat).reshape(TOP_K, bsz, tr, dq)
    return _combine(x, modtab, za, zb, wt, ln_g, ln_b, alpha, ctx_len, first_block)


SC_WINDOW = 128
SC_ROW_WORDS = 256


def _sc_mesh():
    return plsc.VectorSubcoreMesh(core_axis_name="c", subcore_axis_name="s")


def _sc_scatter_rows(src, idx, n_out):
    n_src, width = src.shape
    n_k = idx.shape[0]
    per_k = n_src // SC_WINDOW

    @functools.partial(pl.kernel, out_type=jax.ShapeDtypeStruct((n_out, width), src.dtype), mesh=_sc_mesh())
    def scatter(x_hbm, i_hbm, o_hbm):
        def body(x_vmem, i_vmem):
            pltpu.sync_copy(x_vmem, o_hbm.at[i_vmem.at[0]])

        pltpu.emit_pipeline(
            body, grid=(n_k * per_k,),
            in_specs=[pl.BlockSpec((SC_WINDOW, width), lambda i: (i % per_k, 0)),
                      pl.BlockSpec((1, SC_WINDOW), lambda i: (i // per_k, i % per_k))],
            out_specs=[],
            core_axis_name=("c", "s"),
            dimension_semantics=(pltpu.PARALLEL,),
        )(x_hbm, i_hbm)

    return scatter(src, idx)


def _sc_gather_rows(table, idx):
    width = table.shape[1]
    n_idx = idx.shape[1]

    @functools.partial(pl.kernel, out_type=jax.ShapeDtypeStruct((n_idx, width), table.dtype), mesh=_sc_mesh())
    def gather(x_hbm, i_hbm, o_hbm):
        def body(i_vmem, o_vmem):
            pltpu.sync_copy(x_hbm.at[i_vmem.at[0]], o_vmem)

        pltpu.emit_pipeline(
            body, grid=(n_idx // SC_WINDOW,),
            in_specs=[pl.BlockSpec((1, SC_WINDOW), lambda i: (0, i))],
            out_specs=[pl.BlockSpec((SC_WINDOW, width), lambda i: (i, 0))],
            core_axis_name=("c", "s"),
            dimension_semantics=(pltpu.PARALLEL,),
        )(i_hbm, o_hbm)

    return gather(table, idx)


def _rot_cols(w, head_dim):
    lead = w.shape[:-1]
    q = head_dim // 4
    wr = w.reshape(lead + (-1, 4, q))
    out = jnp.stack([-wr[..., 1, :], wr[..., 0, :], -wr[..., 3, :], wr[..., 2, :]], axis=-2)
    return out.reshape(w.shape)


def _dup_heads(w, head_dim):
    lead = w.shape[:-1]
    wr = w.reshape(lead + (-1, 1, head_dim))
    return jnp.concatenate([wr, wr], axis=-2).reshape(lead + (-1,))


def _rope_tables(n_lat, ctx_len, head_dim):
    rows = n_lat // GRID_W
    row = jnp.repeat(jnp.arange(rows, dtype=F32), GRID_W)
    col = jnp.tile(jnp.arange(GRID_W, dtype=F32), rows)
    half = head_dim // 2
    inv = ROPE_BASE ** (-jnp.arange(0, half, 2, dtype=F32) / half)
    ar = row[:, None] * inv
    ac = col[:, None] * inv
    ang = jnp.concatenate([ar, ar, ac, ac], axis=-1)
    cos = jnp.concatenate([jnp.ones((ctx_len, head_dim), F32), jnp.cos(ang)], axis=0)
    sin = jnp.concatenate([jnp.zeros((ctx_len, head_dim), F32), jnp.sin(ang)], axis=0)
    reps = LANES // head_dim
    return jnp.tile(cos, (1, reps)), jnp.tile(sin, (1, reps))


def _conv_mixer(x, modtab, w_in, w_conv, w_out, ln_g, ln_b, alpha, ctx_len):
    d = x.shape[2]
    plan = [("plain", 0, 0, d, 0, None, 1.0), ("mul", 1, 0, d, d, 2 * d, 1.0)]
    bg, p = _project(x, modtab, w_in.astype(BF16), jnp.zeros((1, 3 * d), F32), plan, (d, d), (BF16, BF16), ctx_len)
    cw = jnp.pad(w_conv, ((0, 8 - SC_WIDTH), (0, 0)))
    return _mixer_out("conv", x, modtab, (bg, p), w_out.astype(BF16), jnp.zeros((1, d), F32), ln_g, ln_b, alpha,
                      ctx_len, extra=cw)


def _swa_mixer(x, modtab, w_qkv, b_qkv, sink, w_o, b_o, ln_g, ln_b, alpha, ctx_len, rope):
    d = x.shape[2]
    nq = SWA_HEADS * SWA_HEAD_DIM
    nkv = SWA_KV_HEADS * SWA_HEAD_DIM
    hd = SWA_HEAD_DIM

    def arrange(a):
        q, k, v = a[..., :nq], a[..., nq:nq + nkv], a[..., nq + nkv:]
        kk = _dup_heads(k, hd)
        return jnp.concatenate([q, _rot_cols(q, hd), kk, _rot_cols(kk, hd), _dup_heads(v, hd)], axis=-1)

    w = arrange(w_qkv).astype(BF16)
    b = arrange(b_qkv.reshape(1, -1))
    plan = [("rope", 0, 0, nq, 0, nq, SWA_HEAD_DIM ** -0.5),
            ("rope", 1, 0, 2 * nkv, 2 * nq, 2 * nq + 2 * nkv, 1.0),
            ("plain", 2, 0, 2 * nkv, 2 * nq + 4 * nkv, None, 1.0)]
    q, kk, vv = _project(x, modtab, w, b, plan, (nq, 2 * nkv, 2 * nkv), (BF16, BF16, BF16), ctx_len, rope=rope)
    o = _swa_attention(q, kk, vv, sink.astype(F32), ctx_len)
    return _mixer_out("plain", x, modtab, (o,), w_o.astype(BF16), b_o.reshape(1, d), ln_g, ln_b, alpha, ctx_len)


def _diff_mixer(x, modtab, w_qkv, lam_p, subln_g, w_o, lam_init, ln_g, ln_b, alpha, ctx_len, rope):
    d = x.shape[2]
    hd = DIFF_HEAD_DIM
    wq, wk, wv = w_qkv[:, :d], w_qkv[:, d:2 * d], w_qkv[:, 2 * d:]
    w = jnp.concatenate([wq, _rot_cols(wq, hd), wk, _rot_cols(wk, hd), wv], axis=-1).astype(BF16)
    plan = [("rope", 0, 0, d, 0, d, DIFF_HEAD_DIM ** -0.5),
            ("rope", 1, 0, d, 2 * d, 3 * d, 1.0),
            ("plain", 2, 0, d, 4 * d, None, 1.0)]
    q, k, v = _project(x, modtab, w, jnp.zeros((1, 5 * d), F32), plan, (d, d, d), (BF16, BF16, BF16), ctx_len,
                       rope=rope)
    o = _diff_attention(q, k, v, lam_p.astype(F32), subln_g.astype(F32), ctx_len, lam_init)
    return _mixer_out("plain", x, modtab, (o,), w_o.astype(BF16), jnp.zeros((1, d), F32), ln_g, ln_b, alpha, ctx_len)


def _delta_mixer(x, modtab, w_qkvz, w_ba, a_log, dt_bias, w_conv, norm_g, w_o, ln_g, ln_b, alpha, ctx_len):
    bsz, t, d = x.shape
    nqk = DN_QK_HEADS * DN_HEAD_DIM
    nv = DN_V_HEADS * DN_HEAD_DIM
    nc = 2 * nqk + nv
    nba = w_ba.shape[1]
    w = jnp.concatenate([w_qkvz, jnp.pad(w_ba, ((0, 0), (0, LANES - nba)))], axis=-1).astype(BF16)
    ntot = w.shape[1]
    plan = [("plain", 0, 0, nc, 0, None, 1.0), ("plain", 1, 0, nv, nc, None, 1.0),
            ("plain", 2, 0, LANES, nc + nv, None, 1.0)]
    qkv_pre, z, ba = _project(x, modtab, w, jnp.zeros((1, ntot), F32), plan, (nc, nv, LANES), (BF16, BF16, F32),
                              ctx_len)
    zeros16 = jnp.zeros((DN_V_HEADS,), F32)
    lanes_of = lambda p: jnp.pad(jnp.concatenate([zeros16, p[0], zeros16, p[1]]), (0, LANES - 4 * DN_V_HEADS))
    alog_l = lanes_of(a_log.astype(F32)).reshape(1, LANES)
    dtb_l = lanes_of(dt_bias.astype(F32)).reshape(1, LANES)
    cw = jnp.pad(w_conv, ((0, 8 - DN_CONV), (0, 0)))
    q, k, v, gb = _delta_prep(qkv_pre, cw, ba, alog_l, dtb_l, ctx_len, nqk, nv)
    rep = DN_V_HEADS // DN_QK_HEADS
    gcum = jnp.stack([gb[:, :, DN_V_HEADS:2 * DN_V_HEADS], gb[:, :, 3 * DN_V_HEADS:4 * DN_V_HEADS]])
    gt = jnp.transpose(gcum, (0, 1, 3, 2))
    gt_a = jnp.pad(gt.reshape(2, bsz, DN_QK_HEADS, rep, t), ((0, 0), (0, 0), (0, 0), (0, 8 - rep), (0, 0)))
    gt_b = gt.reshape(2, bsz, DN_V_HEADS // DN_SCAN_HEADS, DN_SCAN_HEADS, t)
    u, w_, qg, kd, a = _delta_chunks(q, k, v, gb, gt_a)
    o2 = _delta_scan(u, w_, qg, kd, a, gt_b, ctx_len)
    return _mixer_out("delta", x, modtab, (o2[0], o2[1], z), w_o.astype(BF16), jnp.zeros((1, d), F32),
                      ln_g, ln_b, alpha, ctx_len, extra=norm_g.astype(F32).reshape(1, DN_HEAD_DIM))


def kernel(x, c, ctx, c_ctx, mod_w, mod_b, ln1_g, ln1_b, ln2_g, ln2_b, router_w, router_b, exp_w1, exp_b1, exp_w2, exp_b2, conv_in_w, conv_w, conv_out_w, swa_qkv_w, swa_qkv_b, swa_sink, swa_out_w, swa_out_b, diff_qkv_w, diff_lambda, diff_subln_g, diff_out_w, delta_qkvz_w, delta_ba_w, delta_a_log, delta_dt_bias, delta_conv_w, delta_norm_g, delta_out_w):
    bsz, n_lat, d = x.shape
    ctx_len = ctx.shape[1]
    depth = mod_w.shape[0]
    alpha = (2 * depth) ** 0.25
    xs = jnp.concatenate([ctx, x], axis=1)
    cc = jnp.zeros((16, d), F32).at[:bsz].set(c).at[bsz].set(c_ctx)
    mods = _mod_vectors(cc, mod_w, mod_b).reshape(depth, 16, 6, d)
    mod_lat = mods[:, :bsz]
    mod_ctx = jnp.broadcast_to(mods[:, bsz:bsz + 1], mod_lat.shape)
    modtabs = jnp.pad(jnp.stack([mod_ctx, mod_lat], axis=2), ((0, 0), (0, 0), (0, 0), (0, 2), (0, 0)))
    rope = _rope_tables(n_lat, ctx_len, SWA_HEAD_DIM)
    row1 = lambda a: a.reshape(1, d)
    for i in range(depth):
        last = i == depth - 1
        kind, j = i % N_MIXERS, i // N_MIXERS
        mt = modtabs[i]
        g1, b1 = row1(ln1_g[i]), row1(ln1_b[i])
        if kind == 0:
            xs = _conv_mixer(xs, mt, conv_in_w[j], conv_w[j], conv_out_w[j], g1, b1, alpha, ctx_len)
        elif kind == 1:
            xs = _swa_mixer(xs, mt, swa_qkv_w[j], swa_qkv_b[j], swa_sink[j], swa_out_w[j], swa_out_b[j], g1, b1,
                            alpha, ctx_len, rope)
        elif kind == 2:
            lam_init = 0.8 - 0.6 * math.exp(-0.3 * i)
            xs = _diff_mixer(xs, mt, diff_qkv_w[j], diff_lambda[j], diff_subln_g[j], diff_out_w[j], lam_init,
                             g1, b1, alpha, ctx_len, rope)
        else:
            xs = _delta_mixer(xs, mt, delta_qkvz_w[j], delta_ba_w[j], delta_a_log[j], delta_dt_bias[j],
                              delta_conv_w[j], delta_norm_g[j], delta_out_w[j], g1, b1, alpha, ctx_len)
        xs = _moe_layer(xs, mt, router_w[i], router_b[i], (exp_w1, exp_b1, exp_w2, exp_b2, i),
                        row1(ln2_g[i]), row1(ln2_b[i]), alpha, ctx_len, latent_only=last)
    return xs
```

```python
import functools
import math

import jax
import jax.numpy as jnp
from jax import lax
from jax.experimental import pallas as pl
from jax.experimental.pallas import tpu as pltpu
from jax.experimental.pallas import tpu_sc as plsc

F32 = jnp.float32
BF16 = jnp.bfloat16

GRID_W = 64
N_MIXERS = 4
LN_EPS = 1e-5
RMS_EPS = 1e-6
NEG_INF = -1e30
ROPE_BASE = 10000.0
SC_WIDTH = 3
SWA_HEADS = 16
SWA_KV_HEADS = 4
SWA_HEAD_DIM = 64
SWA_WINDOW = 128
DIFF_HEADS = 8
DIFF_HEAD_DIM = 64
DN_QK_HEADS = 8
DN_V_HEADS = 16
DN_HEAD_DIM = 128
DN_CONV = 5
N_EXPERTS = 32
TOP_K = 4
SWIGLU_LIMIT = 7.0
SWIGLU_ALPHA = 1.702

LANES = 128
HALO_ROWS = 16
VMEM_LIMIT = 56 * 1024 * 1024

ROW_TILE = 256
ATT_BLOCK = 128
DIFF_Q_BLOCK = 256
DN_CHUNK = 128
EXPERT_TILE = 512
COL_CHUNK = 512


def _cparams(*sem):
    return pltpu.CompilerParams(dimension_semantics=sem, vmem_limit_bytes=VMEM_LIMIT)


def _split3(x):
    hi = x.astype(BF16)
    r1 = x - hi.astype(F32)
    mid = r1.astype(BF16)
    lo = (r1 - mid.astype(F32)).astype(BF16)
    return hi, mid, lo


def _dot(a, b):
    return jnp.dot(a, b, preferred_element_type=F32)


def _dot_nt(a, b):
    return lax.dot_general(a, b, (((1,), (1,)), ((), ())), preferred_element_type=F32)


def _dot_tn(a, b):
    return lax.dot_general(a, b, (((0,), (0,)), ((), ())), preferred_element_type=F32)


def _silu(x):
    return x * jax.nn.sigmoid(x)


def _seg_of_block(j, ctx_blocks):
    return jnp.where(j < ctx_blocks, 0, 1)


def _mods_kernel(c_ref, w_ref, b_ref, o_ref):
    s = _silu(c_ref[...]).astype(BF16)
    o_ref[0] = _dot(s, w_ref[0].astype(BF16)) + b_ref[0]


def _mod_vectors(cc, mod_w, mod_b):
    depth, d, n = mod_w.shape
    rows = cc.shape[0]
    tn = 1536
    return pl.pallas_call(
        _mods_kernel,
        grid=(depth, n // tn),
        in_specs=[pl.BlockSpec((rows, d), lambda l, j: (0, 0)),
                  pl.BlockSpec((1, d, tn), lambda l, j: (l, 0, j)),
                  pl.BlockSpec((1, 1, tn), lambda l, j: (l, 0, j))],
        out_specs=pl.BlockSpec((1, rows, tn), lambda l, j: (l, 0, j)),
        out_shape=jax.ShapeDtypeStruct((depth, rows, n), F32),
        compiler_params=_cparams("parallel", "parallel"),
    )(cc, mod_w, mod_b.reshape(depth, 1, n))


def _proj_kernel(*refs, plan, n_out, use_rope):
    x_ref, mod_ref, w_ref, b_ref = refs[:4]
    pos = 4
    if use_rope:
        cos_ref, sin_ref = refs[4:6]
        pos = 6
    outs = refs[pos:pos + n_out]
    shift = mod_ref[0, 0, 0:1, :]
    scale = mod_ref[0, 0, 1:2, :]
    h = (x_ref[0] * (1.0 + scale) + shift).astype(BF16)

    def acc(col, width):
        return _dot(h, w_ref[:, col:col + width]) + b_ref[:, col:col + width]

    for kind, oi, ocol, width, wcol, wcol2, mult in plan:
        if kind == "ones":
            outs[oi][0, :, ocol:ocol + width] = jnp.ones((x_ref.shape[1], width), outs[oi].dtype)
            continue
        for c0 in range(0, width, COL_CHUNK):
            cw = min(COL_CHUNK, width - c0)
            a = acc(wcol + c0, cw)
            if kind == "rope":
                reps = cw // LANES
                cos = jnp.tile(cos_ref[...], (1, reps))
                sin = jnp.tile(sin_ref[...], (1, reps))
                a = (a * cos + acc(wcol2 + c0, cw) * sin) * mult
            elif kind == "mul":
                a = a * acc(wcol2 + c0, cw)
            outs[oi][0, :, ocol + c0:ocol + c0 + cw] = a.astype(outs[oi].dtype)


def _project(x, modtab, w, b, plan, out_widths, out_dtypes, ctx_len, rope=None):
    bsz, t, d = x.shape
    n = w.shape[1]
    tm = ROW_TILE
    ctx_blocks = ctx_len // tm
    mt = modtab
    in_specs = [pl.BlockSpec((1, tm, d), lambda i, j: (i, j, 0)),
                pl.BlockSpec((1, 1, 8, d), lambda i, j: (i, _seg_of_block(j, ctx_blocks), 0, 0)),
                pl.BlockSpec((d, n), lambda i, j: (0, 0)),
                pl.BlockSpec((1, n), lambda i, j: (0, 0))]
    args = [x, mt, w, b]
    if rope is not None:
        in_specs += [pl.BlockSpec((tm, LANES), lambda i, j: (j, 0))] * 2
        args += list(rope)
    out_specs = [pl.BlockSpec((1, tm, ow), lambda i, j: (i, j, 0)) for ow in out_widths]
    out_shape = [jax.ShapeDtypeStruct((bsz, t, ow), dt) for ow, dt in zip(out_widths, out_dtypes)]
    return pl.pallas_call(
        functools.partial(_proj_kernel, plan=tuple(plan), n_out=len(out_widths), use_rope=rope is not None),
        grid=(bsz, t // tm),
        in_specs=in_specs, out_specs=out_specs, out_shape=out_shape,
        compiler_params=_cparams("parallel", "parallel"),
    )(*args)


def _resid_ln(x, y, gate, g, b, alpha):
    r = alpha * x + gate * y
    mu = jnp.mean(r, axis=-1, keepdims=True)
    rc = r - mu
    var = jnp.mean(rc * rc, axis=-1, keepdims=True)
    return rc * lax.rsqrt(var + LN_EPS) * g + b


def _centred_conv(ext_ref, p, prev, nxt, w_ref, width, has_prev, has_next):
    tm = p.shape[0]
    h = HALO_ROWS
    pad = (width - 1) // 2
    ext_ref[0:h, :] = jnp.where(has_prev, prev, 0.0)
    ext_ref[h:h + tm, :] = p
    ext_ref[h + tm:h + tm + h, :] = jnp.where(has_next, nxt, 0.0)
    acc = None
    for k in range(width):
        term = ext_ref[h - pad + k:h - pad + k + tm, :] * w_ref[k:k + 1, :]
        acc = term if acc is None else acc + term
    return acc


def _seg_edges(j, ctx_blocks, n_blocks):
    has_prev = jnp.logical_and(j != 0, j != ctx_blocks)
    has_next = jnp.logical_and(j != ctx_blocks - 1, j != n_blocks - 1)
    return has_prev, has_next


def _out_plain_kernel(x_ref, mod_ref, o_ref, w_ref, b_ref, g_ref, bb_ref, out_ref, *, alpha):
    y = _dot(o_ref[0], w_ref[...]) + b_ref[...]
    out_ref[0] = _resid_ln(x_ref[0], y, mod_ref[0, 0, 2:3, :], g_ref[...], bb_ref[...], alpha)


def _out_conv_kernel(x_ref, mod_ref, bg_ref, p_ref, pp_ref, pn_ref, cw_ref, w_ref, b_ref, g_ref, bb_ref,
                     out_ref, ext_ref, *, alpha, ctx_blocks, n_blocks):
    j = pl.program_id(1)
    has_prev, has_next = _seg_edges(j, ctx_blocks, n_blocks)
    conv = _centred_conv(ext_ref, p_ref[0].astype(F32), pp_ref[0].astype(F32), pn_ref[0].astype(F32), cw_ref,
                         SC_WIDTH, has_prev, has_next)
    o = (bg_ref[0].astype(F32) * conv).astype(BF16)
    y = _dot(o, w_ref[...]) + b_ref[...]
    out_ref[0] = _resid_ln(x_ref[0], y, mod_ref[0, 0, 2:3, :], g_ref[...], bb_ref[...], alpha)


def _out_delta_kernel(x_ref, mod_ref, of_ref, ob_ref, z_ref, ng_ref, w_ref, b_ref, g_ref, bb_ref,
                      out_ref, *, alpha):
    hd = DN_HEAD_DIM
    ng = ng_ref[...]
    acc = None
    for h0 in range(0, of_ref.shape[2], hd):
        o = of_ref[0, :, h0:h0 + hd].astype(F32) + ob_ref[0, :, h0:h0 + hd].astype(F32)
        o = o * lax.rsqrt(jnp.mean(o * o, axis=-1, keepdims=True) + RMS_EPS) * ng
        o = (o * _silu(z_ref[0, :, h0:h0 + hd].astype(F32))).astype(BF16)
        part = _dot(o, w_ref[h0:h0 + hd, :])
        acc = part if acc is None else acc + part
    y = acc + b_ref[...]
    out_ref[0] = _resid_ln(x_ref[0], y, mod_ref[0, 0, 2:3, :], g_ref[...], bb_ref[...], alpha)


def _mixer_out(kind, x, modtab, acts, w_o, b_o, ln_g, ln_b, alpha, ctx_len, extra=None):
    bsz, t, d = x.shape
    tm = ROW_TILE
    nb = t // tm
    ctx_blocks = ctx_len // tm
    kin = w_o.shape[0]
    row = lambda i, j: (i, j, 0)
    const2 = lambda i, j: (0, 0)
    x_spec = pl.BlockSpec((1, tm, d), row)
    mod_spec = pl.BlockSpec((1, 1, 8, d), lambda i, j: (i, _seg_of_block(j, ctx_blocks), 0, 0))
    tail_specs = [pl.BlockSpec((kin, d), const2), pl.BlockSpec((1, d), const2),
                  pl.BlockSpec((1, d), const2), pl.BlockSpec((1, d), const2)]
    tail_args = [w_o, b_o, ln_g, ln_b]
    scratch = []
    if kind == "plain":
        body = functools.partial(_out_plain_kernel, alpha=alpha)
        in_specs = [x_spec, mod_spec, pl.BlockSpec((1, tm, kin), row)] + tail_specs
        args = [x, modtab, acts[0]] + tail_args
    elif kind == "conv":
        body = functools.partial(_out_conv_kernel, alpha=alpha, ctx_blocks=ctx_blocks, n_blocks=nb)
        hb = tm // HALO_ROWS
        last_halo = t // HALO_ROWS - 1
        in_specs = [x_spec, mod_spec, pl.BlockSpec((1, tm, d), row), pl.BlockSpec((1, tm, d), row),
                    pl.BlockSpec((1, HALO_ROWS, d), lambda i, j: (i, jnp.maximum(j * hb - 1, 0), 0)),
                    pl.BlockSpec((1, HALO_ROWS, d), lambda i, j: (i, jnp.minimum((j + 1) * hb, last_halo), 0)),
                    pl.BlockSpec((8, d), const2)] + tail_specs
        args = [x, modtab, acts[0], acts[1], acts[1], acts[1], extra] + tail_args
        scratch = [pltpu.VMEM((tm + 2 * HALO_ROWS, d), F32)]
    else:
        body = functools.partial(_out_delta_kernel, alpha=alpha)
        in_specs = [x_spec, mod_spec, pl.BlockSpec((1, tm, kin), row), pl.BlockSpec((1, tm, kin), row),
                    pl.BlockSpec((1, tm, kin), row), pl.BlockSpec((1, DN_HEAD_DIM), const2)] + tail_specs
        args = [x, modtab, acts[0], acts[1], acts[2], extra] + tail_args
    return pl.pallas_call(
        body, grid=(bsz, nb), in_specs=in_specs,
        out_specs=pl.BlockSpec((1, tm, d), row),
        out_shape=jax.ShapeDtypeStruct((bsz, t, d), F32),
        scratch_shapes=scratch,
        compiler_params=_cparams("parallel", "parallel"),
    )(*args)


def _swa_kernel(sink_ref, q_ref, kp_ref, ko_ref, kn_ref, kc_ref, vp_ref, vo_ref, vn_ref, vc_ref, o_ref,
                *, n_lat_blocks, ctx_blocks):
    j = pl.program_id(1)
    blk = ATT_BLOCK
    lat = j - ctx_blocks
    is_lat = j >= ctx_blocks
    n_keys = 3 * blk + kc_ref.shape[1]
    r = lax.broadcasted_iota(jnp.int32, (blk, n_keys), 0)
    c = lax.broadcasted_iota(jnp.int32, (blk, n_keys), 1)
    ok_prev = jnp.logical_and(is_lat, lat > 0)
    ok_next = jnp.logical_and(is_lat, lat < n_lat_blocks - 1)
    r_max = jnp.where(c < blk, jnp.where(ok_prev, c, -1),
                      jnp.where(c < 2 * blk, jnp.where(is_lat, blk, -1),
                                jnp.where(c < 3 * blk, jnp.where(ok_next, blk, -1), blk)))
    r_min = jnp.where(c < 2 * blk, 0, jnp.where(c < 3 * blk, c - 2 * blk, 0))
    valid = jnp.logical_and(r >= r_min, r <= r_max)
    lane = lax.broadcasted_iota(jnp.int32, (blk, LANES), 1)
    lo = lane < SWA_HEAD_DIM
    group = SWA_HEADS // SWA_KV_HEADS
    for g in range(SWA_KV_HEADS):
        gs = slice(g * LANES, (g + 1) * LANES)
        kcat = jnp.concatenate([kp_ref[0, :, gs], ko_ref[0, :, gs], kn_ref[0, :, gs], kc_ref[0, :, gs]], axis=0)
        vcat = jnp.concatenate([vp_ref[0, :, gs], vo_ref[0, :, gs], vn_ref[0, :, gs], vc_ref[0, :, gs]], axis=0)
        pairs = range(g * group // 2, (g + 1) * group // 2)
        qhs, sinks = [], []
        for pair in pairs:
            qp = q_ref[0, :, pair * LANES:(pair + 1) * LANES]
            zero = jnp.zeros_like(qp)
            qhs += [jnp.where(lo, qp, zero), jnp.where(lo, zero, qp)]
            sinks += [sink_ref[2 * pair], sink_ref[2 * pair + 1]]
        ss = [jnp.where(valid, _dot_nt(qh, kcat), NEG_INF) for qh in qhs]
        ms = [jnp.maximum(jnp.max(s, axis=-1, keepdims=True), sink) for s, sink in zip(ss, sinks)]
        es = [jnp.exp(s - m) for s, m in zip(ss, ms)]
        dens = [jnp.sum(e, axis=-1, keepdims=True) + jnp.exp(sink - m) for e, m, sink in zip(es, ms, sinks)]
        ps = [(e * (1.0 / den)).astype(BF16) for e, den in zip(es, dens)]
        accs = [_dot(p, vcat) for p in ps]
        for n, pair in enumerate(pairs):
            o_ref[0, :, pair * LANES:(pair + 1) * LANES] = jnp.where(lo, accs[2 * n], accs[2 * n + 1]).astype(o_ref.dtype)


def _swa_attention(q, kk, vv, sink, ctx_len):
    bsz, t, dq = q.shape
    dk = kk.shape[2]
    blk = ATT_BLOCK
    nb = t // blk
    ctx_blocks = ctx_len // blk
    n_lat_blocks = nb - ctx_blocks

    def prev_map(i, j, s):
        return (i, jnp.clip(j - 1, ctx_blocks, nb - 1), 0)

    def own_map(i, j, s):
        return (i, j, 0)

    def next_map(i, j, s):
        return (i, jnp.clip(j + 1, ctx_blocks, nb - 1), 0)

    def ctx_map(i, j, s):
        return (i, 0, 0)

    kv_specs = [pl.BlockSpec((1, blk, dk), prev_map), pl.BlockSpec((1, blk, dk), own_map),
                pl.BlockSpec((1, blk, dk), next_map), pl.BlockSpec((1, ctx_len, dk), ctx_map)]
    return pl.pallas_call(
        functools.partial(_swa_kernel, n_lat_blocks=n_lat_blocks, ctx_blocks=ctx_blocks),
        grid_spec=pltpu.PrefetchScalarGridSpec(
            num_scalar_prefetch=1, grid=(bsz, nb),
            in_specs=[pl.BlockSpec((1, blk, dq), own_map)] + kv_specs + kv_specs,
            out_specs=pl.BlockSpec((1, blk, dq), own_map)),
        out_shape=jax.ShapeDtypeStruct((bsz, t, dq), BF16),
        compiler_params=_cparams("parallel", "parallel"),
    )(sink, q, kk, kk, kk, kk, vv, vv, vv, vv)


def _diff_kernel(lam_ref, q_ref, k_ref, v_ref, g_ref, o_ref, *, ctx_len, lam_init):
    j = pl.program_id(2)
    blk = DIFF_Q_BLOCK
    ctx_blocks = ctx_len // blk
    lp = lam_ref[...]
    lam = (jnp.exp(jnp.sum(lp[0:1, :] * lp[1:2, :], axis=-1, keepdims=True))
           - jnp.exp(jnp.sum(lp[2:3, :] * lp[3:4, :], axis=-1, keepdims=True)) + lam_init)
    lane = lax.broadcasted_iota(jnp.int32, (blk, LANES), 1)
    lo = lane < DIFF_HEAD_DIM
    qp = q_ref[0]
    zero = jnp.zeros_like(qp)

    def attend(n_keys):
        k = k_ref[0, 0:n_keys, :]
        v = v_ref[0, 0:n_keys, :]
        qhs = [jnp.where(lo, qp, zero), jnp.where(lo, zero, qp)]
        ss = [_dot_nt(qh, k) for qh in qhs]
        es = [jnp.exp2(s - jnp.max(s, axis=-1, keepdims=True)) for s in ss]
        pvs = [_dot(e.astype(BF16), v) for e in es]
        invs = [1.0 / pv[:, LANES:LANES + 1] for pv in pvs]
        o = pvs[0][:, :LANES] * invs[0] - (lam * invs[1]) * pvs[1][:, :LANES]
        o = o * lax.rsqrt(jnp.mean(o * o, axis=-1, keepdims=True) + RMS_EPS) * g_ref[...]
        o_ref[0] = (o * (1.0 - lam_init)).astype(o_ref.dtype)

    @pl.when(j < ctx_blocks)
    def _():
        attend(ctx_len)

    @pl.when(j >= ctx_blocks)
    def _():
        attend(k_ref.shape[1])


def _diff_attention(q, k, v, lam_p, subln_g, ctx_len, lam_init):
    bsz, t, dq = q.shape
    blk = DIFF_Q_BLOCK
    nh = dq // LANES
    qmap = lambda i, h, j: (i, j, h)
    kmap = lambda i, h, j: (i, 0, h)
    const2 = lambda i, h, j: (0, 0)
    return pl.pallas_call(
        functools.partial(_diff_kernel, ctx_len=ctx_len, lam_init=lam_init),
        grid=(bsz, nh, t // blk),
        in_specs=[pl.BlockSpec((8, DIFF_HEAD_DIM), const2),
                  pl.BlockSpec((1, blk, LANES), qmap),
                  pl.BlockSpec((1, t, LANES), kmap),
                  pl.BlockSpec((1, t, 2 * LANES), kmap),
                  pl.BlockSpec((1, LANES), const2)],
        out_specs=pl.BlockSpec((1, blk, LANES), qmap),
        out_shape=jax.ShapeDtypeStruct((bsz, t, dq), BF16),
        compiler_params=_cparams("parallel", "parallel", "parallel"),
    )(jnp.pad(lam_p, ((0, 4), (0, 0))), q, k, v, subln_g.reshape(1, LANES))


def _delta_prep_kernel(x_ref, xp_ref, xn_ref, cw_ref, ba_ref, alog_ref, dtb_ref,
                       q_ref, k_ref, v_ref, gb_ref, ext_ref, *, ctx_blocks, n_blocks, nqk):
    j = pl.program_id(1)
    has_prev, has_next = _seg_edges(j, ctx_blocks, n_blocks)
    hd = DN_HEAD_DIM
    tm = x_ref.shape[1]
    for c0 in range(0, x_ref.shape[2], COL_CHUNK):
        cs = slice(c0, c0 + COL_CHUNK)
        conv = _centred_conv(ext_ref, x_ref[0, :, cs].astype(F32), xp_ref[0, :, cs].astype(F32),
                             xn_ref[0, :, cs].astype(F32), cw_ref.at[:, cs], DN_CONV, has_prev, has_next)
        a = _silu(conv)
        for h0 in range(0, COL_CHUNK, hd):
            col = c0 + h0
            ah = a[:, h0:h0 + hd]
            if col < 2 * nqk:
                ah = ah * lax.rsqrt(jnp.sum(ah * ah, axis=-1, keepdims=True) + RMS_EPS)
                if col < nqk:
                    q_ref[0, :, col:col + hd] = (ah * (hd ** -0.5)).astype(q_ref.dtype)
                else:
                    k_ref[0, :, col - nqk:col - nqk + hd] = ah.astype(k_ref.dtype)
            else:
                v_ref[0, :, col - 2 * nqk:col - 2 * nqk + hd] = ah.astype(v_ref.dtype)
    ba = ba_ref[0]
    lane = lax.broadcasted_iota(jnp.int32, ba.shape, 1)
    is_beta = (lane % 32) < DN_V_HEADS
    z = ba + dtb_ref[...]
    softplus = jnp.maximum(z, 0.0) + jnp.log(1.0 + jnp.exp(-jnp.abs(z)))
    g = -jnp.exp(alog_ref[...]) * softplus
    r = lax.broadcasted_iota(jnp.int32, (tm, tm), 0)
    c = lax.broadcasted_iota(jnp.int32, (tm, tm), 1)
    same = (r // DN_CHUNK) == (c // DN_CHUNK)
    tri_f = jnp.where(jnp.logical_and(same, c <= r), 1.0, 0.0).astype(BF16)
    tri_b = jnp.where(jnp.logical_and(same, c >= r), 1.0, 0.0).astype(BF16)
    parts = _split3(g)
    cum_f = _dot(tri_f, parts[0]) + _dot(tri_f, parts[1]) + _dot(tri_f, parts[2])
    cum_b = _dot(tri_b, parts[0]) + _dot(tri_b, parts[1]) + _dot(tri_b, parts[2])
    gcum = jnp.where(lane < 32, cum_f, cum_b)
    gb_ref[0] = jnp.where(is_beta, jax.nn.sigmoid(ba), gcum)


def _delta_prep(qkv_pre, conv_w, ba, alog_l, dtb_l, ctx_len, nqk, nv):
    bsz, t, nc = qkv_pre.shape
    tm = ROW_TILE
    nb = t // tm
    ctx_blocks = ctx_len // tm
    hb = tm // HALO_ROWS
    last_halo = t // HALO_ROWS - 1
    row = lambda i, j: (i, j, 0)
    const2 = lambda i, j: (0, 0)
    return pl.pallas_call(
        functools.partial(_delta_prep_kernel, ctx_blocks=ctx_blocks, n_blocks=nb, nqk=nqk),
        grid=(bsz, nb),
        in_specs=[pl.BlockSpec((1, tm, nc), row),
                  pl.BlockSpec((1, HALO_ROWS, nc), lambda i, j: (i, jnp.maximum(j * hb - 1, 0), 0)),
                  pl.BlockSpec((1, HALO_ROWS, nc), lambda i, j: (i, jnp.minimum((j + 1) * hb, last_halo), 0)),
                  pl.BlockSpec((8, nc), const2),
                  pl.BlockSpec((1, tm, LANES), row),
                  pl.BlockSpec((1, LANES), const2),
                  pl.BlockSpec((1, LANES), const2)],
        out_specs=[pl.BlockSpec((1, tm, nqk), row), pl.BlockSpec((1, tm, nqk), row),
                   pl.BlockSpec((1, tm, nv), row), pl.BlockSpec((1, tm, LANES), row)],
        out_shape=[jax.ShapeDtypeStruct((bsz, t, nqk), BF16), jax.ShapeDtypeStruct((bsz, t, nqk), BF16),
                   jax.ShapeDtypeStruct((bsz, t, nv), BF16), jax.ShapeDtypeStruct((bsz, t, LANES), F32)],
        scratch_shapes=[pltpu.VMEM((tm + 2 * HALO_ROWS, COL_CHUNK), F32)],
        compiler_params=_cparams("parallel", "parallel"),
    )(qkv_pre, qkv_pre, qkv_pre, conv_w, ba, alog_l, dtb_l)


DN_INV_BLOCK = 16


def _unit_triangular_inverses(lows, eye, r, c):
    n = lows[0].shape[0]
    b = DN_INV_BLOCK
    same = (r // b) == (c // b)
    pws = [jnp.where(same, low, 0.0) for low in lows]
    xs = [eye - pw for pw in pws]
    for _ in range(int(math.log2(b)) - 1):
        pbs = [pw.astype(BF16) for pw in pws]
        pws = [_dot(pb, pb) for pb in pbs]
        xs = [x + _dot(x.astype(BF16), pw.astype(BF16)) for x, pw in zip(xs, pws)]
    while b < n:
        pair = jnp.logical_and((r // (2 * b)) == (c // (2 * b)), (r // b) != (c // b))
        xbs = [x.astype(BF16) for x in xs]
        ys = [_dot(jnp.where(pair, low, 0.0).astype(BF16), xb).astype(BF16) for low, xb in zip(lows, xbs)]
        xs = [x - _dot(xb, y) for x, xb, y in zip(xs, xbs, ys)]
        b *= 2
    return xs


DN_SCAN_HEADS = 16


def _delta_chunk_kernel(q_ref, k_ref, v_ref, gb_ref, gt_ref, u_ref, w_ref, qg_ref, kd_ref, a_ref, *, rep):
    hq = pl.program_id(1)
    ck = DN_CHUNK
    hd = DN_HEAD_DIM
    r = lax.broadcasted_iota(jnp.int32, (ck, ck), 0)
    c = lax.broadcasted_iota(jnp.int32, (ck, ck), 1)
    eye = jnp.where(r == c, 1.0, 0.0)
    lane = lax.broadcasted_iota(jnp.int32, (ck, LANES), 1)
    lows, rhss, dests = [], [], []
    for sub in range(q_ref.shape[1] // ck):
        rows = slice(sub * ck, (sub + 1) * ck)
        q = q_ref[0, rows, :]
        k = k_ref[0, rows, :]
        kk = _dot_nt(k, k)
        qk = _dot_nt(q, k)
        gb = gb_ref[0, rows, :]
        qf = q.astype(F32)
        kf = k.astype(F32)
        for d in range(2):
            incl = (r >= c) if d == 0 else (r <= c)
            strict = (r > c) if d == 0 else (r < c)
            for jh in range(rep):
                head = hq * rep + jh
                cols = slice(jh * hd, (jh + 1) * hd)
                bcol = jnp.sum(jnp.where(lane == d * 32 + head, gb, 0.0), axis=-1, keepdims=True)
                gcol = jnp.sum(jnp.where(lane == d * 32 + DN_V_HEADS + head, gb, 0.0), axis=-1, keepdims=True)
                grow = gt_ref[d, 0, 0, jh:jh + 1, rows]
                glast = grow[:, ck - 1:ck] if d == 0 else grow[:, 0:1]
                decay = jnp.where(incl, jnp.exp(jnp.where(incl, gcol - grow, 0.0)), 0.0)
                lows.append(jnp.where(strict, kk * decay, 0.0) * bcol)
                v = v_ref[0, rows, cols].astype(F32)
                eg = jnp.exp(gcol)
                rhss.append(jnp.concatenate([v * bcol, kf * (bcol * eg)], axis=1).astype(BF16))
                dests.append((d, rows, cols))
                qg_ref[d, 0, rows, cols] = (qf * eg).astype(qg_ref.dtype)
                kd_ref[d, 0, rows, cols] = (kf * jnp.exp(glast - gcol)).astype(kd_ref.dtype)
                a_ref[d, 0, rows, cols] = jnp.where(incl, qk * decay, 0.0).astype(a_ref.dtype)
    tinvs = _unit_triangular_inverses(lows, eye, r, c)
    uws = [_dot(tinv.astype(BF16), rhs) for tinv, rhs in zip(tinvs, rhss)]
    for uw, (d, rows, cols) in zip(uws, dests):
        u_ref[d, 0, rows, cols] = uw[:, :hd].astype(u_ref.dtype)
        w_ref[d, 0, rows, cols] = uw[:, hd:].astype(w_ref.dtype)


def _delta_chunks(q, k, v, gb, gt_a):
    bsz, t, nqk = q.shape
    nv = v.shape[2]
    hd = DN_HEAD_DIM
    nh = nqk // hd
    rep = nv // nqk
    tm = ROW_TILE
    qmap = lambda i, h, j: (i, j, h)
    omap = lambda i, h, j: (0, i, j, h)
    oshape = lambda dt: jax.ShapeDtypeStruct((2, bsz, t, nv), dt)
    ospec = pl.BlockSpec((2, 1, tm, rep * hd), omap)
    return pl.pallas_call(
        functools.partial(_delta_chunk_kernel, rep=rep),
        grid=(bsz, nh, t // tm),
        in_specs=[pl.BlockSpec((1, tm, hd), qmap), pl.BlockSpec((1, tm, hd), qmap),
                  pl.BlockSpec((1, tm, rep * hd), qmap),
                  pl.BlockSpec((1, tm, LANES), lambda i, h, j: (i, j, 0)),
                  pl.BlockSpec((2, 1, 1, 8, tm), lambda i, h, j: (0, i, h, 0, j))],
        out_specs=[ospec] * 5,
        out_shape=[oshape(BF16)] * 5,
        compiler_params=_cparams("parallel", "parallel", "parallel"),
    )(q, k, v, gb, gt_a)


def _scan_chunk_index(d, i, ctx_chunks, n_chunks):
    back = jnp.where(i < ctx_chunks, ctx_chunks - 1 - i, n_chunks - 1 - (i - ctx_chunks))
    return jnp.where(d == 0, i, back)


def _delta_scan_kernel(u_ref, w_ref, qg_ref, kd_ref, a_ref, gt_ref, o_ref, s_ref, *, n_heads):
    d = pl.program_id(0)
    i = pl.program_id(3)
    ck = DN_CHUNK
    hd = DN_HEAD_DIM

    @pl.when(i == 0)
    def _():
        s_ref[...] = jnp.zeros_like(s_ref)

    heads = range(n_heads)
    cols = [slice(jh * hd, (jh + 1) * hd) for jh in heads]
    grows = gt_ref[0, 0, 0]
    glast = jnp.where(d == 0, grows[:, ck - 1:ck], grows[:, 0:1])
    carry = jnp.exp(glast)
    ss = [s_ref[jh] for jh in heads]
    wss = [_dot(jnp.concatenate([w_ref[0, 0, :, cs], qg_ref[0, 0, :, cs]], axis=0), s.astype(BF16))
           for cs, s in zip(cols, ss)]
    v_news = [(u_ref[0, 0, :, cs] - ws[:ck]).astype(BF16) for cs, ws in zip(cols, wss)]
    outs = [ws[ck:] + _dot(a_ref[0, 0, :, cs], vn) for cs, ws, vn in zip(cols, wss, v_news)]
    upds = [_dot_tn(kd_ref[0, 0, :, cs], vn) for cs, vn in zip(cols, v_news)]
    for jh in heads:
        o_ref[0, 0, :, cols[jh]] = outs[jh].astype(o_ref.dtype)
        s_ref[jh] = ss[jh] * carry[jh:jh + 1, :] + upds[jh]


def _delta_scan(u, w, qg, kd, a, gt_b, ctx_len):
    _, bsz, t, nv = u.shape
    ck = DN_CHUNK
    hd = DN_HEAD_DIM
    nh = DN_SCAN_HEADS
    groups = nv // (nh * hd)
    n_chunks = t // ck
    ctx_chunks = ctx_len // ck
    cmap = lambda d, b, g, i: (d, b, _scan_chunk_index(d, i, ctx_chunks, n_chunks), g)
    spec = pl.BlockSpec((1, 1, ck, nh * hd), cmap)
    return pl.pallas_call(
        functools.partial(_delta_scan_kernel, n_heads=nh),
        grid=(2, bsz, groups, n_chunks),
        in_specs=[spec] * 5 + [pl.BlockSpec((1, 1, 1, nh, ck),
                                            lambda d, b, g, i: (d, b, g, 0, _scan_chunk_index(d, i, ctx_chunks, n_chunks)))],
        out_specs=spec,
        out_shape=jax.ShapeDtypeStruct((2, bsz, t, nv), BF16),
        scratch_shapes=[pltpu.VMEM((nh, hd, hd), F32)],
        compiler_params=_cparams("parallel", "parallel", "parallel", "arbitrary"),
    )(u, w, qg, kd, a, gt_b)


def _pack_bf16_pairs(a):
    n = a.shape[1] // 2
    bits = lax.bitcast_convert_type(a, jnp.uint32)
    return (bits[:, :n] & jnp.uint32(0xFFFF0000)) | (bits[:, n:] >> 16)


def _unpack_bf16_pairs(p):
    hi = lax.bitcast_convert_type(p & jnp.uint32(0xFFFF0000), F32)
    lo = lax.bitcast_convert_type(p << 16, F32)
    return jnp.concatenate([hi, lo], axis=1)


def _router_kernel(x_ref, mod_ref, whi_ref, wlo_ref, br_ref, ha_ref, hb_ref, idx_ref, wt_ref, cnt_ref, run_ref):
    first = jnp.logical_and(pl.program_id(0) == 0, pl.program_id(1) == 0)

    @pl.when(first)
    def _():
        run_ref[...] = jnp.zeros_like(run_ref)

    shift = mod_ref[0, 0, 3:4, :]
    scale = mod_ref[0, 0, 4:5, :]
    h = x_ref[0] * (1.0 + scale) + shift
    h_hi = h.astype(BF16)
    packed = _pack_bf16_pairs(h_hi.astype(F32))
    ha_ref[0] = packed[:, :SC_ROW_WORDS]
    hb_ref[0] = packed[:, SC_ROW_WORDS:]
    h_lo = (h - h_hi.astype(F32)).astype(BF16)
    w_hi = whi_ref[...]
    w_lo = wlo_ref[...]
    logits = _dot(h_hi, w_hi) + _dot(h_hi, w_lo) + _dot(h_lo, w_hi) + br_ref[...]
    lane = lax.broadcasted_iota(jnp.int32, logits.shape, 1).astype(F32)
    cur = jnp.where(lane < N_EXPERTS, logits, -jnp.inf)
    idx_out = jnp.zeros(logits.shape, F32)
    wt_out = jnp.zeros(logits.shape, F32)
    tops, sels = [], []
    hits = jnp.zeros(logits.shape, F32)
    for kk in range(TOP_K):
        m = jnp.max(cur, axis=-1, keepdims=True)
        sel = jnp.min(jnp.where(cur == m, lane, float(LANES)), axis=-1, keepdims=True)
        tops.append(m)
        sels.append(sel)
        idx_out = jnp.where(lane == kk, sel, idx_out)
        hits = jnp.where(lane == sel, 1.0, hits)
        cur = jnp.where(lane == sel, -jnp.inf, cur)
    es = [jnp.exp(m - tops[0]) for m in tops]
    den = es[0] + es[1] + es[2] + es[3]
    for kk in range(TOP_K):
        wt_out = jnp.where(lane == kk, es[kk] / den, wt_out)
    tm = logits.shape[0]
    r = lax.broadcasted_iota(jnp.int32, (tm, tm), 0)
    c = lax.broadcasted_iota(jnp.int32, (tm, tm), 1)
    before = _dot(jnp.where(c < r, 1.0, 0.0).astype(BF16), hits.astype(BF16)) + run_ref[...]
    for kk in range(TOP_K):
        rank = jnp.sum(jnp.where(lane == sels[kk], before, 0.0), axis=-1, keepdims=True)
        idx_out = jnp.where(lane == TOP_K + kk, rank, idx_out)
    total = run_ref[...] + jnp.sum(hits, axis=0, keepdims=True)
    run_ref[...] = total
    cnt_ref[...] = jnp.broadcast_to(total, cnt_ref.shape)
    idx_ref[0] = idx_out.astype(jnp.int32)
    wt_ref[0] = wt_out


def _router(x, modtab, w_hi, w_lo, b_r, ctx_len, first_block):
    bsz, t, d = x.shape
    tm = ROW_TILE
    ctx_blocks = ctx_len // tm
    nb = t // tm - first_block
    row_in = lambda i, j: (i, j + first_block, 0)
    row = lambda i, j: (i, j, 0)
    const2 = lambda i, j: (0, 0)
    tout = nb * tm
    return pl.pallas_call(
        _router_kernel,
        grid=(bsz, nb),
        in_specs=[pl.BlockSpec((1, tm, d), row_in),
                  pl.BlockSpec((1, 1, 8, d), lambda i, j: (i, _seg_of_block(j + first_block, ctx_blocks), 0, 0)),
                  pl.BlockSpec((d, LANES), const2), pl.BlockSpec((d, LANES), const2),
                  pl.BlockSpec((1, LANES), const2)],
        out_specs=[pl.BlockSpec((1, tm, SC_ROW_WORDS), row), pl.BlockSpec((1, tm, SC_ROW_WORDS), row),
                   pl.BlockSpec((1, tm, LANES), row),
                   pl.BlockSpec((1, tm, LANES), row), pl.BlockSpec((8, LANES), const2)],
        out_shape=[jax.ShapeDtypeStruct((bsz, tout, SC_ROW_WORDS), jnp.uint32),
                   jax.ShapeDtypeStruct((bsz, tout, SC_ROW_WORDS), jnp.uint32),
                   jax.ShapeDtypeStruct((bsz, tout, LANES), jnp.int32),
                   jax.ShapeDtypeStruct((bsz, tout, LANES), F32),
                   jax.ShapeDtypeStruct((8, LANES), F32)],
        scratch_shapes=[pltpu.VMEM((1, LANES), F32)],
        compiler_params=_cparams("arbitrary", "arbitrary"),
    )(x, modtab, w_hi, w_lo, b_r)


def _expert_kernel(be_ref, nused_ref, xa_ref, xb_ref, w1_ref, b1_ref, w2_ref, b2_ref, ya_ref, yb_ref,
                   w1b_ref, w2b_ref):
    i = pl.program_id(0)
    dff = w2_ref.shape[1]
    prev = be_ref[jnp.maximum(i - 1, 0)]
    new_expert = jnp.logical_or(i == 0, be_ref[i] != prev)

    @pl.when(jnp.logical_and(new_expert, i < nused_ref[0]))
    def _():
        for r0 in range(0, w1_ref.shape[1], LANES):
            w1b_ref[r0:r0 + LANES, :] = w1_ref[0, r0:r0 + LANES, :].astype(BF16)
        for r0 in range(0, dff, LANES):
            w2b_ref[r0:r0 + LANES, :] = w2_ref[0, r0:r0 + LANES, :].astype(BF16)

    @pl.when(i < nused_ref[0])
    def _():
        x = _unpack_bf16_pairs(jnp.concatenate([xa_ref[...], xb_ref[...]], axis=1)).astype(BF16)
        chunks = range(0, dff, COL_CHUNK)
        gates = [_dot(x, w1b_ref[:, c0:c0 + COL_CHUNK]) + b1_ref[0, :, c0:c0 + COL_CHUNK] for c0 in chunks]
        ups = [_dot(x, w1b_ref[:, dff + c0:dff + c0 + COL_CHUNK]) + b1_ref[0, :, dff + c0:dff + c0 + COL_CHUNK]
               for c0 in chunks]
        gates = [jnp.minimum(gate, SWIGLU_LIMIT) for gate in gates]
        ups = [jnp.clip(up, -SWIGLU_LIMIT, SWIGLU_LIMIT) for up in ups]
        acts = [((up + 1.0) * gate * jax.nn.sigmoid(SWIGLU_ALPHA * gate)).astype(BF16) for gate, up in zip(gates, ups)]
        parts = [_dot(act, w2b_ref[c0:c0 + COL_CHUNK, :]) for act, c0 in zip(acts, chunks)]
        acc = parts[0]
        for part in parts[1:]:
            acc = acc + part
        y = (acc + b2_ref[0]).astype(BF16).astype(F32)
        packed = _pack_bf16_pairs(y)
        ya_ref[...] = packed[:, :SC_ROW_WORDS]
        yb_ref[...] = packed[:, SC_ROW_WORDS:]

    @pl.when(i >= nused_ref[0])
    def _():
        ya_ref[...] = jnp.zeros_like(ya_ref)
        yb_ref[...] = jnp.zeros_like(yb_ref)


def _experts(xa, xb, block_e, n_used, w1, b1, w2, b2, layer):
    n_rows, dq = xa.shape
    d = 4 * dq
    tm = EXPERT_TILE
    depth, ne, _, dff2 = w1.shape
    dff = dff2 // 2
    w1 = w1.reshape(depth * ne, d, dff2)
    w2 = w2.reshape(depth * ne, dff, d)
    b1 = b1.reshape(depth * ne, 1, dff2)
    b2 = b2.reshape(depth * ne, 1, d)
    emap = lambda i, be, nu: (layer * ne + be[i], 0, 0)
    return pl.pallas_call(
        _expert_kernel,
        grid_spec=pltpu.PrefetchScalarGridSpec(
            num_scalar_prefetch=2, grid=(n_rows // tm,),
            in_specs=[pl.BlockSpec((tm, dq), lambda i, be, nu: (i, 0)),
                      pl.BlockSpec((tm, dq), lambda i, be, nu: (i, 0)),
                      pl.BlockSpec((1, d, dff2), emap),
                      pl.BlockSpec((1, 1, dff2), emap),
                      pl.BlockSpec((1, dff, d), emap),
                      pl.BlockSpec((1, 1, d), emap)],
            out_specs=[pl.BlockSpec((tm, dq), lambda i, be, nu: (i, 0))] * 2,
            scratch_shapes=[pltpu.VMEM((d, dff2), BF16), pltpu.VMEM((dff, d), BF16)]),
        out_shape=[jax.ShapeDtypeStruct((n_rows, dq), jnp.uint32)] * 2,
        compiler_params=_cparams("arbitrary"),
    )(block_e, n_used, xa, xb, w1, b1, w2, b2)


def _combine_kernel(x_ref, mod_ref, za_ref, zb_ref, wt_ref, g_ref, b_ref, out_ref, *, alpha):
    wt = wt_ref[0]
    f = None
    for kk in range(TOP_K):
        term = _unpack_bf16_pairs(jnp.concatenate([za_ref[kk, 0], zb_ref[kk, 0]], axis=1)) * wt[:, kk:kk + 1]
        f = term if f is None else f + term
    out_ref[0] = _resid_ln(x_ref[0], f, mod_ref[0, 0, 5:6, :], g_ref[...], b_ref[...], alpha)


def _combine(x, modtab, za, zb, wt, ln_g, ln_b, alpha, ctx_len, first_block):
    bsz, t, d = x.shape
    tm = ROW_TILE
    ctx_blocks = ctx_len // tm
    nb = t // tm - first_block
    row_in = lambda i, j: (i, j + first_block, 0)
    row = lambda i, j: (i, j, 0)
    const2 = lambda i, j: (0, 0)
    return pl.pallas_call(
        functools.partial(_combine_kernel, alpha=alpha),
        grid=(bsz, nb),
        in_specs=[pl.BlockSpec((1, tm, d), row_in),
                  pl.BlockSpec((1, 1, 8, d), lambda i, j: (i, _seg_of_block(j + first_block, ctx_blocks), 0, 0)),
                  pl.BlockSpec((TOP_K, 1, tm, SC_ROW_WORDS), lambda i, j: (0, i, j, 0)),
                  pl.BlockSpec((TOP_K, 1, tm, SC_ROW_WORDS), lambda i, j: (0, i, j, 0)),
                  pl.BlockSpec((1, tm, LANES), row),
                  pl.BlockSpec((1, d), const2), pl.BlockSpec((1, d), const2)],
        out_specs=pl.BlockSpec((1, tm, d), row),
        out_shape=jax.ShapeDtypeStruct((bsz, nb * tm, d), F32),
        compiler_params=_cparams("parallel", "parallel"),
    )(x, modtab, za, zb, wt, ln_g, ln_b)


def _moe_layer(x, modtab, w_r, b_r, expert_params, ln_g, ln_b, alpha, ctx_len, latent_only):
    bsz, t, d = x.shape
    first_block = ctx_len // ROW_TILE if latent_only else 0
    wr_pad = jnp.pad(w_r, ((0, 0), (0, LANES - N_EXPERTS)))
    br_pad = jnp.pad(b_r, (0, LANES - N_EXPERTS)).reshape(1, LANES)
    wr_hi = wr_pad.astype(BF16)
    wr_lo = (wr_pad - wr_hi.astype(F32)).astype(BF16)
    ha, hb, idx, wt, cnt = _router(x, modtab, wr_hi, wr_lo, br_pad, ctx_len, first_block)
    tr = ha.shape[1]
    dq = ha.shape[2]
    n_tok = bsz * tr
    n_assign = n_tok * TOP_K
    tm = EXPERT_TILE
    experts = jnp.arange(N_EXPERTS, dtype=jnp.int32)
    counts = cnt[0, :N_EXPERTS].astype(jnp.int32)
    padded = (counts + tm - 1) // tm * tm
    pends = jnp.cumsum(padded)
    pstarts = pends - padded
    e_tk = idx[:, :, :TOP_K].reshape(n_tok, TOP_K)
    rank = idx[:, :, TOP_K:2 * TOP_K].reshape(n_tok, TOP_K)
    dest = jnp.sum(jnp.where(e_tk[:, :, None] == experts, pstarts, 0), axis=-1) + rank
    n_rows = (-(-n_assign // tm) + N_EXPERTS) * tm
    n_blocks = n_rows // tm
    starts = jnp.arange(n_blocks, dtype=jnp.int32) * tm
    block_e = jnp.minimum(jnp.sum((pends[None, :] <= starts[:, None]).astype(jnp.int32), axis=1), N_EXPERTS - 1)
    n_used = (pends[-1] // tm).astype(jnp.int32).reshape(1)
    dest_t = dest.T
    xa = _sc_scatter_rows(ha.reshape(n_tok, dq), dest_t, n_rows)
    xb = _sc_scatter_rows(hb.reshape(n_tok, dq), dest_t, n_rows)
    ya, yb = _experts(xa, xb, block_e, n_used, *expert_params)
    flat = dest_t.reshape(1, n_assign)
    za = _sc_gather_rows(ya, flat).reshape(TOP_K, bsz, tr, dq)
    zb = _sc_gather_rows(yb, flat).reshape(TOP_K, bsz, tr, dq)
    return _combine(x, modtab, za, zb, wt, ln_g, ln_b, alpha, ctx_len, first_block)


SC_WINDOW = 128
SC_ROW_WORDS = 256


def _sc_mesh():
    return plsc.VectorSubcoreMesh(core_axis_name="c", subcore_axis_name="s")


def _sc_scatter_rows(src, idx, n_out):
    n_src, width = src.shape
    n_k = idx.shape[0]
    per_k = n_src // SC_WINDOW

    @functools.partial(pl.kernel, out_type=jax.ShapeDtypeStruct((n_out, width), src.dtype), mesh=_sc_mesh())
    def scatter(x_hbm, i_hbm, o_hbm):
        def body(x_vmem, i_vmem):
            pltpu.sync_copy(x_vmem, o_hbm.at[i_vmem.at[0]])

        pltpu.emit_pipeline(
            body, grid=(n_k * per_k,),
            in_specs=[pl.BlockSpec((SC_WINDOW, width), lambda i: (i % per_k, 0)),
                      pl.BlockSpec((1, SC_WINDOW), lambda i: (i // per_k, i % per_k))],
            out_specs=[],
            core_axis_name=("c", "s"),
            dimension_semantics=(pltpu.PARALLEL,),
        )(x_hbm, i_hbm)

    return scatter(src, idx)


def _sc_gather_rows(table, idx):
    width = table.shape[1]
    n_idx = idx.shape[1]

    @functools.partial(pl.kernel, out_type=jax.ShapeDtypeStruct((n_idx, width), table.dtype), mesh=_sc_mesh())
    def gather(x_hbm, i_hbm, o_hbm):
        def body(i_vmem, o_vmem):
            pltpu.sync_copy(x_hbm.at[i_vmem.at[0]], o_vmem)

        pltpu.emit_pipeline(
            body, grid=(n_idx // SC_WINDOW,),
            in_specs=[pl.BlockSpec((1, SC_WINDOW), lambda i: (0, i))],
            out_specs=[pl.BlockSpec((SC_WINDOW, width), lambda i: (i, 0))],
            core_axis_name=("c", "s"),
            dimension_semantics=(pltpu.PARALLEL,),
        )(i_hbm, o_hbm)

    return gather(table, idx)


def _rot_cols(w, head_dim):
    lead = w.shape[:-1]
    q = head_dim // 4
    wr = w.reshape(lead + (-1, 4, q))
    out = jnp.stack([-wr[..., 1, :], wr[..., 0, :], -wr[..., 3, :], wr[..., 2, :]], axis=-2)
    return out.reshape(w.shape)


def _dup_heads(w, head_dim):
    lead = w.shape[:-1]
    wr = w.reshape(lead + (-1, 1, head_dim))
    return jnp.concatenate([wr, wr], axis=-2).reshape(lead + (-1,))


def _rope_tables(n_lat, ctx_len, head_dim):
    rows = n_lat // GRID_W
    row = jnp.repeat(jnp.arange(rows, dtype=F32), GRID_W)
    col = jnp.tile(jnp.arange(GRID_W, dtype=F32), rows)
    half = head_dim // 2
    inv = ROPE_BASE ** (-jnp.arange(0, half, 2, dtype=F32) / half)
    ar = row[:, None] * inv
    ac = col[:, None] * inv
    ang = jnp.concatenate([ar, ar, ac, ac], axis=-1)
    cos = jnp.concatenate([jnp.ones((ctx_len, head_dim), F32), jnp.cos(ang)], axis=0)
    sin = jnp.concatenate([jnp.zeros((ctx_len, head_dim), F32), jnp.sin(ang)], axis=0)
    reps = LANES // head_dim
    return jnp.tile(cos, (1, reps)), jnp.tile(sin, (1, reps))


def _conv_mixer(x, modtab, w_in, w_conv, w_out, ln_g, ln_b, alpha, ctx_len):
    d = x.shape[2]
    plan = [("plain", 0, 0, d, 0, None, 1.0), ("mul", 1, 0, d, d, 2 * d, 1.0)]
    bg, p = _project(x, modtab, w_in.astype(BF16), jnp.zeros((1, 3 * d), F32), plan, (d, d), (BF16, BF16), ctx_len)
    cw = jnp.pad(w_conv, ((0, 8 - SC_WIDTH), (0, 0)))
    return _mixer_out("conv", x, modtab, (bg, p), w_out.astype(BF16), jnp.zeros((1, d), F32), ln_g, ln_b, alpha,
                      ctx_len, extra=cw)


def _swa_mixer(x, modtab, w_qkv, b_qkv, sink, w_o, b_o, ln_g, ln_b, alpha, ctx_len, rope):
    d = x.shape[2]
    nq = SWA_HEADS * SWA_HEAD_DIM
    nkv = SWA_KV_HEADS * SWA_HEAD_DIM
    hd = SWA_HEAD_DIM

    def arrange(a):
        q, k, v = a[..., :nq], a[..., nq:nq + nkv], a[..., nq + nkv:]
        kk = _dup_heads(k, hd)
        return jnp.concatenate([q, _rot_cols(q, hd), kk, _rot_cols(kk, hd), _dup_heads(v, hd)], axis=-1)

    w = arrange(w_qkv).astype(BF16)
    b = arrange(b_qkv.reshape(1, -1))
    plan = [("rope", 0, 0, nq, 0, nq, SWA_HEAD_DIM ** -0.5),
            ("rope", 1, 0, 2 * nkv, 2 * nq, 2 * nq + 2 * nkv, 1.0),
            ("plain", 2, 0, 2 * nkv, 2 * nq + 4 * nkv, None, 1.0)]
    q, kk, vv = _project(x, modtab, w, b, plan, (nq, 2 * nkv, 2 * nkv), (BF16, BF16, BF16), ctx_len, rope=rope)
    o = _swa_attention(q, kk, vv, sink.astype(F32), ctx_len)
    return _mixer_out("plain", x, modtab, (o,), w_o.astype(BF16), b_o.reshape(1, d), ln_g, ln_b, alpha, ctx_len)


def _diff_mixer(x, modtab, w_qkv, lam_p, subln_g, w_o, lam_init, ln_g, ln_b, alpha, ctx_len, rope):
    d = x.shape[2]
    hd = DIFF_HEAD_DIM
    wq, wk, wv = w_qkv[:, :d], w_qkv[:, d:2 * d], w_qkv[:, 2 * d:]
    w = jnp.concatenate([wq, _rot_cols(wq, hd), wk, _rot_cols(wk, hd), wv], axis=-1).astype(BF16)
    plan = [("rope", 0, 0, d, 0, d, DIFF_HEAD_DIM ** -0.5 * math.log2(math.e)),
            ("rope", 1, 0, d, 2 * d, 3 * d, 1.0)]
    for h in range(DIFF_HEADS):
        plan += [("plain", 2, 2 * h * LANES, LANES, 4 * d + h * LANES, None, 1.0),
                 ("ones", 2, (2 * h + 1) * LANES, LANES, None, None, 1.0)]
    q, k, v = _project(x, modtab, w, jnp.zeros((1, 5 * d), F32), plan, (d, d, 2 * d), (BF16, BF16, BF16), ctx_len,
                       rope=rope)
    o = _diff_attention(q, k, v, lam_p.astype(F32), subln_g.astype(F32), ctx_len, lam_init)
    return _mixer_out("plain", x, modtab, (o,), w_o.astype(BF16), jnp.zeros((1, d), F32), ln_g, ln_b, alpha, ctx_len)


def _delta_mixer(x, modtab, w_qkvz, w_ba, a_log, dt_bias, w_conv, norm_g, w_o, ln_g, ln_b, alpha, ctx_len):
    bsz, t, d = x.shape
    nqk = DN_QK_HEADS * DN_HEAD_DIM
    nv = DN_V_HEADS * DN_HEAD_DIM
    nc = 2 * nqk + nv
    nba = w_ba.shape[1]
    w = jnp.concatenate([w_qkvz, jnp.pad(w_ba, ((0, 0), (0, LANES - nba)))], axis=-1).astype(BF16)
    ntot = w.shape[1]
    plan = [("plain", 0, 0, nc, 0, None, 1.0), ("plain", 1, 0, nv, nc, None, 1.0),
            ("plain", 2, 0, LANES, nc + nv, None, 1.0)]
    qkv_pre, z, ba = _project(x, modtab, w, jnp.zeros((1, ntot), F32), plan, (nc, nv, LANES), (BF16, BF16, F32),
                              ctx_len)
    zeros16 = jnp.zeros((DN_V_HEADS,), F32)
    lanes_of = lambda p: jnp.pad(jnp.concatenate([zeros16, p[0], zeros16, p[1]]), (0, LANES - 4 * DN_V_HEADS))
    alog_l = lanes_of(a_log.astype(F32)).reshape(1, LANES)
    dtb_l = lanes_of(dt_bias.astype(F32)).reshape(1, LANES)
    cw = jnp.pad(w_conv, ((0, 8 - DN_CONV), (0, 0)))
    q, k, v, gb = _delta_prep(qkv_pre, cw, ba, alog_l, dtb_l, ctx_len, nqk, nv)
    rep = DN_V_HEADS // DN_QK_HEADS
    gcum = jnp.stack([gb[:, :, DN_V_HEADS:2 * DN_V_HEADS], gb[:, :, 3 * DN_V_HEADS:4 * DN_V_HEADS]])
    gt = jnp.transpose(gcum, (0, 1, 3, 2))
    gt_a = jnp.pad(gt.reshape(2, bsz, DN_QK_HEADS, rep, t), ((0, 0), (0, 0), (0, 0), (0, 8 - rep), (0, 0)))
    gt_b = gt.reshape(2, bsz, DN_V_HEADS // DN_SCAN_HEADS, DN_SCAN_HEADS, t)
    u, w_, qg, kd, a = _delta_chunks(q, k, v, gb, gt_a)
    o2 = _delta_scan(u, w_, qg, kd, a, gt_b, ctx_len)
    return _mixer_out("delta", x, modtab, (o2[0], o2[1], z), w_o.astype(BF16), jnp.zeros((1, d), F32),
                      ln_g, ln_b, alpha, ctx_len, extra=norm_g.astype(F32).reshape(1, DN_HEAD_DIM))


def kernel(x, c, ctx, c_ctx, mod_w, mod_b, ln1_g, ln1_b, ln2_g, ln2_b, router_w, router_b, exp_w1, exp_b1, exp_w2, exp_b2, conv_in_w, conv_w, conv_out_w, swa_qkv_w, swa_qkv_b, swa_sink, swa_out_w, swa_out_b, diff_qkv_w, diff_lambda, diff_subln_g, diff_out_w, delta_qkvz_w, delta_ba_w, delta_a_log, delta_dt_bias, delta_conv_w, delta_norm_g, delta_out_w):
    bsz, n_lat, d = x.shape
    ctx_len = ctx.shape[1]
    depth = mod_w.shape[0]
    alpha = (2 * depth) ** 0.25
    xs = jnp.concatenate([ctx, x], axis=1)
    cc = jnp.zeros((16, d), F32).at[:bsz].set(c).at[bsz].set(c_ctx)
    mods = _mod_vectors(cc, mod_w, mod_b).reshape(depth, 16, 6, d)
    mod_lat = mods[:, :bsz]
    mod_ctx = jnp.broadcast_to(mods[:, bsz:bsz + 1], mod_lat.shape)
    modtabs = jnp.pad(jnp.stack([mod_ctx, mod_lat], axis=2), ((0, 0), (0, 0), (0, 0), (0, 2), (0, 0)))
    rope = _rope_tables(n_lat, ctx_len, SWA_HEAD_DIM)
    row1 = lambda a: a.reshape(1, d)
    for i in range(depth):
        last = i == depth - 1
        kind, j = i % N_MIXERS, i // N_MIXERS
        mt = modtabs[i]
        g1, b1 = row1(ln1_g[i]), row1(ln1_b[i])
        if kind == 0:
            xs = _conv_mixer(xs, mt, conv_in_w[j], conv_w[j], conv_out_w[j], g1, b1, alpha, ctx_len)
        elif kind == 1:
            xs = _swa_mixer(xs, mt, swa_qkv_w[j], swa_qkv_b[j], swa_sink[j], swa_out_w[j], swa_out_b[j], g1, b1,
                            alpha, ctx_len, rope)
        elif kind == 2:
            lam_init = 0.8 - 0.6 * math.exp(-0.3 * i)
            xs = _diff_mixer(xs, mt, diff_qkv_w[j], diff_lambda[j], diff_subln_g[j], diff_out_w[j], lam_init,
                             g1, b1, alpha, ctx_len, rope)
        else:
            xs = _delta_mixer(xs, mt, delta_qkvz_w[j], delta_ba_w[j], delta_a_log[j], delta_dt_bias[j],
                              delta_conv_w[j], delta_norm_g[j], delta_out_w[j], g1, b1, alpha, ctx_len)
        xs = _moe_layer(xs, mt, router_w[i], router_b[i], (exp_w1, exp_b1, exp_w2, exp_b2, i),
                        row1(ln2_g[i]), row1(ln2_b[i]), alpha, ctx_len, latent_only=last)
    return xs
```

```python
import functools
import math

import jax
import jax.numpy as jnp
from jax import lax
from jax.experimental import pallas as pl
from jax.experimental.pallas import tpu as pltpu
from jax.experimental.pallas import tpu_sc as plsc

F32 = jnp.float32
BF16 = jnp.bfloat16

GRID_W = 64
N_MIXERS = 4
LN_EPS = 1e-5
RMS_EPS = 1e-6
NEG_INF = -1e30
ROPE_BASE = 10000.0
SC_WIDTH = 3
SWA_HEADS = 16
SWA_KV_HEADS = 4
SWA_HEAD_DIM = 64
SWA_WINDOW = 128
DIFF_HEADS = 8
DIFF_HEAD_DIM = 64
DN_QK_HEADS = 8
DN_V_HEADS = 16
DN_HEAD_DIM = 128
DN_CONV = 5
N_EXPERTS = 32
TOP_K = 4
SWIGLU_LIMIT = 7.0
SWIGLU_ALPHA = 1.702

LANES = 128
HALO_ROWS = 16
VMEM_LIMIT = 56 * 1024 * 1024

ROW_TILE = 256
ATT_BLOCK = 128
DIFF_Q_BLOCK = 256
DN_CHUNK = 128
EXPERT_TILE = 512
COL_CHUNK = 512
BATCH_GROUPS = 2


def _cparams(*sem):
    return pltpu.CompilerParams(dimension_semantics=sem, vmem_limit_bytes=VMEM_LIMIT)


def _split3(x):
    hi = x.astype(BF16)
    r1 = x - hi.astype(F32)
    mid = r1.astype(BF16)
    lo = (r1 - mid.astype(F32)).astype(BF16)
    return hi, mid, lo


def _dot(a, b):
    return jnp.dot(a, b, preferred_element_type=F32)


def _dot_nt(a, b):
    return lax.dot_general(a, b, (((1,), (1,)), ((), ())), preferred_element_type=F32)


def _dot_tn(a, b):
    return lax.dot_general(a, b, (((0,), (0,)), ((), ())), preferred_element_type=F32)


def _silu(x):
    return x * jax.nn.sigmoid(x)


def _seg_of_block(j, ctx_blocks):
    return jnp.where(j < ctx_blocks, 0, 1)


def _mods_kernel(c_ref, w_ref, b_ref, o_ref):
    s = _silu(c_ref[...]).astype(BF16)
    o_ref[0] = _dot(s, w_ref[0].astype(BF16)) + b_ref[0]


def _mod_vectors(cc, mod_w, mod_b):
    depth, d, n = mod_w.shape
    rows = cc.shape[0]
    tn = 1536
    return pl.pallas_call(
        _mods_kernel,
        grid=(depth, n // tn),
        in_specs=[pl.BlockSpec((rows, d), lambda l, j: (0, 0)),
                  pl.BlockSpec((1, d, tn), lambda l, j: (l, 0, j)),
                  pl.BlockSpec((1, 1, tn), lambda l, j: (l, 0, j))],
        out_specs=pl.BlockSpec((1, rows, tn), lambda l, j: (l, 0, j)),
        out_shape=jax.ShapeDtypeStruct((depth, rows, n), F32),
        compiler_params=_cparams("parallel", "parallel"),
    )(cc, mod_w, mod_b.reshape(depth, 1, n))


def _proj_kernel(*refs, plan, n_out, use_rope):
    x_ref, mod_ref, w_ref, b_ref = refs[:4]
    pos = 4
    if use_rope:
        cos_ref, sin_ref = refs[4:6]
        pos = 6
    outs = refs[pos:pos + n_out]
    shift = mod_ref[0, 0, 0:1, :]
    scale = mod_ref[0, 0, 1:2, :]
    h = (x_ref[0] * (1.0 + scale) + shift).astype(BF16)

    def acc(col, width):
        return _dot(h, w_ref[:, col:col + width]) + b_ref[:, col:col + width]

    for kind, oi, ocol, width, wcol, wcol2, mult in plan:
        if kind == "ones":
            outs[oi][0, :, ocol:ocol + width] = jnp.ones((x_ref.shape[1], width), outs[oi].dtype)
            continue
        for c0 in range(0, width, COL_CHUNK):
            cw = min(COL_CHUNK, width - c0)
            a = acc(wcol + c0, cw)
            if kind == "rope":
                reps = cw // LANES
                cos = jnp.tile(cos_ref[...], (1, reps))
                sin = jnp.tile(sin_ref[...], (1, reps))
                a = (a * cos + acc(wcol2 + c0, cw) * sin) * mult
            elif kind == "mul":
                a = a * acc(wcol2 + c0, cw)
            outs[oi][0, :, ocol + c0:ocol + c0 + cw] = a.astype(outs[oi].dtype)


def _project(x, modtab, w, b, plan, out_widths, out_dtypes, ctx_len, rope=None):
    bsz, t, d = x.shape
    n = w.shape[1]
    tm = ROW_TILE
    ctx_blocks = ctx_len // tm
    mt = modtab
    in_specs = [pl.BlockSpec((1, tm, d), lambda i, j: (i, j, 0)),
                pl.BlockSpec((1, 1, 8, d), lambda i, j: (i, _seg_of_block(j, ctx_blocks), 0, 0)),
                pl.BlockSpec((d, n), lambda i, j: (0, 0)),
                pl.BlockSpec((1, n), lambda i, j: (0, 0))]
    args = [x, mt, w, b]
    if rope is not None:
        in_specs += [pl.BlockSpec((tm, LANES), lambda i, j: (j, 0))] * 2
        args += list(rope)
    out_specs = [pl.BlockSpec((1, tm, ow), lambda i, j: (i, j, 0)) for ow in out_widths]
    out_shape = [jax.ShapeDtypeStruct((bsz, t, ow), dt) for ow, dt in zip(out_widths, out_dtypes)]
    return pl.pallas_call(
        functools.partial(_proj_kernel, plan=tuple(plan), n_out=len(out_widths), use_rope=rope is not None),
        grid=(bsz, t // tm),
        in_specs=in_specs, out_specs=out_specs, out_shape=out_shape,
        compiler_params=_cparams("parallel", "parallel"),
    )(*args)


def _resid_ln(x, y, gate, g, b, alpha):
    r = alpha * x + gate * y
    mu = jnp.mean(r, axis=-1, keepdims=True)
    rc = r - mu
    var = jnp.mean(rc * rc, axis=-1, keepdims=True)
    return rc * lax.rsqrt(var + LN_EPS) * g + b


def _centred_conv(ext_ref, p, prev, nxt, w_ref, width, has_prev, has_next):
    tm = p.shape[0]
    h = HALO_ROWS
    pad = (width - 1) // 2
    ext_ref[0:h, :] = jnp.where(has_prev, prev, 0.0)
    ext_ref[h:h + tm, :] = p
    ext_ref[h + tm:h + tm + h, :] = jnp.where(has_next, nxt, 0.0)
    acc = None
    for k in range(width):
        term = ext_ref[h - pad + k:h - pad + k + tm, :] * w_ref[k:k + 1, :]
        acc = term if acc is None else acc + term
    return acc


def _seg_edges(j, ctx_blocks, n_blocks):
    has_prev = jnp.logical_and(j != 0, j != ctx_blocks)
    has_next = jnp.logical_and(j != ctx_blocks - 1, j != n_blocks - 1)
    return has_prev, has_next


def _out_plain_kernel(x_ref, mod_ref, o_ref, w_ref, b_ref, g_ref, bb_ref, out_ref, *, alpha):
    y = _dot(o_ref[0], w_ref[...]) + b_ref[...]
    out_ref[0] = _resid_ln(x_ref[0], y, mod_ref[0, 0, 2:3, :], g_ref[...], bb_ref[...], alpha)


def _out_conv_kernel(x_ref, mod_ref, bg_ref, p_ref, pp_ref, pn_ref, cw_ref, w_ref, b_ref, g_ref, bb_ref,
                     out_ref, ext_ref, *, alpha, ctx_blocks, n_blocks):
    j = pl.program_id(1)
    has_prev, has_next = _seg_edges(j, ctx_blocks, n_blocks)
    conv = _centred_conv(ext_ref, p_ref[0].astype(F32), pp_ref[0].astype(F32), pn_ref[0].astype(F32), cw_ref,
                         SC_WIDTH, has_prev, has_next)
    o = (bg_ref[0].astype(F32) * conv).astype(BF16)
    y = _dot(o, w_ref[...]) + b_ref[...]
    out_ref[0] = _resid_ln(x_ref[0], y, mod_ref[0, 0, 2:3, :], g_ref[...], bb_ref[...], alpha)


def _out_delta_kernel(x_ref, mod_ref, of_ref, ob_ref, z_ref, ng_ref, w_ref, b_ref, g_ref, bb_ref,
                      out_ref, *, alpha):
    hd = DN_HEAD_DIM
    ng = ng_ref[...]
    acc = None
    for h0 in range(0, of_ref.shape[2], hd):
        o = of_ref[0, :, h0:h0 + hd].astype(F32) + ob_ref[0, :, h0:h0 + hd].astype(F32)
        o = o * lax.rsqrt(jnp.mean(o * o, axis=-1, keepdims=True) + RMS_EPS) * ng
        o = (o * _silu(z_ref[0, :, h0:h0 + hd].astype(F32))).astype(BF16)
        part = _dot(o, w_ref[h0:h0 + hd, :])
        acc = part if acc is None else acc + part
    y = acc + b_ref[...]
    out_ref[0] = _resid_ln(x_ref[0], y, mod_ref[0, 0, 2:3, :], g_ref[...], bb_ref[...], alpha)


def _mixer_out(kind, x, modtab, acts, w_o, b_o, ln_g, ln_b, alpha, ctx_len, extra=None):
    bsz, t, d = x.shape
    tm = ROW_TILE
    nb = t // tm
    ctx_blocks = ctx_len // tm
    kin = w_o.shape[0]
    row = lambda i, j: (i, j, 0)
    const2 = lambda i, j: (0, 0)
    x_spec = pl.BlockSpec((1, tm, d), row)
    mod_spec = pl.BlockSpec((1, 1, 8, d), lambda i, j: (i, _seg_of_block(j, ctx_blocks), 0, 0))
    tail_specs = [pl.BlockSpec((kin, d), const2), pl.BlockSpec((1, d), const2),
                  pl.BlockSpec((1, d), const2), pl.BlockSpec((1, d), const2)]
    tail_args = [w_o, b_o, ln_g, ln_b]
    scratch = []
    if kind == "plain":
        body = functools.partial(_out_plain_kernel, alpha=alpha)
        in_specs = [x_spec, mod_spec, pl.BlockSpec((1, tm, kin), row)] + tail_specs
        args = [x, modtab, acts[0]] + tail_args
    elif kind == "conv":
        body = functools.partial(_out_conv_kernel, alpha=alpha, ctx_blocks=ctx_blocks, n_blocks=nb)
        hb = tm // HALO_ROWS
        last_halo = t // HALO_ROWS - 1
        in_specs = [x_spec, mod_spec, pl.BlockSpec((1, tm, d), row), pl.BlockSpec((1, tm, d), row),
                    pl.BlockSpec((1, HALO_ROWS, d), lambda i, j: (i, jnp.maximum(j * hb - 1, 0), 0)),
                    pl.BlockSpec((1, HALO_ROWS, d), lambda i, j: (i, jnp.minimum((j + 1) * hb, last_halo), 0)),
                    pl.BlockSpec((8, d), const2)] + tail_specs
        args = [x, modtab, acts[0], acts[1], acts[1], acts[1], extra] + tail_args
        scratch = [pltpu.VMEM((tm + 2 * HALO_ROWS, d), F32)]
    else:
        body = functools.partial(_out_delta_kernel, alpha=alpha)
        in_specs = [x_spec, mod_spec, pl.BlockSpec((1, tm, kin), row), pl.BlockSpec((1, tm, kin), row),
                    pl.BlockSpec((1, tm, kin), row), pl.BlockSpec((1, DN_HEAD_DIM), const2)] + tail_specs
        args = [x, modtab, acts[0], acts[1], acts[2], extra] + tail_args
    return pl.pallas_call(
        body, grid=(bsz, nb), in_specs=in_specs,
        out_specs=pl.BlockSpec((1, tm, d), row),
        out_shape=jax.ShapeDtypeStruct((bsz, t, d), F32),
        scratch_shapes=scratch,
        compiler_params=_cparams("parallel", "parallel"),
    )(*args)


def _swa_kernel(sink_ref, q_ref, kp_ref, ko_ref, kn_ref, kc_ref, vp_ref, vo_ref, vn_ref, vc_ref, o_ref,
                *, n_lat_blocks, ctx_blocks):
    j = pl.program_id(1)
    blk = ATT_BLOCK
    lat = j - ctx_blocks
    is_lat = j >= ctx_blocks
    n_keys = 3 * blk + kc_ref.shape[1]
    r = lax.broadcasted_iota(jnp.int32, (blk, n_keys), 0)
    c = lax.broadcasted_iota(jnp.int32, (blk, n_keys), 1)
    ok_prev = jnp.logical_and(is_lat, lat > 0)
    ok_next = jnp.logical_and(is_lat, lat < n_lat_blocks - 1)
    r_max = jnp.where(c < blk, jnp.where(ok_prev, c, -1),
                      jnp.where(c < 2 * blk, jnp.where(is_lat, blk, -1),
                                jnp.where(c < 3 * blk, jnp.where(ok_next, blk, -1), blk)))
    r_min = jnp.where(c < 2 * blk, 0, jnp.where(c < 3 * blk, c - 2 * blk, 0))
    valid = jnp.logical_and(r >= r_min, r <= r_max)
    lane = lax.broadcasted_iota(jnp.int32, (blk, LANES), 1)
    lo = lane < SWA_HEAD_DIM
    group = SWA_HEADS // SWA_KV_HEADS
    for g in range(SWA_KV_HEADS):
        gs = slice(g * LANES, (g + 1) * LANES)
        kcat = jnp.concatenate([kp_ref[0, :, gs], ko_ref[0, :, gs], kn_ref[0, :, gs], kc_ref[0, :, gs]], axis=0)
        vcat = jnp.concatenate([vp_ref[0, :, gs], vo_ref[0, :, gs], vn_ref[0, :, gs], vc_ref[0, :, gs]], axis=0)
        pairs = range(g * group // 2, (g + 1) * group // 2)
        qhs, sinks = [], []
        for pair in pairs:
            qp = q_ref[0, :, pair * LANES:(pair + 1) * LANES]
            zero = jnp.zeros_like(qp)
            qhs += [jnp.where(lo, qp, zero), jnp.where(lo, zero, qp)]
            sinks += [sink_ref[2 * pair], sink_ref[2 * pair + 1]]
        ss = [jnp.where(valid, _dot_nt(qh, kcat), NEG_INF) for qh in qhs]
        ms = [jnp.maximum(jnp.max(s, axis=-1, keepdims=True), sink) for s, sink in zip(ss, sinks)]
        es = [jnp.exp(s - m) for s, m in zip(ss, ms)]
        dens = [jnp.sum(e, axis=-1, keepdims=True) + jnp.exp(sink - m) for e, m, sink in zip(es, ms, sinks)]
        ps = [(e * (1.0 / den)).astype(BF16) for e, den in zip(es, dens)]
        accs = [_dot(p, vcat) for p in ps]
        for n, pair in enumerate(pairs):
            o_ref[0, :, pair * LANES:(pair + 1) * LANES] = jnp.where(lo, accs[2 * n], accs[2 * n + 1]).astype(o_ref.dtype)


def _swa_attention(q, kk, vv, sink, ctx_len):
    bsz, t, dq = q.shape
    dk = kk.shape[2]
    blk = ATT_BLOCK
    nb = t // blk
    ctx_blocks = ctx_len // blk
    n_lat_blocks = nb - ctx_blocks

    def prev_map(i, j, s):
        return (i, jnp.clip(j - 1, ctx_blocks, nb - 1), 0)

    def own_map(i, j, s):
        return (i, j, 0)

    def next_map(i, j, s):
        return (i, jnp.clip(j + 1, ctx_blocks, nb - 1), 0)

    def ctx_map(i, j, s):
        return (i, 0, 0)

    kv_specs = [pl.BlockSpec((1, blk, dk), prev_map), pl.BlockSpec((1, blk, dk), own_map),
                pl.BlockSpec((1, blk, dk), next_map), pl.BlockSpec((1, ctx_len, dk), ctx_map)]
    return pl.pallas_call(
        functools.partial(_swa_kernel, n_lat_blocks=n_lat_blocks, ctx_blocks=ctx_blocks),
        grid_spec=pltpu.PrefetchScalarGridSpec(
            num_scalar_prefetch=1, grid=(bsz, nb),
            in_specs=[pl.BlockSpec((1, blk, dq), own_map)] + kv_specs + kv_specs,
            out_specs=pl.BlockSpec((1, blk, dq), own_map)),
        out_shape=jax.ShapeDtypeStruct((bsz, t, dq), BF16),
        compiler_params=_cparams("parallel", "parallel"),
    )(sink, q, kk, kk, kk, kk, vv, vv, vv, vv)


def _diff_kernel(lam_ref, q_ref, k_ref, v_ref, g_ref, o_ref, *, ctx_len, lam_init):
    j = pl.program_id(2)
    blk = DIFF_Q_BLOCK
    ctx_blocks = ctx_len // blk
    lp = lam_ref[...]
    lam = (jnp.exp(jnp.sum(lp[0:1, :] * lp[1:2, :], axis=-1, keepdims=True))
           - jnp.exp(jnp.sum(lp[2:3, :] * lp[3:4, :], axis=-1, keepdims=True)) + lam_init)
    lane = lax.broadcasted_iota(jnp.int32, (blk, LANES), 1)
    lo = lane < DIFF_HEAD_DIM
    qp = q_ref[0]
    zero = jnp.zeros_like(qp)

    def attend(n_keys):
        k = k_ref[0, 0:n_keys, :]
        v = v_ref[0, 0:n_keys, :]
        qhs = [jnp.where(lo, qp, zero), jnp.where(lo, zero, qp)]
        ss = [_dot_nt(qh, k) for qh in qhs]
        es = [jnp.exp2(s - jnp.max(s, axis=-1, keepdims=True)) for s in ss]
        pvs = [_dot(e.astype(BF16), v) for e in es]
        invs = [1.0 / pv[:, LANES:LANES + 1] for pv in pvs]
        o = pvs[0][:, :LANES] * invs[0] - (lam * invs[1]) * pvs[1][:, :LANES]
        o = o * lax.rsqrt(jnp.mean(o * o, axis=-1, keepdims=True) + RMS_EPS) * g_ref[...]
        o_ref[0] = (o * (1.0 - lam_init)).astype(o_ref.dtype)

    @pl.when(j < ctx_blocks)
    def _():
        attend(ctx_len)

    @pl.when(j >= ctx_blocks)
    def _():
        attend(k_ref.shape[1])


def _diff_attention(q, k, v, lam_p, subln_g, ctx_len, lam_init):
    bsz, t, dq = q.shape
    blk = DIFF_Q_BLOCK
    nh = dq // LANES
    qmap = lambda i, h, j: (i, j, h)
    kmap = lambda i, h, j: (i, 0, h)
    const2 = lambda i, h, j: (0, 0)
    return pl.pallas_call(
        functools.partial(_diff_kernel, ctx_len=ctx_len, lam_init=lam_init),
        grid=(bsz, nh, t // blk),
        in_specs=[pl.BlockSpec((8, DIFF_HEAD_DIM), const2),
                  pl.BlockSpec((1, blk, LANES), qmap),
                  pl.BlockSpec((1, t, LANES), kmap),
                  pl.BlockSpec((1, t, 2 * LANES), kmap),
                  pl.BlockSpec((1, LANES), const2)],
        out_specs=pl.BlockSpec((1, blk, LANES), qmap),
        out_shape=jax.ShapeDtypeStruct((bsz, t, dq), BF16),
        compiler_params=_cparams("parallel", "parallel", "parallel"),
    )(jnp.pad(lam_p, ((0, 4), (0, 0))), q, k, v, subln_g.reshape(1, LANES))


def _delta_prep_kernel(x_ref, xp_ref, xn_ref, cw_ref, ba_ref, alog_ref, dtb_ref,
                       q_ref, k_ref, v_ref, gb_ref, ext_ref, *, ctx_blocks, n_blocks, nqk):
    j = pl.program_id(1)
    has_prev, has_next = _seg_edges(j, ctx_blocks, n_blocks)
    hd = DN_HEAD_DIM
    tm = x_ref.shape[1]
    for c0 in range(0, x_ref.shape[2], COL_CHUNK):
        cs = slice(c0, c0 + COL_CHUNK)
        conv = _centred_conv(ext_ref, x_ref[0, :, cs].astype(F32), xp_ref[0, :, cs].astype(F32),
                             xn_ref[0, :, cs].astype(F32), cw_ref.at[:, cs], DN_CONV, has_prev, has_next)
        a = _silu(conv)
        for h0 in range(0, COL_CHUNK, hd):
            col = c0 + h0
            ah = a[:, h0:h0 + hd]
            if col < 2 * nqk:
                ah = ah * lax.rsqrt(jnp.sum(ah * ah, axis=-1, keepdims=True) + RMS_EPS)
                if col < nqk:
                    q_ref[0, :, col:col + hd] = (ah * (hd ** -0.5)).astype(q_ref.dtype)
                else:
                    k_ref[0, :, col - nqk:col - nqk + hd] = ah.astype(k_ref.dtype)
            else:
                v_ref[0, :, col - 2 * nqk:col - 2 * nqk + hd] = ah.astype(v_ref.dtype)
    ba = ba_ref[0]
    lane = lax.broadcasted_iota(jnp.int32, ba.shape, 1)
    is_beta = (lane % 32) < DN_V_HEADS
    z = ba + dtb_ref[...]
    softplus = jnp.maximum(z, 0.0) + jnp.log(1.0 + jnp.exp(-jnp.abs(z)))
    g = -jnp.exp(alog_ref[...]) * softplus
    r = lax.broadcasted_iota(jnp.int32, (tm, tm), 0)
    c = lax.broadcasted_iota(jnp.int32, (tm, tm), 1)
    same = (r // DN_CHUNK) == (c // DN_CHUNK)
    tri_f = jnp.where(jnp.logical_and(same, c <= r), 1.0, 0.0).astype(BF16)
    tri_b = jnp.where(jnp.logical_and(same, c >= r), 1.0, 0.0).astype(BF16)
    parts = _split3(g)
    cum_f = _dot(tri_f, parts[0]) + _dot(tri_f, parts[1]) + _dot(tri_f, parts[2])
    cum_b = _dot(tri_b, parts[0]) + _dot(tri_b, parts[1]) + _dot(tri_b, parts[2])
    gcum = jnp.where(lane < 32, cum_f, cum_b)
    gb_ref[0] = jnp.where(is_beta, jax.nn.sigmoid(ba), gcum)


def _delta_prep(qkv_pre, conv_w, ba, alog_l, dtb_l, ctx_len, nqk, nv):
    bsz, t, nc = qkv_pre.shape
    tm = ROW_TILE
    nb = t // tm
    ctx_blocks = ctx_len // tm
    hb = tm // HALO_ROWS
    last_halo = t // HALO_ROWS - 1
    row = lambda i, j: (i, j, 0)
    const2 = lambda i, j: (0, 0)
    return pl.pallas_call(
        functools.partial(_delta_prep_kernel, ctx_blocks=ctx_blocks, n_blocks=nb, nqk=nqk),
        grid=(bsz, nb),
        in_specs=[pl.BlockSpec((1, tm, nc), row),
                  pl.BlockSpec((1, HALO_ROWS, nc), lambda i, j: (i, jnp.maximum(j * hb - 1, 0), 0)),
                  pl.BlockSpec((1, HALO_ROWS, nc), lambda i, j: (i, jnp.minimum((j + 1) * hb, last_halo), 0)),
                  pl.BlockSpec((8, nc), const2),
                  pl.BlockSpec((1, tm, LANES), row),
                  pl.BlockSpec((1, LANES), const2),
                  pl.BlockSpec((1, LANES), const2)],
        out_specs=[pl.BlockSpec((1, tm, nqk), row), pl.BlockSpec((1, tm, nqk), row),
                   pl.BlockSpec((1, tm, nv), row), pl.BlockSpec((1, tm, LANES), row)],
        out_shape=[jax.ShapeDtypeStruct((bsz, t, nqk), BF16), jax.ShapeDtypeStruct((bsz, t, nqk), BF16),
                   jax.ShapeDtypeStruct((bsz, t, nv), BF16), jax.ShapeDtypeStruct((bsz, t, LANES), F32)],
        scratch_shapes=[pltpu.VMEM((tm + 2 * HALO_ROWS, COL_CHUNK), F32)],
        compiler_params=_cparams("parallel", "parallel"),
    )(qkv_pre, qkv_pre, qkv_pre, conv_w, ba, alog_l, dtb_l)


DN_INV_BLOCK = 16


def _unit_triangular_inverses(lows, eye, r, c):
    n = lows[0].shape[0]
    b = DN_INV_BLOCK
    same = (r // b) == (c // b)
    pws = [jnp.where(same, low, 0.0) for low in lows]
    xs = [eye - pw for pw in pws]
    for _ in range(int(math.log2(b)) - 1):
        pbs = [pw.astype(BF16) for pw in pws]
        pws = [_dot(pb, pb) for pb in pbs]
        xs = [x + _dot(x.astype(BF16), pw.astype(BF16)) for x, pw in zip(xs, pws)]
    while b < n:
        pair = jnp.logical_and((r // (2 * b)) == (c // (2 * b)), (r // b) != (c // b))
        xbs = [x.astype(BF16) for x in xs]
        ys = [_dot(jnp.where(pair, low, 0.0).astype(BF16), xb).astype(BF16) for low, xb in zip(lows, xbs)]
        xs = [x - _dot(xb, y) for x, xb, y in zip(xs, xbs, ys)]
        b *= 2
    return xs


DN_SCAN_HEADS = 16


def _delta_chunk_kernel(q_ref, k_ref, v_ref, gb_ref, gt_ref, u_ref, w_ref, qg_ref, kd_ref, a_ref, *, rep):
    hq = pl.program_id(1)
    ck = DN_CHUNK
    hd = DN_HEAD_DIM
    r = lax.broadcasted_iota(jnp.int32, (ck, ck), 0)
    c = lax.broadcasted_iota(jnp.int32, (ck, ck), 1)
    eye = jnp.where(r == c, 1.0, 0.0)
    lane = lax.broadcasted_iota(jnp.int32, (ck, LANES), 1)
    lows, rhss, dests = [], [], []
    for sub in range(q_ref.shape[1] // ck):
        rows = slice(sub * ck, (sub + 1) * ck)
        q = q_ref[0, rows, :]
        k = k_ref[0, rows, :]
        kk = _dot_nt(k, k)
        qk = _dot_nt(q, k)
        gb = gb_ref[0, rows, :]
        qf = q.astype(F32)
        kf = k.astype(F32)
        for d in range(2):
            incl = (r >= c) if d == 0 else (r <= c)
            strict = (r > c) if d == 0 else (r < c)
            for jh in range(rep):
                head = hq * rep + jh
                cols = slice(jh * hd, (jh + 1) * hd)
                bcol = jnp.sum(jnp.where(lane == d * 32 + head, gb, 0.0), axis=-1, keepdims=True)
                gcol = jnp.sum(jnp.where(lane == d * 32 + DN_V_HEADS + head, gb, 0.0), axis=-1, keepdims=True)
                grow = gt_ref[d, 0, 0, jh:jh + 1, rows]
                glast = grow[:, ck - 1:ck] if d == 0 else grow[:, 0:1]
                decay = jnp.where(incl, jnp.exp(jnp.where(incl, gcol - grow, 0.0)), 0.0)
                lows.append(jnp.where(strict, kk * decay, 0.0) * bcol)
                v = v_ref[0, rows, cols].astype(F32)
                eg = jnp.exp(gcol)
                rhss.append(jnp.concatenate([v * bcol, kf * (bcol * eg)], axis=1).astype(BF16))
                dests.append((d, rows, cols))
                qg_ref[d, 0, rows, cols] = (qf * eg).astype(qg_ref.dtype)
                kd_ref[d, 0, rows, cols] = (kf * jnp.exp(glast - gcol)).astype(kd_ref.dtype)
                a_ref[d, 0, rows, cols] = jnp.where(incl, qk * decay, 0.0).astype(a_ref.dtype)
    tinvs = _unit_triangular_inverses(lows, eye, r, c)
    uws = [_dot(tinv.astype(BF16), rhs) for tinv, rhs in zip(tinvs, rhss)]
    for uw, (d, rows, cols) in zip(uws, dests):
        u_ref[d, 0, rows, cols] = uw[:, :hd].astype(u_ref.dtype)
        w_ref[d, 0, rows, cols] = uw[:, hd:].astype(w_ref.dtype)


def _delta_chunks(q, k, v, gb, gt_a):
    bsz, t, nqk = q.shape
    nv = v.shape[2]
    hd = DN_HEAD_DIM
    nh = nqk // hd
    rep = nv // nqk
    tm = ROW_TILE
    qmap = lambda i, h, j: (i, j, h)
    omap = lambda i, h, j: (0, i, j, h)
    oshape = lambda dt: jax.ShapeDtypeStruct((2, bsz, t, nv), dt)
    ospec = pl.BlockSpec((2, 1, tm, rep * hd), omap)
    return pl.pallas_call(
        functools.partial(_delta_chunk_kernel, rep=rep),
        grid=(bsz, nh, t // tm),
        in_specs=[pl.BlockSpec((1, tm, hd), qmap), pl.BlockSpec((1, tm, hd), qmap),
                  pl.BlockSpec((1, tm, rep * hd), qmap),
                  pl.BlockSpec((1, tm, LANES), lambda i, h, j: (i, j, 0)),
                  pl.BlockSpec((2, 1, 1, 8, tm), lambda i, h, j: (0, i, h, 0, j))],
        out_specs=[ospec] * 5,
        out_shape=[oshape(BF16)] * 5,
        compiler_params=_cparams("parallel", "parallel", "parallel"),
    )(q, k, v, gb, gt_a)


def _scan_chunk_index(d, i, ctx_chunks, n_chunks):
    back = jnp.where(i < ctx_chunks, ctx_chunks - 1 - i, n_chunks - 1 - (i - ctx_chunks))
    return jnp.where(d == 0, i, back)


def _delta_scan_kernel(u_ref, w_ref, qg_ref, kd_ref, a_ref, gt_ref, o_ref, s_ref, *, n_heads):
    d = pl.program_id(0)
    i = pl.program_id(3)
    ck = DN_CHUNK
    hd = DN_HEAD_DIM

    @pl.when(i == 0)
    def _():
        s_ref[...] = jnp.zeros_like(s_ref)

    heads = range(n_heads)
    cols = [slice(jh * hd, (jh + 1) * hd) for jh in heads]
    grows = gt_ref[0, 0, 0]
    glast = jnp.where(d == 0, grows[:, ck - 1:ck], grows[:, 0:1])
    carry = jnp.exp(glast)
    ss = [s_ref[jh] for jh in heads]
    wss = [_dot(jnp.concatenate([w_ref[0, 0, :, cs], qg_ref[0, 0, :, cs]], axis=0), s.astype(BF16))
           for cs, s in zip(cols, ss)]
    v_news = [(u_ref[0, 0, :, cs] - ws[:ck]).astype(BF16) for cs, ws in zip(cols, wss)]
    outs = [ws[ck:] + _dot(a_ref[0, 0, :, cs], vn) for cs, ws, vn in zip(cols, wss, v_news)]
    upds = [_dot_tn(kd_ref[0, 0, :, cs], vn) for cs, vn in zip(cols, v_news)]
    for jh in heads:
        o_ref[0, 0, :, cols[jh]] = outs[jh].astype(o_ref.dtype)
        s_ref[jh] = ss[jh] * carry[jh:jh + 1, :] + upds[jh]


def _delta_scan(u, w, qg, kd, a, gt_b, ctx_len):
    _, bsz, t, nv = u.shape
    ck = DN_CHUNK
    hd = DN_HEAD_DIM
    nh = DN_SCAN_HEADS
    groups = nv // (nh * hd)
    n_chunks = t // ck
    ctx_chunks = ctx_len // ck
    cmap = lambda d, b, g, i: (d, b, _scan_chunk_index(d, i, ctx_chunks, n_chunks), g)
    spec = pl.BlockSpec((1, 1, ck, nh * hd), cmap)
    return pl.pallas_call(
        functools.partial(_delta_scan_kernel, n_heads=nh),
        grid=(2, bsz, groups, n_chunks),
        in_specs=[spec] * 5 + [pl.BlockSpec((1, 1, 1, nh, ck),
                                            lambda d, b, g, i: (d, b, g, 0, _scan_chunk_index(d, i, ctx_chunks, n_chunks)))],
        out_specs=spec,
        out_shape=jax.ShapeDtypeStruct((2, bsz, t, nv), BF16),
        scratch_shapes=[pltpu.VMEM((nh, hd, hd), F32)],
        compiler_params=_cparams("parallel", "parallel", "parallel", "arbitrary"),
    )(u, w, qg, kd, a, gt_b)


def _pack_bf16_pairs(a):
    n = a.shape[1] // 2
    bits = lax.bitcast_convert_type(a, jnp.uint32)
    return (bits[:, :n] & jnp.uint32(0xFFFF0000)) | (bits[:, n:] >> 16)


def _unpack_bf16_pairs(p):
    hi = lax.bitcast_convert_type(p & jnp.uint32(0xFFFF0000), F32)
    lo = lax.bitcast_convert_type(p << 16, F32)
    return jnp.concatenate([hi, lo], axis=1)


def _router_kernel(x_ref, mod_ref, whi_ref, wlo_ref, br_ref, ha_ref, hb_ref, idx_ref, wt_ref, cnt_ref, run_ref):
    first = jnp.logical_and(pl.program_id(0) == 0, pl.program_id(1) == 0)

    @pl.when(first)
    def _():
        run_ref[...] = jnp.zeros_like(run_ref)

    shift = mod_ref[0, 0, 3:4, :]
    scale = mod_ref[0, 0, 4:5, :]
    h = x_ref[0] * (1.0 + scale) + shift
    h_hi = h.astype(BF16)
    packed = _pack_bf16_pairs(h_hi.astype(F32))
    ha_ref[0] = packed[:, :SC_ROW_WORDS]
    hb_ref[0] = packed[:, SC_ROW_WORDS:]
    h_lo = (h - h_hi.astype(F32)).astype(BF16)
    w_hi = whi_ref[...]
    w_lo = wlo_ref[...]
    logits = _dot(h_hi, w_hi) + _dot(h_hi, w_lo) + _dot(h_lo, w_hi) + br_ref[...]
    lane = lax.broadcasted_iota(jnp.int32, logits.shape, 1).astype(F32)
    cur = jnp.where(lane < N_EXPERTS, logits, -jnp.inf)
    idx_out = jnp.zeros(logits.shape, F32)
    wt_out = jnp.zeros(logits.shape, F32)
    tops, sels = [], []
    hits = jnp.zeros(logits.shape, F32)
    for kk in range(TOP_K):
        m = jnp.max(cur, axis=-1, keepdims=True)
        sel = jnp.min(jnp.where(cur == m, lane, float(LANES)), axis=-1, keepdims=True)
        tops.append(m)
        sels.append(sel)
        idx_out = jnp.where(lane == kk, sel, idx_out)
        hits = jnp.where(lane == sel, 1.0, hits)
        cur = jnp.where(lane == sel, -jnp.inf, cur)
    es = [jnp.exp(m - tops[0]) for m in tops]
    den = es[0] + es[1] + es[2] + es[3]
    for kk in range(TOP_K):
        wt_out = jnp.where(lane == kk, es[kk] / den, wt_out)
    tm = logits.shape[0]
    r = lax.broadcasted_iota(jnp.int32, (tm, tm), 0)
    c = lax.broadcasted_iota(jnp.int32, (tm, tm), 1)
    before = _dot(jnp.where(c < r, 1.0, 0.0).astype(BF16), hits.astype(BF16)) + run_ref[...]
    for kk in range(TOP_K):
        rank = jnp.sum(jnp.where(lane == sels[kk], before, 0.0), axis=-1, keepdims=True)
        idx_out = jnp.where(lane == TOP_K + kk, rank, idx_out)
    total = run_ref[...] + jnp.sum(hits, axis=0, keepdims=True)
    run_ref[...] = total
    cnt_ref[...] = jnp.broadcast_to(total, cnt_ref.shape)
    idx_ref[0] = idx_out.astype(jnp.int32)
    wt_ref[0] = wt_out


def _router(x, modtab, w_hi, w_lo, b_r, ctx_len, first_block):
    bsz, t, d = x.shape
    tm = ROW_TILE
    ctx_blocks = ctx_len // tm
    nb = t // tm - first_block
    row_in = lambda i, j: (i, j + first_block, 0)
    row = lambda i, j: (i, j, 0)
    const2 = lambda i, j: (0, 0)
    tout = nb * tm
    return pl.pallas_call(
        _router_kernel,
        grid=(bsz, nb),
        in_specs=[pl.BlockSpec((1, tm, d), row_in),
                  pl.BlockSpec((1, 1, 8, d), lambda i, j: (i, _seg_of_block(j + first_block, ctx_blocks), 0, 0)),
                  pl.BlockSpec((d, LANES), const2), pl.BlockSpec((d, LANES), const2),
                  pl.BlockSpec((1, LANES), const2)],
        out_specs=[pl.BlockSpec((1, tm, SC_ROW_WORDS), row), pl.BlockSpec((1, tm, SC_ROW_WORDS), row),
                   pl.BlockSpec((1, tm, LANES), row),
                   pl.BlockSpec((1, tm, LANES), row), pl.BlockSpec((8, LANES), const2)],
        out_shape=[jax.ShapeDtypeStruct((bsz, tout, SC_ROW_WORDS), jnp.uint32),
                   jax.ShapeDtypeStruct((bsz, tout, SC_ROW_WORDS), jnp.uint32),
                   jax.ShapeDtypeStruct((bsz, tout, LANES), jnp.int32),
                   jax.ShapeDtypeStruct((bsz, tout, LANES), F32),
                   jax.ShapeDtypeStruct((8, LANES), F32)],
        scratch_shapes=[pltpu.VMEM((1, LANES), F32)],
        compiler_params=_cparams("arbitrary", "arbitrary"),
    )(x, modtab, w_hi, w_lo, b_r)


def _expert_kernel(be_ref, nused_ref, xa_ref, xb_ref, w1_ref, b1_ref, w2_ref, b2_ref, ya_ref, yb_ref,
                   w1b_ref, w2b_ref):
    i = pl.program_id(0)
    dff = w2_ref.shape[1]
    prev = be_ref[jnp.maximum(i - 1, 0)]
    new_expert = jnp.logical_or(i == 0, be_ref[i] != prev)

    @pl.when(jnp.logical_and(new_expert, i < nused_ref[0]))
    def _():
        for r0 in range(0, w1_ref.shape[1], LANES):
            w1b_ref[r0:r0 + LANES, :] = w1_ref[0, r0:r0 + LANES, :].astype(BF16)
        for r0 in range(0, dff, LANES):
            w2b_ref[r0:r0 + LANES, :] = w2_ref[0, r0:r0 + LANES, :].astype(BF16)

    @pl.when(i < nused_ref[0])
    def _():
        x = _unpack_bf16_pairs(jnp.concatenate([xa_ref[...], xb_ref[...]], axis=1)).astype(BF16)
        chunks = range(0, dff, COL_CHUNK)
        gates = [_dot(x, w1b_ref[:, c0:c0 + COL_CHUNK]) + b1_ref[0, :, c0:c0 + COL_CHUNK] for c0 in chunks]
        ups = [_dot(x, w1b_ref[:, dff + c0:dff + c0 + COL_CHUNK]) + b1_ref[0, :, dff + c0:dff + c0 + COL_CHUNK]
               for c0 in chunks]
        gates = [jnp.minimum(gate, SWIGLU_LIMIT) for gate in gates]
        ups = [jnp.clip(up, -SWIGLU_LIMIT, SWIGLU_LIMIT) for up in ups]
        acts = [((up + 1.0) * gate * jax.nn.sigmoid(SWIGLU_ALPHA * gate)).astype(BF16) for gate, up in zip(gates, ups)]
        parts = [_dot(act, w2b_ref[c0:c0 + COL_CHUNK, :]) for act, c0 in zip(acts, chunks)]
        acc = parts[0]
        for part in parts[1:]:
            acc = acc + part
        y = (acc + b2_ref[0]).astype(BF16).astype(F32)
        packed = _pack_bf16_pairs(y)
        ya_ref[...] = packed[:, :SC_ROW_WORDS]
        yb_ref[...] = packed[:, SC_ROW_WORDS:]

    @pl.when(i >= nused_ref[0])
    def _():
        ya_ref[...] = jnp.zeros_like(ya_ref)
        yb_ref[...] = jnp.zeros_like(yb_ref)


def _experts(xa, xb, block_e, n_used, w1, b1, w2, b2, layer):
    n_rows, dq = xa.shape
    d = 4 * dq
    tm = EXPERT_TILE
    depth, ne, _, dff2 = w1.shape
    dff = dff2 // 2
    w1 = w1.reshape(depth * ne, d, dff2)
    w2 = w2.reshape(depth * ne, dff, d)
    b1 = b1.reshape(depth * ne, 1, dff2)
    b2 = b2.reshape(depth * ne, 1, d)
    emap = lambda i, be, nu: (layer * ne + be[i], 0, 0)
    return pl.pallas_call(
        _expert_kernel,
        grid_spec=pltpu.PrefetchScalarGridSpec(
            num_scalar_prefetch=2, grid=(n_rows // tm,),
            in_specs=[pl.BlockSpec((tm, dq), lambda i, be, nu: (i, 0)),
                      pl.BlockSpec((tm, dq), lambda i, be, nu: (i, 0)),
                      pl.BlockSpec((1, d, dff2), emap),
                      pl.BlockSpec((1, 1, dff2), emap),
                      pl.BlockSpec((1, dff, d), emap),
                      pl.BlockSpec((1, 1, d), emap)],
            out_specs=[pl.BlockSpec((tm, dq), lambda i, be, nu: (i, 0))] * 2,
            scratch_shapes=[pltpu.VMEM((d, dff2), BF16), pltpu.VMEM((dff, d), BF16)]),
        out_shape=[jax.ShapeDtypeStruct((n_rows, dq), jnp.uint32)] * 2,
        compiler_params=_cparams("arbitrary"),
    )(block_e, n_used, xa, xb, w1, b1, w2, b2)


def _combine_kernel(x_ref, mod_ref, za_ref, zb_ref, wt_ref, g_ref, b_ref, out_ref, *, alpha):
    wt = wt_ref[0]
    f = None
    for kk in range(TOP_K):
        term = _unpack_bf16_pairs(jnp.concatenate([za_ref[kk, 0], zb_ref[kk, 0]], axis=1)) * wt[:, kk:kk + 1]
        f = term if f is None else f + term
    out_ref[0] = _resid_ln(x_ref[0], f, mod_ref[0, 0, 5:6, :], g_ref[...], b_ref[...], alpha)


def _combine(x, modtab, za, zb, wt, ln_g, ln_b, alpha, ctx_len, first_block):
    bsz, t, d = x.shape
    tm = ROW_TILE
    ctx_blocks = ctx_len // tm
    nb = t // tm - first_block
    row_in = lambda i, j: (i, j + first_block, 0)
    row = lambda i, j: (i, j, 0)
    const2 = lambda i, j: (0, 0)
    return pl.pallas_call(
        functools.partial(_combine_kernel, alpha=alpha),
        grid=(bsz, nb),
        in_specs=[pl.BlockSpec((1, tm, d), row_in),
                  pl.BlockSpec((1, 1, 8, d), lambda i, j: (i, _seg_of_block(j + first_block, ctx_blocks), 0, 0)),
                  pl.BlockSpec((TOP_K, 1, tm, SC_ROW_WORDS), lambda i, j: (0, i, j, 0)),
                  pl.BlockSpec((TOP_K, 1, tm, SC_ROW_WORDS), lambda i, j: (0, i, j, 0)),
                  pl.BlockSpec((1, tm, LANES), row),
                  pl.BlockSpec((1, d), const2), pl.BlockSpec((1, d), const2)],
        out_specs=pl.BlockSpec((1, tm, d), row),
        out_shape=jax.ShapeDtypeStruct((bsz, nb * tm, d), F32),
        compiler_params=_cparams("parallel", "parallel"),
    )(x, modtab, za, zb, wt, ln_g, ln_b)


def _moe_dispatch(x, modtab, w_r, b_r, ctx_len, latent_only):
    bsz, t, d = x.shape
    first_block = ctx_len // ROW_TILE if latent_only else 0
    wr_pad = jnp.pad(w_r, ((0, 0), (0, LANES - N_EXPERTS)))
    br_pad = jnp.pad(b_r, (0, LANES - N_EXPERTS)).reshape(1, LANES)
    wr_hi = wr_pad.astype(BF16)
    wr_lo = (wr_pad - wr_hi.astype(F32)).astype(BF16)
    ha, hb, idx, wt, cnt = _router(x, modtab, wr_hi, wr_lo, br_pad, ctx_len, first_block)
    tr = ha.shape[1]
    dq = ha.shape[2]
    n_tok = bsz * tr
    n_assign = n_tok * TOP_K
    tm = EXPERT_TILE
    experts = jnp.arange(N_EXPERTS, dtype=jnp.int32)
    counts = cnt[0, :N_EXPERTS].astype(jnp.int32)
    padded = (counts + tm - 1) // tm * tm
    pends = jnp.cumsum(padded)
    pstarts = pends - padded
    e_tk = idx[:, :, :TOP_K].reshape(n_tok, TOP_K)
    rank = idx[:, :, TOP_K:2 * TOP_K].reshape(n_tok, TOP_K)
    dest = jnp.sum(jnp.where(e_tk[:, :, None] == experts, pstarts, 0), axis=-1) + rank
    n_rows = (-(-n_assign // tm) + N_EXPERTS) * tm
    n_blocks = n_rows // tm
    starts = jnp.arange(n_blocks, dtype=jnp.int32) * tm
    block_e = jnp.minimum(jnp.sum((pends[None, :] <= starts[:, None]).astype(jnp.int32), axis=1), N_EXPERTS - 1)
    n_used = (pends[-1] // tm).astype(jnp.int32).reshape(1)
    dest_t = dest.T
    xa = _sc_scatter_rows(ha.reshape(n_tok, dq), dest_t, n_rows)
    xb = _sc_scatter_rows(hb.reshape(n_tok, dq), dest_t, n_rows)
    return dict(xa=xa, xb=xb, block_e=block_e, n_used=n_used, dest_t=dest_t, wt=wt, first_block=first_block,
                shape=(bsz, tr, dq))


def _moe_experts(st, expert_params):
    bsz, tr, dq = st["shape"]
    ya, yb = _experts(st["xa"], st["xb"], st["block_e"], st["n_used"], *expert_params)
    flat = st["dest_t"].reshape(1, bsz * tr * TOP_K)
    za = _sc_gather_rows(ya, flat).reshape(TOP_K, bsz, tr, dq)
    zb = _sc_gather_rows(yb, flat).reshape(TOP_K, bsz, tr, dq)
    return za, zb


def _moe_combine(x, modtab, st, z, ln_g, ln_b, alpha, ctx_len):
    return _combine(x, modtab, z[0], z[1], st["wt"], ln_g, ln_b, alpha, ctx_len, st["first_block"])


SC_WINDOW = 128
SC_ROW_WORDS = 256


def _sc_mesh():
    return plsc.VectorSubcoreMesh(core_axis_name="c", subcore_axis_name="s")


def _sc_scatter_rows(src, idx, n_out):
    n_src, width = src.shape
    n_k = idx.shape[0]
    per_k = n_src // SC_WINDOW

    @functools.partial(pl.kernel, out_type=jax.ShapeDtypeStruct((n_out, width), src.dtype), mesh=_sc_mesh())
    def scatter(x_hbm, i_hbm, o_hbm):
        def body(x_vmem, i_vmem):
            pltpu.sync_copy(x_vmem, o_hbm.at[i_vmem.at[0]])

        pltpu.emit_pipeline(
            body, grid=(n_k * per_k,),
            in_specs=[pl.BlockSpec((SC_WINDOW, width), lambda i: (i % per_k, 0)),
                      pl.BlockSpec((1, SC_WINDOW), lambda i: (i // per_k, i % per_k))],
            out_specs=[],
            core_axis_name=("c", "s"),
            dimension_semantics=(pltpu.PARALLEL,),
        )(x_hbm, i_hbm)

    return scatter(src, idx)


def _sc_gather_rows(table, idx):
    width = table.shape[1]
    n_idx = idx.shape[1]

    @functools.partial(pl.kernel, out_type=jax.ShapeDtypeStruct((n_idx, width), table.dtype), mesh=_sc_mesh())
    def gather(x_hbm, i_hbm, o_hbm):
        def body(i_vmem, o_vmem):
            pltpu.sync_copy(x_hbm.at[i_vmem.at[0]], o_vmem)

        pltpu.emit_pipeline(
            body, grid=(n_idx // SC_WINDOW,),
            in_specs=[pl.BlockSpec((1, SC_WINDOW), lambda i: (0, i))],
            out_specs=[pl.BlockSpec((SC_WINDOW, width), lambda i: (i, 0))],
            core_axis_name=("c", "s"),
            dimension_semantics=(pltpu.PARALLEL,),
        )(i_hbm, o_hbm)

    return gather(table, idx)


def _rot_cols(w, head_dim):
    lead = w.shape[:-1]
    q = head_dim // 4
    wr = w.reshape(lead + (-1, 4, q))
    out = jnp.stack([-wr[..., 1, :], wr[..., 0, :], -wr[..., 3, :], wr[..., 2, :]], axis=-2)
    return out.reshape(w.shape)


def _dup_heads(w, head_dim):
    lead = w.shape[:-1]
    wr = w.reshape(lead + (-1, 1, head_dim))
    return jnp.concatenate([wr, wr], axis=-2).reshape(lead + (-1,))


def _rope_tables(n_lat, ctx_len, head_dim):
    rows = n_lat // GRID_W
    row = jnp.repeat(jnp.arange(rows, dtype=F32), GRID_W)
    col = jnp.tile(jnp.arange(GRID_W, dtype=F32), rows)
    half = head_dim // 2
    inv = ROPE_BASE ** (-jnp.arange(0, half, 2, dtype=F32) / half)
    ar = row[:, None] * inv
    ac = col[:, None] * inv
    ang = jnp.concatenate([ar, ar, ac, ac], axis=-1)
    cos = jnp.concatenate([jnp.ones((ctx_len, head_dim), F32), jnp.cos(ang)], axis=0)
    sin = jnp.concatenate([jnp.zeros((ctx_len, head_dim), F32), jnp.sin(ang)], axis=0)
    reps = LANES // head_dim
    return jnp.tile(cos, (1, reps)), jnp.tile(sin, (1, reps))


def _conv_mixer(x, modtab, w_in, w_conv, w_out, ln_g, ln_b, alpha, ctx_len):
    d = x.shape[2]
    plan = [("plain", 0, 0, d, 0, None, 1.0), ("mul", 1, 0, d, d, 2 * d, 1.0)]
    bg, p = _project(x, modtab, w_in.astype(BF16), jnp.zeros((1, 3 * d), F32), plan, (d, d), (BF16, BF16), ctx_len)
    cw = jnp.pad(w_conv, ((0, 8 - SC_WIDTH), (0, 0)))
    return _mixer_out("conv", x, modtab, (bg, p), w_out.astype(BF16), jnp.zeros((1, d), F32), ln_g, ln_b, alpha,
                      ctx_len, extra=cw)


def _swa_mixer(x, modtab, w_qkv, b_qkv, sink, w_o, b_o, ln_g, ln_b, alpha, ctx_len, rope):
    d = x.shape[2]
    nq = SWA_HEADS * SWA_HEAD_DIM
    nkv = SWA_KV_HEADS * SWA_HEAD_DIM
    hd = SWA_HEAD_DIM

    def arrange(a):
        q, k, v = a[..., :nq], a[..., nq:nq + nkv], a[..., nq + nkv:]
        kk = _dup_heads(k, hd)
        return jnp.concatenate([q, _rot_cols(q, hd), kk, _rot_cols(kk, hd), _dup_heads(v, hd)], axis=-1)

    w = arrange(w_qkv).astype(BF16)
    b = arrange(b_qkv.reshape(1, -1))
    plan = [("rope", 0, 0, nq, 0, nq, SWA_HEAD_DIM ** -0.5),
            ("rope", 1, 0, 2 * nkv, 2 * nq, 2 * nq + 2 * nkv, 1.0),
            ("plain", 2, 0, 2 * nkv, 2 * nq + 4 * nkv, None, 1.0)]
    q, kk, vv = _project(x, modtab, w, b, plan, (nq, 2 * nkv, 2 * nkv), (BF16, BF16, BF16), ctx_len, rope=rope)
    o = _swa_attention(q, kk, vv, sink.astype(F32), ctx_len)
    return _mixer_out("plain", x, modtab, (o,), w_o.astype(BF16), b_o.reshape(1, d), ln_g, ln_b, alpha, ctx_len)


def _diff_mixer(x, modtab, w_qkv, lam_p, subln_g, w_o, lam_init, ln_g, ln_b, alpha, ctx_len, rope):
    d = x.shape[2]
    hd = DIFF_HEAD_DIM
    wq, wk, wv = w_qkv[:, :d], w_qkv[:, d:2 * d], w_qkv[:, 2 * d:]
    w = jnp.concatenate([wq, _rot_cols(wq, hd), wk, _rot_cols(wk, hd), wv], axis=-1).astype(BF16)
    plan = [("rope", 0, 0, d, 0, d, DIFF_HEAD_DIM ** -0.5 * math.log2(math.e)),
            ("rope", 1, 0, d, 2 * d, 3 * d, 1.0)]
    for h in range(DIFF_HEADS):
        plan += [("plain", 2, 2 * h * LANES, LANES, 4 * d + h * LANES, None, 1.0),
                 ("ones", 2, (2 * h + 1) * LANES, LANES, None, None, 1.0)]
    q, k, v = _project(x, modtab, w, jnp.zeros((1, 5 * d), F32), plan, (d, d, 2 * d), (BF16, BF16, BF16), ctx_len,
                       rope=rope)
    o = _diff_attention(q, k, v, lam_p.astype(F32), subln_g.astype(F32), ctx_len, lam_init)
    return _mixer_out("plain", x, modtab, (o,), w_o.astype(BF16), jnp.zeros((1, d), F32), ln_g, ln_b, alpha, ctx_len)


def _delta_mixer(x, modtab, w_qkvz, w_ba, a_log, dt_bias, w_conv, norm_g, w_o, ln_g, ln_b, alpha, ctx_len):
    bsz, t, d = x.shape
    nqk = DN_QK_HEADS * DN_HEAD_DIM
    nv = DN_V_HEADS * DN_HEAD_DIM
    nc = 2 * nqk + nv
    nba = w_ba.shape[1]
    w = jnp.concatenate([w_qkvz, jnp.pad(w_ba, ((0, 0), (0, LANES - nba)))], axis=-1).astype(BF16)
    ntot = w.shape[1]
    plan = [("plain", 0, 0, nc, 0, None, 1.0), ("plain", 1, 0, nv, nc, None, 1.0),
            ("plain", 2, 0, LANES, nc + nv, None, 1.0)]
    qkv_pre, z, ba = _project(x, modtab, w, jnp.zeros((1, ntot), F32), plan, (nc, nv, LANES), (BF16, BF16, F32),
                              ctx_len)
    zeros16 = jnp.zeros((DN_V_HEADS,), F32)
    lanes_of = lambda p: jnp.pad(jnp.concatenate([zeros16, p[0], zeros16, p[1]]), (0, LANES - 4 * DN_V_HEADS))
    alog_l = lanes_of(a_log.astype(F32)).reshape(1, LANES)
    dtb_l = lanes_of(dt_bias.astype(F32)).reshape(1, LANES)
    cw = jnp.pad(w_conv, ((0, 8 - DN_CONV), (0, 0)))
    q, k, v, gb = _delta_prep(qkv_pre, cw, ba, alog_l, dtb_l, ctx_len, nqk, nv)
    rep = DN_V_HEADS // DN_QK_HEADS
    gcum = jnp.stack([gb[:, :, DN_V_HEADS:2 * DN_V_HEADS], gb[:, :, 3 * DN_V_HEADS:4 * DN_V_HEADS]])
    gt = jnp.transpose(gcum, (0, 1, 3, 2))
    gt_a = jnp.pad(gt.reshape(2, bsz, DN_QK_HEADS, rep, t), ((0, 0), (0, 0), (0, 0), (0, 8 - rep), (0, 0)))
    gt_b = gt.reshape(2, bsz, DN_V_HEADS // DN_SCAN_HEADS, DN_SCAN_HEADS, t)
    u, w_, qg, kd, a = _delta_chunks(q, k, v, gb, gt_a)
    o2 = _delta_scan(u, w_, qg, kd, a, gt_b, ctx_len)
    return _mixer_out("delta", x, modtab, (o2[0], o2[1], z), w_o.astype(BF16), jnp.zeros((1, d), F32),
                      ln_g, ln_b, alpha, ctx_len, extra=norm_g.astype(F32).reshape(1, DN_HEAD_DIM))


def kernel(x, c, ctx, c_ctx, mod_w, mod_b, ln1_g, ln1_b, ln2_g, ln2_b, router_w, router_b, exp_w1, exp_b1, exp_w2, exp_b2, conv_in_w, conv_w, conv_out_w, swa_qkv_w, swa_qkv_b, swa_sink, swa_out_w, swa_out_b, diff_qkv_w, diff_lambda, diff_subln_g, diff_out_w, delta_qkvz_w, delta_ba_w, delta_a_log, delta_dt_bias, delta_conv_w, delta_norm_g, delta_out_w):
    bsz, n_lat, d = x.shape
    ctx_len = ctx.shape[1]
    depth = mod_w.shape[0]
    alpha = (2 * depth) ** 0.25
    cc = jnp.zeros((16, d), F32).at[:bsz].set(c).at[bsz].set(c_ctx)
    mods = _mod_vectors(cc, mod_w, mod_b).reshape(depth, 16, 6, d)
    mod_lat = mods[:, :bsz]
    mod_ctx = jnp.broadcast_to(mods[:, bsz:bsz + 1], mod_lat.shape)
    modtabs = jnp.pad(jnp.stack([mod_ctx, mod_lat], axis=2), ((0, 0), (0, 0), (0, 0), (0, 2), (0, 0)))
    rope = _rope_tables(n_lat, ctx_len, SWA_HEAD_DIM)
    row1 = lambda a: a.reshape(1, d)
    groups = [slice(g * bsz // BATCH_GROUPS, (g + 1) * bsz // BATCH_GROUPS) for g in range(BATCH_GROUPS)]
    xs = [jnp.concatenate([ctx[g], x[g]], axis=1) for g in groups]
    for i in range(depth):
        last = i == depth - 1
        kind, j = i % N_MIXERS, i // N_MIXERS
        mts = [modtabs[i, g] for g in groups]
        g1, b1 = row1(ln1_g[i]), row1(ln1_b[i])
        g2, b2 = row1(ln2_g[i]), row1(ln2_b[i])

        def mixer(xg, mt):
            if kind == 0:
                return _conv_mixer(xg, mt, conv_in_w[j], conv_w[j], conv_out_w[j], g1, b1, alpha, ctx_len)
            if kind == 1:
                return _swa_mixer(xg, mt, swa_qkv_w[j], swa_qkv_b[j], swa_sink[j], swa_out_w[j], swa_out_b[j],
                                  g1, b1, alpha, ctx_len, rope)
            if kind == 2:
                lam_init = 0.8 - 0.6 * math.exp(-0.3 * i)
                return _diff_mixer(xg, mt, diff_qkv_w[j], diff_lambda[j], diff_subln_g[j], diff_out_w[j], lam_init,
                                   g1, b1, alpha, ctx_len, rope)
            return _delta_mixer(xg, mt, delta_qkvz_w[j], delta_ba_w[j], delta_a_log[j], delta_dt_bias[j],
                                delta_conv_w[j], delta_norm_g[j], delta_out_w[j], g1, b1, alpha, ctx_len)

        xs = [mixer(xg, mt) for xg, mt in zip(xs, mts)]
        sts = [_moe_dispatch(xg, mt, router_w[i], router_b[i], ctx_len, last) for xg, mt in zip(xs, mts)]
        zs = [_moe_experts(st, (exp_w1, exp_b1, exp_w2, exp_b2, i)) for st in sts]
        xs = [_moe_combine(xg, mt, st, z, g2, b2, alpha, ctx_len) for xg, mt, st, z in zip(xs, mts, sts, zs)]
    return jnp.concatenate(xs, axis=0)
```

```python
import functools
import math

import jax
import jax.numpy as jnp
from jax import lax
from jax.experimental import pallas as pl
from jax.experimental.pallas import tpu as pltpu
from jax.experimental.pallas import tpu_sc as plsc

F32 = jnp.float32
BF16 = jnp.bfloat16

GRID_W = 64
N_MIXERS = 4
LN_EPS = 1e-5
RMS_EPS = 1e-6
NEG_INF = -1e30
ROPE_BASE = 10000.0
SC_WIDTH = 3
SWA_HEADS = 16
SWA_KV_HEADS = 4
SWA_HEAD_DIM = 64
SWA_WINDOW = 128
DIFF_HEADS = 8
DIFF_HEAD_DIM = 64
DN_QK_HEADS = 8
DN_V_HEADS = 16
DN_HEAD_DIM = 128
DN_CONV = 5
N_EXPERTS = 32
TOP_K = 4
SWIGLU_LIMIT = 7.0
SWIGLU_ALPHA = 1.702

LANES = 128
HALO_ROWS = 16
VMEM_LIMIT = 56 * 1024 * 1024

ROW_TILE = 256
ATT_BLOCK = 128
DIFF_Q_BLOCK = 256
DIFF_HEADS_PER_STEP = 2
DN_CHUNK = 128
EXPERT_TILE = 512
COL_CHUNK = 512
BATCH_GROUPS = 1


def _cparams(*sem):
    return pltpu.CompilerParams(dimension_semantics=sem, vmem_limit_bytes=VMEM_LIMIT)


def _split3(x):
    hi = x.astype(BF16)
    r1 = x - hi.astype(F32)
    mid = r1.astype(BF16)
    lo = (r1 - mid.astype(F32)).astype(BF16)
    return hi, mid, lo


def _dot(a, b):
    return jnp.dot(a, b, preferred_element_type=F32)


def _dot_nt(a, b):
    return lax.dot_general(a, b, (((1,), (1,)), ((), ())), preferred_element_type=F32)


def _dot_tn(a, b):
    return lax.dot_general(a, b, (((0,), (0,)), ((), ())), preferred_element_type=F32)


def _silu(x):
    return x * jax.nn.sigmoid(x)


def _seg_of_block(j, ctx_blocks):
    return jnp.where(j < ctx_blocks, 0, 1)


def _mods_kernel(c_ref, w_ref, b_ref, o_ref):
    s = _silu(c_ref[...]).astype(BF16)
    o_ref[0] = _dot(s, w_ref[0].astype(BF16)) + b_ref[0]


def _mod_vectors(cc, mod_w, mod_b):
    depth, d, n = mod_w.shape
    rows = cc.shape[0]
    tn = 1536
    return pl.pallas_call(
        _mods_kernel,
        grid=(depth, n // tn),
        in_specs=[pl.BlockSpec((rows, d), lambda l, j: (0, 0)),
                  pl.BlockSpec((1, d, tn), lambda l, j: (l, 0, j)),
                  pl.BlockSpec((1, 1, tn), lambda l, j: (l, 0, j))],
        out_specs=pl.BlockSpec((1, rows, tn), lambda l, j: (l, 0, j)),
        out_shape=jax.ShapeDtypeStruct((depth, rows, n), F32),
        compiler_params=_cparams("parallel", "parallel"),
    )(cc, mod_w, mod_b.reshape(depth, 1, n))


def _proj_kernel(*refs, plan, n_out, use_rope):
    x_ref, mod_ref, w_ref, b_ref = refs[:4]
    pos = 4
    if use_rope:
        cos_ref, sin_ref = refs[4:6]
        pos = 6
    outs = refs[pos:pos + n_out]
    shift = mod_ref[0, 0, 0:1, :]
    scale = mod_ref[0, 0, 1:2, :]
    h = (x_ref[0] * (1.0 + scale) + shift).astype(BF16)

    def acc(col, width):
        return _dot(h, w_ref[:, col:col + width]) + b_ref[:, col:col + width]

    for kind, oi, ocol, width, wcol, wcol2, mult in plan:
        if kind == "ones":
            outs[oi][0, :, ocol:ocol + width] = jnp.ones((x_ref.shape[1], width), outs[oi].dtype)
            continue
        for c0 in range(0, width, COL_CHUNK):
            cw = min(COL_CHUNK, width - c0)
            a = acc(wcol + c0, cw)
            if kind == "rope":
                reps = cw // LANES
                cos = jnp.tile(cos_ref[...], (1, reps))
                sin = jnp.tile(sin_ref[...], (1, reps))
                a = (a * cos + acc(wcol2 + c0, cw) * sin) * mult
            elif kind == "mul":
                a = a * acc(wcol2 + c0, cw)
            outs[oi][0, :, ocol + c0:ocol + c0 + cw] = a.astype(outs[oi].dtype)


def _project(x, modtab, w, b, plan, out_widths, out_dtypes, ctx_len, rope=None):
    bsz, t, d = x.shape
    n = w.shape[1]
    tm = ROW_TILE
    ctx_blocks = ctx_len // tm
    mt = modtab
    in_specs = [pl.BlockSpec((1, tm, d), lambda i, j: (i, j, 0)),
                pl.BlockSpec((1, 1, 8, d), lambda i, j: (i, _seg_of_block(j, ctx_blocks), 0, 0)),
                pl.BlockSpec((d, n), lambda i, j: (0, 0)),
                pl.BlockSpec((1, n), lambda i, j: (0, 0))]
    args = [x, mt, w, b]
    if rope is not None:
        in_specs += [pl.BlockSpec((tm, LANES), lambda i, j: (j, 0))] * 2
        args += list(rope)
    out_specs = [pl.BlockSpec((1, tm, ow), lambda i, j: (i, j, 0)) for ow in out_widths]
    out_shape = [jax.ShapeDtypeStruct((bsz, t, ow), dt) for ow, dt in zip(out_widths, out_dtypes)]
    return pl.pallas_call(
        functools.partial(_proj_kernel, plan=tuple(plan), n_out=len(out_widths), use_rope=rope is not None),
        grid=(bsz, t // tm),
        in_specs=in_specs, out_specs=out_specs, out_shape=out_shape,
        compiler_params=_cparams("parallel", "parallel"),
    )(*args)


def _resid_ln(x, y, gate, g, b, alpha):
    r = alpha * x + gate * y
    mu = jnp.mean(r, axis=-1, keepdims=True)
    rc = r - mu
    var = jnp.mean(rc * rc, axis=-1, keepdims=True)
    return rc * lax.rsqrt(var + LN_EPS) * g + b


def _centred_conv(ext_ref, p, prev, nxt, w_ref, width, has_prev, has_next):
    tm = p.shape[0]
    h = HALO_ROWS
    pad = (width - 1) // 2
    ext_ref[0:h, :] = jnp.where(has_prev, prev, 0.0)
    ext_ref[h:h + tm, :] = p
    ext_ref[h + tm:h + tm + h, :] = jnp.where(has_next, nxt, 0.0)
    acc = None
    for k in range(width):
        term = ext_ref[h - pad + k:h - pad + k + tm, :] * w_ref[k:k + 1, :]
        acc = term if acc is None else acc + term
    return acc


def _seg_edges(j, ctx_blocks, n_blocks):
    has_prev = jnp.logical_and(j != 0, j != ctx_blocks)
    has_next = jnp.logical_and(j != ctx_blocks - 1, j != n_blocks - 1)
    return has_prev, has_next


def _out_plain_kernel(x_ref, mod_ref, o_ref, w_ref, b_ref, g_ref, bb_ref, out_ref, *, alpha):
    y = _dot(o_ref[0], w_ref[...]) + b_ref[...]
    out_ref[0] = _resid_ln(x_ref[0], y, mod_ref[0, 0, 2:3, :], g_ref[...], bb_ref[...], alpha)


def _out_conv_kernel(x_ref, mod_ref, bg_ref, p_ref, pp_ref, pn_ref, cw_ref, w_ref, b_ref, g_ref, bb_ref,
                     out_ref, ext_ref, *, alpha, ctx_blocks, n_blocks):
    j = pl.program_id(1)
    has_prev, has_next = _seg_edges(j, ctx_blocks, n_blocks)
    conv = _centred_conv(ext_ref, p_ref[0].astype(F32), pp_ref[0].astype(F32), pn_ref[0].astype(F32), cw_ref,
                         SC_WIDTH, has_prev, has_next)
    o = (bg_ref[0].astype(F32) * conv).astype(BF16)
    y = _dot(o, w_ref[...]) + b_ref[...]
    out_ref[0] = _resid_ln(x_ref[0], y, mod_ref[0, 0, 2:3, :], g_ref[...], bb_ref[...], alpha)


def _out_delta_kernel(x_ref, mod_ref, of_ref, ob_ref, z_ref, ng_ref, w_ref, b_ref, g_ref, bb_ref,
                      out_ref, *, alpha):
    hd = DN_HEAD_DIM
    ng = ng_ref[...]
    acc = None
    for h0 in range(0, of_ref.shape[2], hd):
        o = of_ref[0, :, h0:h0 + hd].astype(F32) + ob_ref[0, :, h0:h0 + hd].astype(F32)
        o = o * lax.rsqrt(jnp.mean(o * o, axis=-1, keepdims=True) + RMS_EPS) * ng
        o = (o * _silu(z_ref[0, :, h0:h0 + hd].astype(F32))).astype(BF16)
        part = _dot(o, w_ref[h0:h0 + hd, :])
        acc = part if acc is None else acc + part
    y = acc + b_ref[...]
    out_ref[0] = _resid_ln(x_ref[0], y, mod_ref[0, 0, 2:3, :], g_ref[...], bb_ref[...], alpha)


def _mixer_out(kind, x, modtab, acts, w_o, b_o, ln_g, ln_b, alpha, ctx_len, extra=None):
    bsz, t, d = x.shape
    tm = ROW_TILE
    nb = t // tm
    ctx_blocks = ctx_len // tm
    kin = w_o.shape[0]
    row = lambda i, j: (i, j, 0)
    const2 = lambda i, j: (0, 0)
    x_spec = pl.BlockSpec((1, tm, d), row)
    mod_spec = pl.BlockSpec((1, 1, 8, d), lambda i, j: (i, _seg_of_block(j, ctx_blocks), 0, 0))
    tail_specs = [pl.BlockSpec((kin, d), const2), pl.BlockSpec((1, d), const2),
                  pl.BlockSpec((1, d), const2), pl.BlockSpec((1, d), const2)]
    tail_args = [w_o, b_o, ln_g, ln_b]
    scratch = []
    if kind == "plain":
        body = functools.partial(_out_plain_kernel, alpha=alpha)
        in_specs = [x_spec, mod_spec, pl.BlockSpec((1, tm, kin), row)] + tail_specs
        args = [x, modtab, acts[0]] + tail_args
    elif kind == "conv":
        body = functools.partial(_out_conv_kernel, alpha=alpha, ctx_blocks=ctx_blocks, n_blocks=nb)
        hb = tm // HALO_ROWS
        last_halo = t // HALO_ROWS - 1
        in_specs = [x_spec, mod_spec, pl.BlockSpec((1, tm, d), row), pl.BlockSpec((1, tm, d), row),
                    pl.BlockSpec((1, HALO_ROWS, d), lambda i, j: (i, jnp.maximum(j * hb - 1, 0), 0)),
                    pl.BlockSpec((1, HALO_ROWS, d), lambda i, j: (i, jnp.minimum((j + 1) * hb, last_halo), 0)),
                    pl.BlockSpec((8, d), const2)] + tail_specs
        args = [x, modtab, acts[0], acts[1], acts[1], acts[1], extra] + tail_args
        scratch = [pltpu.VMEM((tm + 2 * HALO_ROWS, d), F32)]
    else:
        body = functools.partial(_out_delta_kernel, alpha=alpha)
        in_specs = [x_spec, mod_spec, pl.BlockSpec((1, tm, kin), row), pl.BlockSpec((1, tm, kin), row),
                    pl.BlockSpec((1, tm, kin), row), pl.BlockSpec((1, DN_HEAD_DIM), const2)] + tail_specs
        args = [x, modtab, acts[0], acts[1], acts[2], extra] + tail_args
    return pl.pallas_call(
        body, grid=(bsz, nb), in_specs=in_specs,
        out_specs=pl.BlockSpec((1, tm, d), row),
        out_shape=jax.ShapeDtypeStruct((bsz, t, d), F32),
        scratch_shapes=scratch,
        compiler_params=_cparams("parallel", "parallel"),
    )(*args)


def _swa_kernel(sink_ref, q_ref, kp_ref, ko_ref, kn_ref, kc_ref, vp_ref, vo_ref, vn_ref, vc_ref, o_ref,
                *, n_lat_blocks, ctx_blocks):
    j = pl.program_id(1)
    blk = ATT_BLOCK
    lat = j - ctx_blocks
    is_lat = j >= ctx_blocks
    n_keys = 3 * blk + kc_ref.shape[1]
    r = lax.broadcasted_iota(jnp.int32, (blk, n_keys), 0)
    c = lax.broadcasted_iota(jnp.int32, (blk, n_keys), 1)
    ok_prev = jnp.logical_and(is_lat, lat > 0)
    ok_next = jnp.logical_and(is_lat, lat < n_lat_blocks - 1)
    r_max = jnp.where(c < blk, jnp.where(ok_prev, c, -1),
                      jnp.where(c < 2 * blk, jnp.where(is_lat, blk, -1),
                                jnp.where(c < 3 * blk, jnp.where(ok_next, blk, -1), blk)))
    r_min = jnp.where(c < 2 * blk, 0, jnp.where(c < 3 * blk, c - 2 * blk, 0))
    valid = jnp.logical_and(r >= r_min, r <= r_max)
    lane = lax.broadcasted_iota(jnp.int32, (blk, LANES), 1)
    lo = lane < SWA_HEAD_DIM
    group = SWA_HEADS // SWA_KV_HEADS
    kcats, vcats = [], []
    for g in range(SWA_KV_HEADS):
        gs = slice(g * LANES, (g + 1) * LANES)
        kcats.append(jnp.concatenate([kp_ref[0, :, gs], ko_ref[0, :, gs], kn_ref[0, :, gs], kc_ref[0, :, gs]], axis=0))
        vcats.append(jnp.concatenate([vp_ref[0, :, gs], vo_ref[0, :, gs], vn_ref[0, :, gs], vc_ref[0, :, gs]], axis=0))
    qhs, sinks, kv = [], [], []
    for pair in range(SWA_HEADS // 2):
        qp = q_ref[0, :, pair * LANES:(pair + 1) * LANES]
        zero = jnp.zeros_like(qp)
        qhs += [jnp.where(lo, qp, zero), jnp.where(lo, zero, qp)]
        sinks += [sink_ref[2 * pair], sink_ref[2 * pair + 1]]
        kv += [(2 * pair) // group] * 2
    ss = [jnp.where(valid, _dot_nt(qh, kcats[g]), NEG_INF) for qh, g in zip(qhs, kv)]
    ms = [jnp.maximum(jnp.max(s, axis=-1, keepdims=True), sink) for s, sink in zip(ss, sinks)]
    es = [jnp.exp2(s - m) for s, m in zip(ss, ms)]
    invs = [1.0 / (jnp.sum(e, axis=-1, keepdims=True) + jnp.exp2(sink - m)) for e, m, sink in zip(es, ms, sinks)]
    accs = [_dot(e.astype(BF16), vcats[g]) * inv for e, g, inv in zip(es, kv, invs)]
    for pair in range(SWA_HEADS // 2):
        o_ref[0, :, pair * LANES:(pair + 1) * LANES] = jnp.where(lo, accs[2 * pair], accs[2 * pair + 1]).astype(o_ref.dtype)


def _swa_attention(q, kk, vv, sink, ctx_len):
    bsz, t, dq = q.shape
    dk = kk.shape[2]
    blk = ATT_BLOCK
    nb = t // blk
    ctx_blocks = ctx_len // blk
    n_lat_blocks = nb - ctx_blocks

    def prev_map(i, j, s):
        return (i, jnp.clip(j - 1, ctx_blocks, nb - 1), 0)

    def own_map(i, j, s):
        return (i, j, 0)

    def next_map(i, j, s):
        return (i, jnp.clip(j + 1, ctx_blocks, nb - 1), 0)

    def ctx_map(i, j, s):
        return (i, 0, 0)

    kv_specs = [pl.BlockSpec((1, blk, dk), prev_map), pl.BlockSpec((1, blk, dk), own_map),
                pl.BlockSpec((1, blk, dk), next_map), pl.BlockSpec((1, ctx_len, dk), ctx_map)]
    return pl.pallas_call(
        functools.partial(_swa_kernel, n_lat_blocks=n_lat_blocks, ctx_blocks=ctx_blocks),
        grid_spec=pltpu.PrefetchScalarGridSpec(
            num_scalar_prefetch=1, grid=(bsz, nb),
            in_specs=[pl.BlockSpec((1, blk, dq), own_map)] + kv_specs + kv_specs,
            out_specs=pl.BlockSpec((1, blk, dq), own_map)),
        out_shape=jax.ShapeDtypeStruct((bsz, t, dq), BF16),
        compiler_params=_cparams("parallel", "parallel"),
    )(sink, q, kk, kk, kk, kk, vv, vv, vv, vv)


def _diff_kernel(lam_ref, q_ref, k_ref, v_ref, g_ref, o_ref, *, ctx_len, lam_init):
    j = pl.program_id(2)
    blk = DIFF_Q_BLOCK
    ctx_blocks = ctx_len // blk
    lp = lam_ref[...]
    lam = (jnp.exp(jnp.sum(lp[0:1, :] * lp[1:2, :], axis=-1, keepdims=True))
           - jnp.exp(jnp.sum(lp[2:3, :] * lp[3:4, :], axis=-1, keepdims=True)) + lam_init)
    lane = lax.broadcasted_iota(jnp.int32, (blk, LANES), 1)
    lo = lane < DIFF_HEAD_DIM
    n_heads = q_ref.shape[2] // LANES

    def attend(n_keys):
        qhs, kv = [], []
        for h in range(n_heads):
            qp = q_ref[0, :, h * LANES:(h + 1) * LANES]
            zero = jnp.zeros_like(qp)
            qhs += [jnp.where(lo, qp, zero), jnp.where(lo, zero, qp)]
            kv += [h, h]
        ss = [_dot_nt(qh, k_ref[0, 0:n_keys, h * LANES:(h + 1) * LANES]) for qh, h in zip(qhs, kv)]
        es = [jnp.exp2(s - jnp.max(s, axis=-1, keepdims=True)) for s in ss]
        pvs = [_dot(e.astype(BF16), v_ref[0, 0:n_keys, 2 * h * LANES:2 * (h + 1) * LANES]) for e, h in zip(es, kv)]
        invs = [1.0 / pv[:, LANES:LANES + 1] for pv in pvs]
        for h in range(n_heads):
            o = pvs[2 * h][:, :LANES] * invs[2 * h] - (lam * invs[2 * h + 1]) * pvs[2 * h + 1][:, :LANES]
            o = o * lax.rsqrt(jnp.mean(o * o, axis=-1, keepdims=True) + RMS_EPS) * g_ref[...]
            o_ref[0, :, h * LANES:(h + 1) * LANES] = (o * (1.0 - lam_init)).astype(o_ref.dtype)

    @pl.when(j < ctx_blocks)
    def _():
        attend(ctx_len)

    @pl.when(j >= ctx_blocks)
    def _():
        attend(k_ref.shape[1])


def _diff_attention(q, k, v, lam_p, subln_g, ctx_len, lam_init):
    bsz, t, dq = q.shape
    blk = DIFF_Q_BLOCK
    nh = dq // LANES
    hs = DIFF_HEADS_PER_STEP
    qmap = lambda i, h, j: (i, j, h)
    kmap = lambda i, h, j: (i, 0, h)
    const2 = lambda i, h, j: (0, 0)
    return pl.pallas_call(
        functools.partial(_diff_kernel, ctx_len=ctx_len, lam_init=lam_init),
        grid=(bsz, nh // hs, t // blk),
        in_specs=[pl.BlockSpec((8, DIFF_HEAD_DIM), const2),
                  pl.BlockSpec((1, blk, hs * LANES), qmap),
                  pl.BlockSpec((1, t, hs * LANES), kmap),
                  pl.BlockSpec((1, t, 2 * hs * LANES), kmap),
                  pl.BlockSpec((1, LANES), const2)],
        out_specs=pl.BlockSpec((1, blk, hs * LANES), qmap),
        out_shape=jax.ShapeDtypeStruct((bsz, t, dq), BF16),
        compiler_params=_cparams("parallel", "parallel", "parallel"),
    )(jnp.pad(lam_p, ((0, 4), (0, 0))), q, k, v, subln_g.reshape(1, LANES))


def _delta_prep_kernel(x_ref, xp_ref, xn_ref, cw_ref, ba_ref, alog_ref, dtb_ref,
                       q_ref, k_ref, v_ref, gb_ref, ext_ref, *, ctx_blocks, n_blocks, nqk):
    j = pl.program_id(1)
    has_prev, has_next = _seg_edges(j, ctx_blocks, n_blocks)
    hd = DN_HEAD_DIM
    tm = x_ref.shape[1]
    for c0 in range(0, x_ref.shape[2], COL_CHUNK):
        cs = slice(c0, c0 + COL_CHUNK)
        conv = _centred_conv(ext_ref, x_ref[0, :, cs].astype(F32), xp_ref[0, :, cs].astype(F32),
                             xn_ref[0, :, cs].astype(F32), cw_ref.at[:, cs], DN_CONV, has_prev, has_next)
        a = _silu(conv)
        for h0 in range(0, COL_CHUNK, hd):
            col = c0 + h0
            ah = a[:, h0:h0 + hd]
            if col < 2 * nqk:
                ah = ah * lax.rsqrt(jnp.sum(ah * ah, axis=-1, keepdims=True) + RMS_EPS)
                if col < nqk:
                    q_ref[0, :, col:col + hd] = (ah * (hd ** -0.5)).astype(q_ref.dtype)
                else:
                    k_ref[0, :, col - nqk:col - nqk + hd] = ah.astype(k_ref.dtype)
            else:
                v_ref[0, :, col - 2 * nqk:col - 2 * nqk + hd] = ah.astype(v_ref.dtype)
    ba = ba_ref[0]
    lane = lax.broadcasted_iota(jnp.int32, ba.shape, 1)
    is_beta = (lane % 32) < DN_V_HEADS
    z = ba + dtb_ref[...]
    softplus = jnp.maximum(z, 0.0) + jnp.log(1.0 + jnp.exp(-jnp.abs(z)))
    g = -jnp.exp(alog_ref[...]) * softplus
    r = lax.broadcasted_iota(jnp.int32, (tm, tm), 0)
    c = lax.broadcasted_iota(jnp.int32, (tm, tm), 1)
    same = (r // DN_CHUNK) == (c // DN_CHUNK)
    tri_f = jnp.where(jnp.logical_and(same, c <= r), 1.0, 0.0).astype(BF16)
    tri_b = jnp.where(jnp.logical_and(same, c >= r), 1.0, 0.0).astype(BF16)
    parts = _split3(g)
    cum_f = _dot(tri_f, parts[0]) + _dot(tri_f, parts[1]) + _dot(tri_f, parts[2])
    cum_b = _dot(tri_b, parts[0]) + _dot(tri_b, parts[1]) + _dot(tri_b, parts[2])
    gcum = jnp.where(lane < 32, cum_f, cum_b)
    gb_ref[0] = jnp.where(is_beta, jax.nn.sigmoid(ba), gcum)


def _delta_prep(qkv_pre, conv_w, ba, alog_l, dtb_l, ctx_len, nqk, nv):
    bsz, t, nc = qkv_pre.shape
    tm = ROW_TILE
    nb = t // tm
    ctx_blocks = ctx_len // tm
    hb = tm // HALO_ROWS
    last_halo = t // HALO_ROWS - 1
    row = lambda i, j: (i, j, 0)
    const2 = lambda i, j: (0, 0)
    return pl.pallas_call(
        functools.partial(_delta_prep_kernel, ctx_blocks=ctx_blocks, n_blocks=nb, nqk=nqk),
        grid=(bsz, nb),
        in_specs=[pl.BlockSpec((1, tm, nc), row),
                  pl.BlockSpec((1, HALO_ROWS, nc), lambda i, j: (i, jnp.maximum(j * hb - 1, 0), 0)),
                  pl.BlockSpec((1, HALO_ROWS, nc), lambda i, j: (i, jnp.minimum((j + 1) * hb, last_halo), 0)),
                  pl.BlockSpec((8, nc), const2),
                  pl.BlockSpec((1, tm, LANES), row),
                  pl.BlockSpec((1, LANES), const2),
                  pl.BlockSpec((1, LANES), const2)],
        out_specs=[pl.BlockSpec((1, tm, nqk), row), pl.BlockSpec((1, tm, nqk), row),
                   pl.BlockSpec((1, tm, nv), row), pl.BlockSpec((1, tm, LANES), row)],
        out_shape=[jax.ShapeDtypeStruct((bsz, t, nqk), BF16), jax.ShapeDtypeStruct((bsz, t, nqk), BF16),
                   jax.ShapeDtypeStruct((bsz, t, nv), BF16), jax.ShapeDtypeStruct((bsz, t, LANES), F32)],
        scratch_shapes=[pltpu.VMEM((tm + 2 * HALO_ROWS, COL_CHUNK), F32)],
        compiler_params=_cparams("parallel", "parallel"),
    )(qkv_pre, qkv_pre, qkv_pre, conv_w, ba, alog_l, dtb_l)


DN_INV_BLOCK = 16


def _unit_triangular_inverses(lows, eye, r, c):
    n = lows[0].shape[0]
    b = DN_INV_BLOCK
    same = (r // b) == (c // b)
    pws = [jnp.where(same, low, 0.0) for low in lows]
    xs = [eye - pw for pw in pws]
    for _ in range(int(math.log2(b)) - 1):
        pbs = [pw.astype(BF16) for pw in pws]
        pws = [_dot(pb, pb) for pb in pbs]
        xs = [x + _dot(x.astype(BF16), pw.astype(BF16)) for x, pw in zip(xs, pws)]
    while b < n:
        pair = jnp.logical_and((r // (2 * b)) == (c // (2 * b)), (r // b) != (c // b))
        xbs = [x.astype(BF16) for x in xs]
        ys = [_dot(jnp.where(pair, low, 0.0).astype(BF16), xb).astype(BF16) for low, xb in zip(lows, xbs)]
        xs = [x - _dot(xb, y) for x, xb, y in zip(xs, xbs, ys)]
        b *= 2
    return xs


DN_CHUNK_QK_HEADS = 2
DN_SCAN_HEADS = 16


def _delta_chunk_kernel(q_ref, k_ref, v_ref, gb_ref, gt_ref, u_ref, w_ref, qg_ref, kd_ref, a_ref, *, rep, nhq):
    hq0 = pl.program_id(1) * nhq
    ck = DN_CHUNK
    hd = DN_HEAD_DIM
    r = lax.broadcasted_iota(jnp.int32, (ck, ck), 0)
    c = lax.broadcasted_iota(jnp.int32, (ck, ck), 1)
    eye = jnp.where(r == c, 1.0, 0.0)
    lane = lax.broadcasted_iota(jnp.int32, (ck, LANES), 1)
    lows, rhss, dests = [], [], []
    for sub, hh in [(sub, hh) for sub in range(q_ref.shape[1] // ck) for hh in range(nhq)]:
        rows = slice(sub * ck, (sub + 1) * ck)
        q = q_ref[0, rows, hh * hd:(hh + 1) * hd]
        k = k_ref[0, rows, hh * hd:(hh + 1) * hd]
        kk = _dot_nt(k, k)
        qk = _dot_nt(q, k)
        gb = gb_ref[0, rows, :]
        qf = q.astype(F32)
        kf = k.astype(F32)
        for d in range(2):
            incl = (r >= c) if d == 0 else (r <= c)
            strict = (r > c) if d == 0 else (r < c)
            for jh in range(rep):
                head = (hq0 + hh) * rep + jh
                cols = slice((hh * rep + jh) * hd, (hh * rep + jh + 1) * hd)
                bcol = jnp.sum(jnp.where(lane == d * 32 + head, gb, 0.0), axis=-1, keepdims=True)
                gcol = jnp.sum(jnp.where(lane == d * 32 + DN_V_HEADS + head, gb, 0.0), axis=-1, keepdims=True)
                grow = gt_ref[d, 0, hh, jh:jh + 1, rows]
                glast = grow[:, ck - 1:ck] if d == 0 else grow[:, 0:1]
                decay = jnp.where(incl, jnp.exp(jnp.where(incl, gcol - grow, 0.0)), 0.0)
                lows.append(jnp.where(strict, kk * decay, 0.0) * bcol)
                v = v_ref[0, rows, cols].astype(F32)
                eg = jnp.exp(gcol)
                rhss.append(jnp.concatenate([v * bcol, kf * (bcol * eg)], axis=1).astype(BF16))
                dests.append((d, rows, cols))
                qg_ref[d, 0, rows, cols] = (qf * eg).astype(qg_ref.dtype)
                kd_ref[d, 0, rows, cols] = (kf * jnp.exp(glast - gcol)).astype(kd_ref.dtype)
                a_ref[d, 0, rows, cols] = jnp.where(incl, qk * decay, 0.0).astype(a_ref.dtype)
    tinvs = _unit_triangular_inverses(lows, eye, r, c)
    uws = [_dot(tinv.astype(BF16), rhs) for tinv, rhs in zip(tinvs, rhss)]
    for uw, (d, rows, cols) in zip(uws, dests):
        u_ref[d, 0, rows, cols] = uw[:, :hd].astype(u_ref.dtype)
        w_ref[d, 0, rows, cols] = uw[:, hd:].astype(w_ref.dtype)


def _delta_chunks(q, k, v, gb, gt_a):
    bsz, t, nqk = q.shape
    nv = v.shape[2]
    hd = DN_HEAD_DIM
    nh = nqk // hd
    rep = nv // nqk
    tm = ROW_TILE
    nhq = DN_CHUNK_QK_HEADS
    qmap = lambda i, h, j: (i, j, h)
    omap = lambda i, h, j: (0, i, j, h)
    oshape = lambda dt: jax.ShapeDtypeStruct((2, bsz, t, nv), dt)
    ospec = pl.BlockSpec((2, 1, tm, nhq * rep * hd), omap)
    return pl.pallas_call(
        functools.partial(_delta_chunk_kernel, rep=rep, nhq=nhq),
        grid=(bsz, nh // nhq, t // tm),
        in_specs=[pl.BlockSpec((1, tm, nhq * hd), qmap), pl.BlockSpec((1, tm, nhq * hd), qmap),
                  pl.BlockSpec((1, tm, nhq * rep * hd), qmap),
                  pl.BlockSpec((1, tm, LANES), lambda i, h, j: (i, j, 0)),
                  pl.BlockSpec((2, 1, nhq, 8, tm), lambda i, h, j: (0, i, h, 0, j))],
        out_specs=[ospec] * 5,
        out_shape=[oshape(BF16)] * 5,
        compiler_params=_cparams("parallel", "parallel", "parallel"),
    )(q, k, v, gb, gt_a)


def _scan_chunk_index(d, i, ctx_chunks, n_chunks):
    back = jnp.where(i < ctx_chunks, ctx_chunks - 1 - i, n_chunks - 1 - (i - ctx_chunks))
    return jnp.where(d == 0, i, back)


def _delta_scan_kernel(u_ref, w_ref, qg_ref, kd_ref, a_ref, gt_ref, o_ref, s_ref, *, n_heads):
    d = pl.program_id(0)
    i = pl.program_id(3)
    ck = DN_CHUNK
    hd = DN_HEAD_DIM

    @pl.when(i == 0)
    def _():
        s_ref[...] = jnp.zeros_like(s_ref)

    heads = range(n_heads)
    cols = [slice(jh * hd, (jh + 1) * hd) for jh in heads]
    grows = gt_ref[0, 0, 0]
    glast = jnp.where(d == 0, grows[:, ck - 1:ck], grows[:, 0:1])
    carry = jnp.exp(glast)
    ss = [s_ref[jh] for jh in heads]
    wss = [_dot(jnp.concatenate([w_ref[0, 0, :, cs], qg_ref[0, 0, :, cs]], axis=0), s.astype(BF16))
           for cs, s in zip(cols, ss)]
    v_news = [(u_ref[0, 0, :, cs] - ws[:ck]).astype(BF16) for cs, ws in zip(cols, wss)]
    outs = [ws[ck:] + _dot(a_ref[0, 0, :, cs], vn) for cs, ws, vn in zip(cols, wss, v_news)]
    upds = [_dot_tn(kd_ref[0, 0, :, cs], vn) for cs, vn in zip(cols, v_news)]
    for jh in heads:
        o_ref[0, 0, :, cols[jh]] = outs[jh].astype(o_ref.dtype)
        s_ref[jh] = ss[jh] * carry[jh:jh + 1, :] + upds[jh]


def _delta_scan(u, w, qg, kd, a, gt_b, ctx_len):
    _, bsz, t, nv = u.shape
    ck = DN_CHUNK
    hd = DN_HEAD_DIM
    nh = DN_SCAN_HEADS
    groups = nv // (nh * hd)
    n_chunks = t // ck
    ctx_chunks = ctx_len // ck
    cmap = lambda d, b, g, i: (d, b, _scan_chunk_index(d, i, ctx_chunks, n_chunks), g)
    spec = pl.BlockSpec((1, 1, ck, nh * hd), cmap)
    return pl.pallas_call(
        functools.partial(_delta_scan_kernel, n_heads=nh),
        grid=(2, bsz, groups, n_chunks),
        in_specs=[spec] * 5 + [pl.BlockSpec((1, 1, 1, nh, ck),
                                            lambda d, b, g, i: (d, b, g, 0, _scan_chunk_index(d, i, ctx_chunks, n_chunks)))],
        out_specs=spec,
        out_shape=jax.ShapeDtypeStruct((2, bsz, t, nv), BF16),
        scratch_shapes=[pltpu.VMEM((nh, hd, hd), F32)],
        compiler_params=_cparams("parallel", "parallel", "parallel", "arbitrary"),
    )(u, w, qg, kd, a, gt_b)


def _pack_bf16_pairs(a):
    n = a.shape[1] // 2
    bits = lax.bitcast_convert_type(a, jnp.uint32)
    return (bits[:, :n] & jnp.uint32(0xFFFF0000)) | (bits[:, n:] >> 16)


def _unpack_bf16_pairs(p):
    hi = lax.bitcast_convert_type(p & jnp.uint32(0xFFFF0000), F32)
    lo = lax.bitcast_convert_type(p << 16, F32)
    return jnp.concatenate([hi, lo], axis=1)


def _router_kernel(x_ref, mod_ref, whi_ref, wlo_ref, br_ref, ha_ref, hb_ref, idx_ref, wt_ref, cnt_ref, run_ref):
    first = jnp.logical_and(pl.program_id(0) == 0, pl.program_id(1) == 0)

    @pl.when(first)
    def _():
        run_ref[...] = jnp.zeros_like(run_ref)

    shift = mod_ref[0, 0, 3:4, :]
    scale = mod_ref[0, 0, 4:5, :]
    h = x_ref[0] * (1.0 + scale) + shift
    h_hi = h.astype(BF16)
    packed = _pack_bf16_pairs(h_hi.astype(F32))
    ha_ref[0] = packed[:, :SC_ROW_WORDS]
    hb_ref[0] = packed[:, SC_ROW_WORDS:]
    h_lo = (h - h_hi.astype(F32)).astype(BF16)
    w_hi = whi_ref[...]
    w_lo = wlo_ref[...]
    logits = _dot(h_hi, w_hi) + _dot(h_hi, w_lo) + _dot(h_lo, w_hi) + br_ref[...]
    lane = lax.broadcasted_iota(jnp.int32, logits.shape, 1).astype(F32)
    cur = jnp.where(lane < N_EXPERTS, logits, -jnp.inf)
    idx_out = jnp.zeros(logits.shape, F32)
    wt_out = jnp.zeros(logits.shape, F32)
    tops, sels = [], []
    hits = jnp.zeros(logits.shape, F32)
    for kk in range(TOP_K):
        m = jnp.max(cur, axis=-1, keepdims=True)
        sel = jnp.min(jnp.where(cur == m, lane, float(LANES)), axis=-1, keepdims=True)
        tops.append(m)
        sels.append(sel)
        idx_out = jnp.where(lane == kk, sel, idx_out)
        hits = jnp.where(lane == sel, 1.0, hits)
        cur = jnp.where(lane == sel, -jnp.inf, cur)
    es = [jnp.exp(m - tops[0]) for m in tops]
    den = es[0] + es[1] + es[2] + es[3]
    for kk in range(TOP_K):
        wt_out = jnp.where(lane == kk, es[kk] / den, wt_out)
    tm = logits.shape[0]
    r = lax.broadcasted_iota(jnp.int32, (tm, tm), 0)
    c = lax.broadcasted_iota(jnp.int32, (tm, tm), 1)
    before = _dot(jnp.where(c < r, 1.0, 0.0).astype(BF16), hits.astype(BF16)) + run_ref[...]
    for kk in range(TOP_K):
        rank = jnp.sum(jnp.where(lane == sels[kk], before, 0.0), axis=-1, keepdims=True)
        idx_out = jnp.where(lane == TOP_K + kk, rank, idx_out)
    total = run_ref[...] + jnp.sum(hits, axis=0, keepdims=True)
    run_ref[...] = total
    cnt_ref[...] = jnp.broadcast_to(total, cnt_ref.shape)
    idx_ref[0] = idx_out.astype(jnp.int32)
    wt_ref[0] = wt_out


def _router(x, modtab, w_hi, w_lo, b_r, ctx_len, first_block):
    bsz, t, d = x.shape
    tm = ROW_TILE
    ctx_blocks = ctx_len // tm
    nb = t // tm - first_block
    row_in = lambda i, j: (i, j + first_block, 0)
    row = lambda i, j: (i, j, 0)
    const2 = lambda i, j: (0, 0)
    tout = nb * tm
    return pl.pallas_call(
        _router_kernel,
        grid=(bsz, nb),
        in_specs=[pl.BlockSpec((1, tm, d), row_in),
                  pl.BlockSpec((1, 1, 8, d), lambda i, j: (i, _seg_of_block(j + first_block, ctx_blocks), 0, 0)),
                  pl.BlockSpec((d, LANES), const2), pl.BlockSpec((d, LANES), const2),
                  pl.BlockSpec((1, LANES), const2)],
        out_specs=[pl.BlockSpec((1, tm, SC_ROW_WORDS), row), pl.BlockSpec((1, tm, SC_ROW_WORDS), row),
                   pl.BlockSpec((1, tm, LANES), row),
                   pl.BlockSpec((1, tm, LANES), row), pl.BlockSpec((8, LANES), const2)],
        out_shape=[jax.ShapeDtypeStruct((bsz, tout, SC_ROW_WORDS), jnp.uint32),
                   jax.ShapeDtypeStruct((bsz, tout, SC_ROW_WORDS), jnp.uint32),
                   jax.ShapeDtypeStruct((bsz, tout, LANES), jnp.int32),
                   jax.ShapeDtypeStruct((bsz, tout, LANES), F32),
                   jax.ShapeDtypeStruct((8, LANES), F32)],
        scratch_shapes=[pltpu.VMEM((1, LANES), F32)],
        compiler_params=_cparams("arbitrary", "arbitrary"),
    )(x, modtab, w_hi, w_lo, b_r)


def _expert_kernel(be_ref, nused_ref, xa_ref, xb_ref, w1_ref, b1_ref, w2_ref, b2_ref, ya_ref, yb_ref,
                   w1b_ref, w2b_ref):
    i = pl.program_id(0)
    dff = w2_ref.shape[1]
    prev = be_ref[jnp.maximum(i - 1, 0)]
    new_expert = jnp.logical_or(i == 0, be_ref[i] != prev)

    @pl.when(jnp.logical_and(new_expert, i < nused_ref[0]))
    def _():
        for r0 in range(0, w1_ref.shape[1], LANES):
            w1b_ref[r0:r0 + LANES, :] = w1_ref[0, r0:r0 + LANES, :].astype(BF16)
        for r0 in range(0, dff, LANES):
            w2b_ref[r0:r0 + LANES, :] = w2_ref[0, r0:r0 + LANES, :].astype(BF16)

    @pl.when(i < nused_ref[0])
    def _():
        x = _unpack_bf16_pairs(jnp.concatenate([xa_ref[...], xb_ref[...]], axis=1)).astype(BF16)
        chunks = range(0, dff, COL_CHUNK)
        gates = [_dot(x, w1b_ref[:, c0:c0 + COL_CHUNK]) + b1_ref[0, :, c0:c0 + COL_CHUNK] for c0 in chunks]
        ups = [_dot(x, w1b_ref[:, dff + c0:dff + c0 + COL_CHUNK]) + b1_ref[0, :, dff + c0:dff + c0 + COL_CHUNK]
               for c0 in chunks]
        gates = [jnp.minimum(gate, SWIGLU_LIMIT) for gate in gates]
        ups = [jnp.clip(up, -SWIGLU_LIMIT, SWIGLU_LIMIT) for up in ups]
        acts = [((up + 1.0) * gate * jax.nn.sigmoid(SWIGLU_ALPHA * gate)).astype(BF16) for gate, up in zip(gates, ups)]
        parts = [_dot(act, w2b_ref[c0:c0 + COL_CHUNK, :]) for act, c0 in zip(acts, chunks)]
        acc = parts[0]
        for part in parts[1:]:
            acc = acc + part
        y = (acc + b2_ref[0]).astype(BF16).astype(F32)
        packed = _pack_bf16_pairs(y)
        ya_ref[...] = packed[:, :SC_ROW_WORDS]
        yb_ref[...] = packed[:, SC_ROW_WORDS:]

    @pl.when(i >= nused_ref[0])
    def _():
        ya_ref[...] = jnp.zeros_like(ya_ref)
        yb_ref[...] = jnp.zeros_like(yb_ref)


def _experts(xa, xb, block_e, n_used, w1, b1, w2, b2, layer):
    n_rows, dq = xa.shape
    d = 4 * dq
    tm = EXPERT_TILE
    depth, ne, _, dff2 = w1.shape
    dff = dff2 // 2
    w1 = w1.reshape(depth * ne, d, dff2)
    w2 = w2.reshape(depth * ne, dff, d)
    b1 = b1.reshape(depth * ne, 1, dff2)
    b2 = b2.reshape(depth * ne, 1, d)
    emap = lambda i, be, nu: (layer * ne + be[i], 0, 0)
    return pl.pallas_call(
        _expert_kernel,
        grid_spec=pltpu.PrefetchScalarGridSpec(
            num_scalar_prefetch=2, grid=(n_rows // tm,),
            in_specs=[pl.BlockSpec((tm, dq), lambda i, be, nu: (i, 0)),
                      pl.BlockSpec((tm, dq), lambda i, be, nu: (i, 0)),
                      pl.BlockSpec((1, d, dff2), emap),
                      pl.BlockSpec((1, 1, dff2), emap),
                      pl.BlockSpec((1, dff, d), emap),
                      pl.BlockSpec((1, 1, d), emap)],
            out_specs=[pl.BlockSpec((tm, dq), lambda i, be, nu: (i, 0))] * 2,
            scratch_shapes=[pltpu.VMEM((d, dff2), BF16), pltpu.VMEM((dff, d), BF16)]),
        out_shape=[jax.ShapeDtypeStruct((n_rows, dq), jnp.uint32)] * 2,
        compiler_params=_cparams("arbitrary"),
    )(block_e, n_used, xa, xb, w1, b1, w2, b2)


def _combine_kernel(x_ref, mod_ref, za_ref, zb_ref, wt_ref, g_ref, b_ref, out_ref, *, alpha):
    wt = wt_ref[0]
    f = None
    for kk in range(TOP_K):
        term = _unpack_bf16_pairs(jnp.concatenate([za_ref[kk, 0], zb_ref[kk, 0]], axis=1)) * wt[:, kk:kk + 1]
        f = term if f is None else f + term
    out_ref[0] = _resid_ln(x_ref[0], f, mod_ref[0, 0, 5:6, :], g_ref[...], b_ref[...], alpha)


def _combine(x, modtab, za, zb, wt, ln_g, ln_b, alpha, ctx_len, first_block):
    bsz, t, d = x.shape
    tm = ROW_TILE
    ctx_blocks = ctx_len // tm
    nb = t // tm - first_block
    row_in = lambda i, j: (i, j + first_block, 0)
    row = lambda i, j: (i, j, 0)
    const2 = lambda i, j: (0, 0)
    return pl.pallas_call(
        functools.partial(_combine_kernel, alpha=alpha),
        grid=(bsz, nb),
        in_specs=[pl.BlockSpec((1, tm, d), row_in),
                  pl.BlockSpec((1, 1, 8, d), lambda i, j: (i, _seg_of_block(j + first_block, ctx_blocks), 0, 0)),
                  pl.BlockSpec((TOP_K, 1, tm, SC_ROW_WORDS), lambda i, j: (0, i, j, 0)),
                  pl.BlockSpec((TOP_K, 1, tm, SC_ROW_WORDS), lambda i, j: (0, i, j, 0)),
                  pl.BlockSpec((1, tm, LANES), row),
                  pl.BlockSpec((1, d), const2), pl.BlockSpec((1, d), const2)],
        out_specs=pl.BlockSpec((1, tm, d), row),
        out_shape=jax.ShapeDtypeStruct((bsz, nb * tm, d), F32),
        compiler_params=_cparams("parallel", "parallel"),
    )(x, modtab, za, zb, wt, ln_g, ln_b)


def _moe_dispatch(x, modtab, w_r, b_r, ctx_len, latent_only):
    bsz, t, d = x.shape
    first_block = ctx_len // ROW_TILE if latent_only else 0
    wr_pad = jnp.pad(w_r, ((0, 0), (0, LANES - N_EXPERTS)))
    br_pad = jnp.pad(b_r, (0, LANES - N_EXPERTS)).reshape(1, LANES)
    wr_hi = wr_pad.astype(BF16)
    wr_lo = (wr_pad - wr_hi.astype(F32)).astype(BF16)
    ha, hb, idx, wt, cnt = _router(x, modtab, wr_hi, wr_lo, br_pad, ctx_len, first_block)
    tr = ha.shape[1]
    dq = ha.shape[2]
    n_tok = bsz * tr
    n_assign = n_tok * TOP_K
    tm = EXPERT_TILE
    experts = jnp.arange(N_EXPERTS, dtype=jnp.int32)
    counts = cnt[0, :N_EXPERTS].astype(jnp.int32)
    padded = (counts + tm - 1) // tm * tm
    pends = jnp.cumsum(padded)
    pstarts = pends - padded
    e_tk = idx[:, :, :TOP_K].reshape(n_tok, TOP_K)
    rank = idx[:, :, TOP_K:2 * TOP_K].reshape(n_tok, TOP_K)
    dest = jnp.sum(jnp.where(e_tk[:, :, None] == experts, pstarts, 0), axis=-1) + rank
    n_rows = (-(-n_assign // tm) + N_EXPERTS) * tm
    n_blocks = n_rows // tm
    starts = jnp.arange(n_blocks, dtype=jnp.int32) * tm
    block_e = jnp.minimum(jnp.sum((pends[None, :] <= starts[:, None]).astype(jnp.int32), axis=1), N_EXPERTS - 1)
    n_used = (pends[-1] // tm).astype(jnp.int32).reshape(1)
    dest_t = dest.T
    xa = _sc_scatter_rows(ha.reshape(n_tok, dq), dest_t, n_rows)
    xb = _sc_scatter_rows(hb.reshape(n_tok, dq), dest_t, n_rows)
    return dict(xa=xa, xb=xb, block_e=block_e, n_used=n_used, dest_t=dest_t, wt=wt, first_block=first_block,
                shape=(bsz, tr, dq))


def _moe_experts(st, expert_params):
    bsz, tr, dq = st["shape"]
    ya, yb = _experts(st["xa"], st["xb"], st["block_e"], st["n_used"], *expert_params)
    flat = st["dest_t"].reshape(1, bsz * tr * TOP_K)
    za = _sc_gather_rows(ya, flat).reshape(TOP_K, bsz, tr, dq)
    zb = _sc_gather_rows(yb, flat).reshape(TOP_K, bsz, tr, dq)
    return za, zb


def _moe_combine(x, modtab, st, z, ln_g, ln_b, alpha, ctx_len):
    return _combine(x, modtab, z[0], z[1], st["wt"], ln_g, ln_b, alpha, ctx_len, st["first_block"])


SC_WINDOW = 128
SC_ROW_WORDS = 256


def _sc_mesh():
    return plsc.VectorSubcoreMesh(core_axis_name="c", subcore_axis_name="s")


def _sc_scatter_rows(src, idx, n_out):
    n_src, width = src.shape
    n_k = idx.shape[0]
    per_k = n_src // SC_WINDOW

    @functools.partial(pl.kernel, out_type=jax.ShapeDtypeStruct((n_out, width), src.dtype), mesh=_sc_mesh())
    def scatter(x_hbm, i_hbm, o_hbm):
        def body(x_vmem, i_vmem):
            pltpu.sync_copy(x_vmem, o_hbm.at[i_vmem.at[0]])

        pltpu.emit_pipeline(
            body, grid=(n_k * per_k,),
            in_specs=[pl.BlockSpec((SC_WINDOW, width), lambda i: (i % per_k, 0)),
                      pl.BlockSpec((1, SC_WINDOW), lambda i: (i // per_k, i % per_k))],
            out_specs=[],
            core_axis_name=("c", "s"),
            dimension_semantics=(pltpu.PARALLEL,),
        )(x_hbm, i_hbm)

    return scatter(src, idx)


def _sc_gather_rows(table, idx):
    width = table.shape[1]
    n_idx = idx.shape[1]

    @functools.partial(pl.kernel, out_type=jax.ShapeDtypeStruct((n_idx, width), table.dtype), mesh=_sc_mesh())
    def gather(x_hbm, i_hbm, o_hbm):
        def body(i_vmem, o_vmem):
            pltpu.sync_copy(x_hbm.at[i_vmem.at[0]], o_vmem)

        pltpu.emit_pipeline(
            body, grid=(n_idx // SC_WINDOW,),
            in_specs=[pl.BlockSpec((1, SC_WINDOW), lambda i: (0, i))],
            out_specs=[pl.BlockSpec((SC_WINDOW, width), lambda i: (i, 0))],
            core_axis_name=("c", "s"),
            dimension_semantics=(pltpu.PARALLEL,),
        )(i_hbm, o_hbm)

    return gather(table, idx)


def _rot_cols(w, head_dim):
    lead = w.shape[:-1]
    q = head_dim // 4
    wr = w.reshape(lead + (-1, 4, q))
    out = jnp.stack([-wr[..., 1, :], wr[..., 0, :], -wr[..., 3, :], wr[..., 2, :]], axis=-2)
    return out.reshape(w.shape)


def _dup_heads(w, head_dim):
    lead = w.shape[:-1]
    wr = w.reshape(lead + (-1, 1, head_dim))
    return jnp.concatenate([wr, wr], axis=-2).reshape(lead + (-1,))


def _rope_tables(n_lat, ctx_len, head_dim):
    rows = n_lat // GRID_W
    row = jnp.repeat(jnp.arange(rows, dtype=F32), GRID_W)
    col = jnp.tile(jnp.arange(GRID_W, dtype=F32), rows)
    half = head_dim // 2
    inv = ROPE_BASE ** (-jnp.arange(0, half, 2, dtype=F32) / half)
    ar = row[:, None] * inv
    ac = col[:, None] * inv
    ang = jnp.concatenate([ar, ar, ac, ac], axis=-1)
    cos = jnp.concatenate([jnp.ones((ctx_len, head_dim), F32), jnp.cos(ang)], axis=0)
    sin = jnp.concatenate([jnp.zeros((ctx_len, head_dim), F32), jnp.sin(ang)], axis=0)
    reps = LANES // head_dim
    return jnp.tile(cos, (1, reps)), jnp.tile(sin, (1, reps))


def _conv_mixer(x, modtab, w_in, w_conv, w_out, ln_g, ln_b, alpha, ctx_len):
    d = x.shape[2]
    plan = [("plain", 0, 0, d, 0, None, 1.0), ("mul", 1, 0, d, d, 2 * d, 1.0)]
    bg, p = _project(x, modtab, w_in.astype(BF16), jnp.zeros((1, 3 * d), F32), plan, (d, d), (BF16, BF16), ctx_len)
    cw = jnp.pad(w_conv, ((0, 8 - SC_WIDTH), (0, 0)))
    return _mixer_out("conv", x, modtab, (bg, p), w_out.astype(BF16), jnp.zeros((1, d), F32), ln_g, ln_b, alpha,
                      ctx_len, extra=cw)


def _swa_mixer(x, modtab, w_qkv, b_qkv, sink, w_o, b_o, ln_g, ln_b, alpha, ctx_len, rope):
    d = x.shape[2]
    nq = SWA_HEADS * SWA_HEAD_DIM
    nkv = SWA_KV_HEADS * SWA_HEAD_DIM
    hd = SWA_HEAD_DIM

    def arrange(a):
        q, k, v = a[..., :nq], a[..., nq:nq + nkv], a[..., nq + nkv:]
        kk = _dup_heads(k, hd)
        return jnp.concatenate([q, _rot_cols(q, hd), kk, _rot_cols(kk, hd), _dup_heads(v, hd)], axis=-1)

    w = arrange(w_qkv).astype(BF16)
    b = arrange(b_qkv.reshape(1, -1))
    log2e = math.log2(math.e)
    plan = [("rope", 0, 0, nq, 0, nq, SWA_HEAD_DIM ** -0.5 * log2e),
            ("rope", 1, 0, 2 * nkv, 2 * nq, 2 * nq + 2 * nkv, 1.0),
            ("plain", 2, 0, 2 * nkv, 2 * nq + 4 * nkv, None, 1.0)]
    q, kk, vv = _project(x, modtab, w, b, plan, (nq, 2 * nkv, 2 * nkv), (BF16, BF16, BF16), ctx_len, rope=rope)
    o = _swa_attention(q, kk, vv, sink.astype(F32) * log2e, ctx_len)
    return _mixer_out("plain", x, modtab, (o,), w_o.astype(BF16), b_o.reshape(1, d), ln_g, ln_b, alpha, ctx_len)


def _diff_mixer(x, modtab, w_qkv, lam_p, subln_g, w_o, lam_init, ln_g, ln_b, alpha, ctx_len, rope):
    d = x.shape[2]
    hd = DIFF_HEAD_DIM
    wq, wk, wv = w_qkv[:, :d], w_qkv[:, d:2 * d], w_qkv[:, 2 * d:]
    w = jnp.concatenate([wq, _rot_cols(wq, hd), wk, _rot_cols(wk, hd), wv], axis=-1).astype(BF16)
    plan = [("rope", 0, 0, d, 0, d, DIFF_HEAD_DIM ** -0.5 * math.log2(math.e)),
            ("rope", 1, 0, d, 2 * d, 3 * d, 1.0)]
    for h in range(DIFF_HEADS):
        plan += [("plain", 2, 2 * h * LANES, LANES, 4 * d + h * LANES, None, 1.0),
                 ("ones", 2, (2 * h + 1) * LANES, LANES, None, None, 1.0)]
    q, k, v = _project(x, modtab, w, jnp.zeros((1, 5 * d), F32), plan, (d, d, 2 * d), (BF16, BF16, BF16), ctx_len,
                       rope=rope)
    o = _diff_attention(q, k, v, lam_p.astype(F32), subln_g.astype(F32), ctx_len, lam_init)
    return _mixer_out("plain", x, modtab, (o,), w_o.astype(BF16), jnp.zeros((1, d), F32), ln_g, ln_b, alpha, ctx_len)


def _delta_mixer(x, modtab, w_qkvz, w_ba, a_log, dt_bias, w_conv, norm_g, w_o, ln_g, ln_b, alpha, ctx_len):
    bsz, t, d = x.shape
    nqk = DN_QK_HEADS * DN_HEAD_DIM
    nv = DN_V_HEADS * DN_HEAD_DIM
    nc = 2 * nqk + nv
    nba = w_ba.shape[1]
    w = jnp.concatenate([w_qkvz, jnp.pad(w_ba, ((0, 0), (0, LANES - nba)))], axis=-1).astype(BF16)
    ntot = w.shape[1]
    plan = [("plain", 0, 0, nc, 0, None, 1.0), ("plain", 1, 0, nv, nc, None, 1.0),
            ("plain", 2, 0, LANES, nc + nv, None, 1.0)]
    qkv_pre, z, ba = _project(x, modtab, w, jnp.zeros((1, ntot), F32), plan, (nc, nv, LANES), (BF16, BF16, F32),
                              ctx_len)
    zeros16 = jnp.zeros((DN_V_HEADS,), F32)
    lanes_of = lambda p: jnp.pad(jnp.concatenate([zeros16, p[0], zeros16, p[1]]), (0, LANES - 4 * DN_V_HEADS))
    alog_l = lanes_of(a_log.astype(F32)).reshape(1, LANES)
    dtb_l = lanes_of(dt_bias.astype(F32)).reshape(1, LANES)
    cw = jnp.pad(w_conv, ((0, 8 - DN_CONV), (0, 0)))
    q, k, v, gb = _delta_prep(qkv_pre, cw, ba, alog_l, dtb_l, ctx_len, nqk, nv)
    rep = DN_V_HEADS // DN_QK_HEADS
    gcum = jnp.stack([gb[:, :, DN_V_HEADS:2 * DN_V_HEADS], gb[:, :, 3 * DN_V_HEADS:4 * DN_V_HEADS]])
    gt = jnp.transpose(gcum, (0, 1, 3, 2))
    gt_a = jnp.pad(gt.reshape(2, bsz, DN_QK_HEADS, rep, t), ((0, 0), (0, 0), (0, 0), (0, 8 - rep), (0, 0)))
    gt_b = gt.reshape(2, bsz, DN_V_HEADS // DN_SCAN_HEADS, DN_SCAN_HEADS, t)
    u, w_, qg, kd, a = _delta_chunks(q, k, v, gb, gt_a)
    o2 = _delta_scan(u, w_, qg, kd, a, gt_b, ctx_len)
    return _mixer_out("delta", x, modtab, (o2[0], o2[1], z), w_o.astype(BF16), jnp.zeros((1, d), F32),
                      ln_g, ln_b, alpha, ctx_len, extra=norm_g.astype(F32).reshape(1, DN_HEAD_DIM))


def kernel(x, c, ctx, c_ctx, mod_w, mod_b, ln1_g, ln1_b, ln2_g, ln2_b, router_w, router_b, exp_w1, exp_b1, exp_w2, exp_b2, conv_in_w, conv_w, conv_out_w, swa_qkv_w, swa_qkv_b, swa_sink, swa_out_w, swa_out_b, diff_qkv_w, diff_lambda, diff_subln_g, diff_out_w, delta_qkvz_w, delta_ba_w, delta_a_log, delta_dt_bias, delta_conv_w, delta_norm_g, delta_out_w):
    bsz, n_lat, d = x.shape
    ctx_len = ctx.shape[1]
    depth = mod_w.shape[0]
    alpha = (2 * depth) ** 0.25
    cc = jnp.zeros((16, d), F32).at[:bsz].set(c).at[bsz].set(c_ctx)
    mods = _mod_vectors(cc, mod_w, mod_b).reshape(depth, 16, 6, d)
    mod_lat = mods[:, :bsz]
    mod_ctx = jnp.broadcast_to(mods[:, bsz:bsz + 1], mod_lat.shape)
    modtabs = jnp.pad(jnp.stack([mod_ctx, mod_lat], axis=2), ((0, 0), (0, 0), (0, 0), (0, 2), (0, 0)))
    rope = _rope_tables(n_lat, ctx_len, SWA_HEAD_DIM)
    row1 = lambda a: a.reshape(1, d)
    groups = [slice(g * bsz // BATCH_GROUPS, (g + 1) * bsz // BATCH_GROUPS) for g in range(BATCH_GROUPS)]
    xs = [jnp.concatenate([ctx[g], x[g]], axis=1) for g in groups]
    for i in range(depth):
        last = i == depth - 1
        kind, j = i % N_MIXERS, i // N_MIXERS
        mts = [modtabs[i, g] for g in groups]
        g1, b1 = row1(ln1_g[i]), row1(ln1_b[i])
        g2, b2 = row1(ln2_g[i]), row1(ln2_b[i])

        def mixer(xg, mt):
            if kind == 0:
                return _conv_mixer(xg, mt, conv_in_w[j], conv_w[j], conv_out_w[j], g1, b1, alpha, ctx_len)
            if kind == 1:
                return _swa_mixer(xg, mt, swa_qkv_w[j], swa_qkv_b[j], swa_sink[j], swa_out_w[j], swa_out_b[j],
                                  g1, b1, alpha, ctx_len, rope)
            if kind == 2:
                lam_init = 0.8 - 0.6 * math.exp(-0.3 * i)
                return _diff_mixer(xg, mt, diff_qkv_w[j], diff_lambda[j], diff_subln_g[j], diff_out_w[j], lam_init,
                                   g1, b1, alpha, ctx_len, rope)
            return _delta_mixer(xg, mt, delta_qkvz_w[j], delta_ba_w[j], delta_a_log[j], delta_dt_bias[j],
                                delta_conv_w[j], delta_norm_g[j], delta_out_w[j], g1, b1, alpha, ctx_len)

        xs = [mixer(xg, mt) for xg, mt in zip(xs, mts)]
        sts = [_moe_dispatch(xg, mt, router_w[i], router_b[i], ctx_len, last) for xg, mt in zip(xs, mts)]
        zs = [_moe_experts(st, (exp_w1, exp_b1, exp_w2, exp_b2, i)) for st in sts]
        xs = [_moe_combine(xg, mt, st, z, g2, b2, alpha, ctx_len) for xg, mt, st, z in zip(xs, mts, sts, zs)]
    return jnp.concatenate(xs, axis=0)
```

```python
import functools
import math

import jax
import jax.numpy as jnp
from jax import lax
from jax.experimental import pallas as pl
from jax.experimental.pallas import tpu as pltpu
from jax.experimental.pallas import tpu_sc as plsc

F32 = jnp.float32
BF16 = jnp.bfloat16

GRID_W = 64
N_MIXERS = 4
LN_EPS = 1e-5
RMS_EPS = 1e-6
NEG_INF = -1e30
ROPE_BASE = 10000.0
SC_WIDTH = 3
SWA_HEADS = 16
SWA_KV_HEADS = 4
SWA_HEAD_DIM = 64
SWA_WINDOW = 128
DIFF_HEADS = 8
DIFF_HEAD_DIM = 64
DN_QK_HEADS = 8
DN_V_HEADS = 16
DN_HEAD_DIM = 128
DN_CONV = 5
N_EXPERTS = 32
TOP_K = 4
SWIGLU_LIMIT = 7.0
SWIGLU_ALPHA = 1.702

LANES = 128
HALO_ROWS = 16
VMEM_LIMIT = 56 * 1024 * 1024

ROW_TILE = 256
ATT_BLOCK = 128
DIFF_Q_BLOCK = 256
DIFF_HEADS_PER_STEP = 2
DN_CHUNK = 128
EXPERT_TILE = 512
COL_CHUNK = 512
BATCH_GROUPS = 1


def _cparams(*sem):
    return pltpu.CompilerParams(dimension_semantics=sem, vmem_limit_bytes=VMEM_LIMIT)


def _split3(x):
    hi = x.astype(BF16)
    r1 = x - hi.astype(F32)
    mid = r1.astype(BF16)
    lo = (r1 - mid.astype(F32)).astype(BF16)
    return hi, mid, lo


def _dot(a, b):
    return jnp.dot(a, b, preferred_element_type=F32)


def _dot_nt(a, b):
    return lax.dot_general(a, b, (((1,), (1,)), ((), ())), preferred_element_type=F32)


def _dot_tn(a, b):
    return lax.dot_general(a, b, (((0,), (0,)), ((), ())), preferred_element_type=F32)


def _silu(x):
    return x * jax.nn.sigmoid(x)


def _seg_of_block(j, ctx_blocks):
    return jnp.where(j < ctx_blocks, 0, 1)


def _mods_kernel(c_ref, w_ref, b_ref, o_ref):
    s = _silu(c_ref[...]).astype(BF16)
    o_ref[0] = _dot(s, w_ref[0].astype(BF16)) + b_ref[0]


def _mod_vectors(cc, mod_w, mod_b):
    depth, d, n = mod_w.shape
    rows = cc.shape[0]
    tn = 1536
    return pl.pallas_call(
        _mods_kernel,
        grid=(depth, n // tn),
        in_specs=[pl.BlockSpec((rows, d), lambda l, j: (0, 0)),
                  pl.BlockSpec((1, d, tn), lambda l, j: (l, 0, j)),
                  pl.BlockSpec((1, 1, tn), lambda l, j: (l, 0, j))],
        out_specs=pl.BlockSpec((1, rows, tn), lambda l, j: (l, 0, j)),
        out_shape=jax.ShapeDtypeStruct((depth, rows, n), F32),
        compiler_params=_cparams("parallel", "parallel"),
    )(cc, mod_w, mod_b.reshape(depth, 1, n))


def _proj_kernel(*refs, plan, n_out, use_rope):
    x_ref, mod_ref, w_ref, b_ref = refs[:4]
    pos = 4
    if use_rope:
        cos_ref, sin_ref = refs[4:6]
        pos = 6
    outs = refs[pos:pos + n_out]
    shift = mod_ref[0, 0, 0:1, :]
    scale = mod_ref[0, 0, 1:2, :]
    h = (x_ref[0] * (1.0 + scale) + shift).astype(BF16)

    def acc(col, width):
        return _dot(h, w_ref[:, col:col + width]) + b_ref[:, col:col + width]

    for kind, oi, ocol, width, wcol, wcol2, mult in plan:
        if kind == "ones":
            outs[oi][0, :, ocol:ocol + width] = jnp.ones((x_ref.shape[1], width), outs[oi].dtype)
            continue
        for c0 in range(0, width, COL_CHUNK):
            cw = min(COL_CHUNK, width - c0)
            a = acc(wcol + c0, cw)
            if kind == "rope":
                reps = cw // LANES
                cos = jnp.tile(cos_ref[...], (1, reps))
                sin = jnp.tile(sin_ref[...], (1, reps))
                a = (a * cos + acc(wcol2 + c0, cw) * sin) * mult
            elif kind == "mul":
                a = a * acc(wcol2 + c0, cw)
            outs[oi][0, :, ocol + c0:ocol + c0 + cw] = a.astype(outs[oi].dtype)


def _project(x, modtab, w, b, plan, out_widths, out_dtypes, ctx_len, rope=None):
    bsz, t, d = x.shape
    n = w.shape[1]
    tm = ROW_TILE
    ctx_blocks = ctx_len // tm
    mt = modtab
    in_specs = [pl.BlockSpec((1, tm, d), lambda i, j: (i, j, 0)),
                pl.BlockSpec((1, 1, 8, d), lambda i, j: (i, _seg_of_block(j, ctx_blocks), 0, 0)),
                pl.BlockSpec((d, n), lambda i, j: (0, 0)),
                pl.BlockSpec((1, n), lambda i, j: (0, 0))]
    args = [x, mt, w, b]
    if rope is not None:
        in_specs += [pl.BlockSpec((tm, LANES), lambda i, j: (j, 0))] * 2
        args += list(rope)
    out_specs = [pl.BlockSpec((1, tm, ow), lambda i, j: (i, j, 0)) for ow in out_widths]
    out_shape = [jax.ShapeDtypeStruct((bsz, t, ow), dt) for ow, dt in zip(out_widths, out_dtypes)]
    return pl.pallas_call(
        functools.partial(_proj_kernel, plan=tuple(plan), n_out=len(out_widths), use_rope=rope is not None),
        grid=(bsz, t // tm),
        in_specs=in_specs, out_specs=out_specs, out_shape=out_shape,
        compiler_params=_cparams("parallel", "parallel"),
    )(*args)


def _resid_ln(x, y, gate, g, b, alpha):
    r = alpha * x + gate * y
    mu = jnp.mean(r, axis=-1, keepdims=True)
    rc = r - mu
    var = jnp.mean(rc * rc, axis=-1, keepdims=True)
    return rc * lax.rsqrt(var + LN_EPS) * g + b


def _centred_conv(ext_ref, p, prev, nxt, w_ref, width, has_prev, has_next):
    tm = p.shape[0]
    h = HALO_ROWS
    pad = (width - 1) // 2
    ext_ref[0:h, :] = jnp.where(has_prev, prev, 0.0)
    ext_ref[h:h + tm, :] = p
    ext_ref[h + tm:h + tm + h, :] = jnp.where(has_next, nxt, 0.0)
    acc = None
    for k in range(width):
        term = ext_ref[h - pad + k:h - pad + k + tm, :] * w_ref[k:k + 1, :]
        acc = term if acc is None else acc + term
    return acc


def _seg_edges(j, ctx_blocks, n_blocks):
    has_prev = jnp.logical_and(j != 0, j != ctx_blocks)
    has_next = jnp.logical_and(j != ctx_blocks - 1, j != n_blocks - 1)
    return has_prev, has_next


def _out_plain_kernel(x_ref, mod_ref, o_ref, w_ref, b_ref, g_ref, bb_ref, out_ref, *, alpha):
    y = _dot(o_ref[0], w_ref[...]) + b_ref[...]
    out_ref[0] = _resid_ln(x_ref[0], y, mod_ref[0, 0, 2:3, :], g_ref[...], bb_ref[...], alpha)


def _out_conv_kernel(x_ref, mod_ref, bg_ref, p_ref, pp_ref, pn_ref, cw_ref, w_ref, b_ref, g_ref, bb_ref,
                     out_ref, ext_ref, *, alpha, ctx_blocks, n_blocks):
    j = pl.program_id(1)
    has_prev, has_next = _seg_edges(j, ctx_blocks, n_blocks)
    conv = _centred_conv(ext_ref, p_ref[0].astype(F32), pp_ref[0].astype(F32), pn_ref[0].astype(F32), cw_ref,
                         SC_WIDTH, has_prev, has_next)
    o = (bg_ref[0].astype(F32) * conv).astype(BF16)
    y = _dot(o, w_ref[...]) + b_ref[...]
    out_ref[0] = _resid_ln(x_ref[0], y, mod_ref[0, 0, 2:3, :], g_ref[...], bb_ref[...], alpha)


def _out_delta_kernel(x_ref, mod_ref, of_ref, ob_ref, z_ref, ng_ref, w_ref, b_ref, g_ref, bb_ref,
                      out_ref, *, alpha):
    hd = DN_HEAD_DIM
    ng = ng_ref[...]
    acc = None
    for h0 in range(0, of_ref.shape[2], hd):
        o = of_ref[0, :, h0:h0 + hd].astype(F32) + ob_ref[0, :, h0:h0 + hd].astype(F32)
        o = o * lax.rsqrt(jnp.mean(o * o, axis=-1, keepdims=True) + RMS_EPS) * ng
        o = (o * _silu(z_ref[0, :, h0:h0 + hd].astype(F32))).astype(BF16)
        part = _dot(o, w_ref[h0:h0 + hd, :])
        acc = part if acc is None else acc + part
    y = acc + b_ref[...]
    out_ref[0] = _resid_ln(x_ref[0], y, mod_ref[0, 0, 2:3, :], g_ref[...], bb_ref[...], alpha)


def _mixer_out(kind, x, modtab, acts, w_o, b_o, ln_g, ln_b, alpha, ctx_len, extra=None):
    bsz, t, d = x.shape
    tm = ROW_TILE
    nb = t // tm
    ctx_blocks = ctx_len // tm
    kin = w_o.shape[0]
    row = lambda i, j: (i, j, 0)
    const2 = lambda i, j: (0, 0)
    x_spec = pl.BlockSpec((1, tm, d), row)
    mod_spec = pl.BlockSpec((1, 1, 8, d), lambda i, j: (i, _seg_of_block(j, ctx_blocks), 0, 0))
    tail_specs = [pl.BlockSpec((kin, d), const2), pl.BlockSpec((1, d), const2),
                  pl.BlockSpec((1, d), const2), pl.BlockSpec((1, d), const2)]
    tail_args = [w_o, b_o, ln_g, ln_b]
    scratch = []
    if kind == "plain":
        body = functools.partial(_out_plain_kernel, alpha=alpha)
        in_specs = [x_spec, mod_spec, pl.BlockSpec((1, tm, kin), row)] + tail_specs
        args = [x, modtab, acts[0]] + tail_args
    elif kind == "conv":
        body = functools.partial(_out_conv_kernel, alpha=alpha, ctx_blocks=ctx_blocks, n_blocks=nb)
        hb = tm // HALO_ROWS
        last_halo = t // HALO_ROWS - 1
        in_specs = [x_spec, mod_spec, pl.BlockSpec((1, tm, d), row), pl.BlockSpec((1, tm, d), row),
                    pl.BlockSpec((1, HALO_ROWS, d), lambda i, j: (i, jnp.maximum(j * hb - 1, 0), 0)),
                    pl.BlockSpec((1, HALO_ROWS, d), lambda i, j: (i, jnp.minimum((j + 1) * hb, last_halo), 0)),
                    pl.BlockSpec((8, d), const2)] + tail_specs
        args = [x, modtab, acts[0], acts[1], acts[1], acts[1], extra] + tail_args
        scratch = [pltpu.VMEM((tm + 2 * HALO_ROWS, d), F32)]
    else:
        body = functools.partial(_out_delta_kernel, alpha=alpha)
        in_specs = [x_spec, mod_spec, pl.BlockSpec((1, tm, kin), row), pl.BlockSpec((1, tm, kin), row),
                    pl.BlockSpec((1, tm, kin), row), pl.BlockSpec((1, DN_HEAD_DIM), const2)] + tail_specs
        args = [x, modtab, acts[0], acts[1], acts[2], extra] + tail_args
    return pl.pallas_call(
        body, grid=(bsz, nb), in_specs=in_specs,
        out_specs=pl.BlockSpec((1, tm, d), row),
        out_shape=jax.ShapeDtypeStruct((bsz, t, d), F32),
        scratch_shapes=scratch,
        compiler_params=_cparams("parallel", "parallel"),
    )(*args)


def _swa_kernel(sink_ref, q_ref, kp_ref, ko_ref, kn_ref, kc_ref, vp_ref, vo_ref, vn_ref, vc_ref, o_ref,
                *, n_lat_blocks, ctx_blocks):
    j = pl.program_id(1)
    blk = ATT_BLOCK
    lat = j - ctx_blocks
    is_lat = j >= ctx_blocks
    n_keys = 3 * blk + kc_ref.shape[1]
    r = lax.broadcasted_iota(jnp.int32, (blk, n_keys), 0)
    c = lax.broadcasted_iota(jnp.int32, (blk, n_keys), 1)
    ok_prev = jnp.logical_and(is_lat, lat > 0)
    ok_next = jnp.logical_and(is_lat, lat < n_lat_blocks - 1)
    r_max = jnp.where(c < blk, jnp.where(ok_prev, c, -1),
                      jnp.where(c < 2 * blk, jnp.where(is_lat, blk, -1),
                                jnp.where(c < 3 * blk, jnp.where(ok_next, blk, -1), blk)))
    r_min = jnp.where(c < 2 * blk, 0, jnp.where(c < 3 * blk, c - 2 * blk, 0))
    valid = jnp.logical_and(r >= r_min, r <= r_max)
    lane = lax.broadcasted_iota(jnp.int32, (blk, LANES), 1)
    lo = lane < SWA_HEAD_DIM
    group = SWA_HEADS // SWA_KV_HEADS
    kcats, vcats = [], []
    for g in range(SWA_KV_HEADS):
        gs = slice(g * LANES, (g + 1) * LANES)
        kcats.append(jnp.concatenate([kp_ref[0, :, gs], ko_ref[0, :, gs], kn_ref[0, :, gs], kc_ref[0, :, gs]], axis=0))
        vcats.append(jnp.concatenate([vp_ref[0, :, gs], vo_ref[0, :, gs], vn_ref[0, :, gs], vc_ref[0, :, gs]], axis=0))
    qhs, sinks, kv = [], [], []
    for pair in range(SWA_HEADS // 2):
        qp = q_ref[0, :, pair * LANES:(pair + 1) * LANES]
        zero = jnp.zeros_like(qp)
        qhs += [jnp.where(lo, qp, zero), jnp.where(lo, zero, qp)]
        sinks += [sink_ref[2 * pair], sink_ref[2 * pair + 1]]
        kv += [(2 * pair) // group] * 2
    ss = [jnp.where(valid, _dot_nt(qh, kcats[g]), NEG_INF) for qh, g in zip(qhs, kv)]
    ms = [jnp.maximum(jnp.max(s, axis=-1, keepdims=True), sink) for s, sink in zip(ss, sinks)]
    es = [jnp.exp2(s - m) for s, m in zip(ss, ms)]
    invs = [1.0 / (jnp.sum(e, axis=-1, keepdims=True) + jnp.exp2(sink - m)) for e, m, sink in zip(es, ms, sinks)]
    accs = [_dot(e.astype(BF16), vcats[g]) * inv for e, g, inv in zip(es, kv, invs)]
    for pair in range(SWA_HEADS // 2):
        o_ref[0, :, pair * LANES:(pair + 1) * LANES] = jnp.where(lo, accs[2 * pair], accs[2 * pair + 1]).astype(o_ref.dtype)


def _swa_attention(q, kk, vv, sink, ctx_len):
    bsz, t, dq = q.shape
    dk = kk.shape[2]
    blk = ATT_BLOCK
    nb = t // blk
    ctx_blocks = ctx_len // blk
    n_lat_blocks = nb - ctx_blocks

    def prev_map(i, j, s):
        return (i, jnp.clip(j - 1, ctx_blocks, nb - 1), 0)

    def own_map(i, j, s):
        return (i, j, 0)

    def next_map(i, j, s):
        return (i, jnp.clip(j + 1, ctx_blocks, nb - 1), 0)

    def ctx_map(i, j, s):
        return (i, 0, 0)

    kv_specs = [pl.BlockSpec((1, blk, dk), prev_map), pl.BlockSpec((1, blk, dk), own_map),
                pl.BlockSpec((1, blk, dk), next_map), pl.BlockSpec((1, ctx_len, dk), ctx_map)]
    return pl.pallas_call(
        functools.partial(_swa_kernel, n_lat_blocks=n_lat_blocks, ctx_blocks=ctx_blocks),
        grid_spec=pltpu.PrefetchScalarGridSpec(
            num_scalar_prefetch=1, grid=(bsz, nb),
            in_specs=[pl.BlockSpec((1, blk, dq), own_map)] + kv_specs + kv_specs,
            out_specs=pl.BlockSpec((1, blk, dq), own_map)),
        out_shape=jax.ShapeDtypeStruct((bsz, t, dq), BF16),
        compiler_params=_cparams("parallel", "parallel"),
    )(sink, q, kk, kk, kk, kk, vv, vv, vv, vv)


def _diff_kernel(lam_ref, q_ref, k_ref, v_ref, g_ref, o_ref, *, ctx_len, lam_init):
    j = pl.program_id(2)
    blk = DIFF_Q_BLOCK
    ctx_blocks = ctx_len // blk
    lp = lam_ref[...]
    lam = (jnp.exp(jnp.sum(lp[0:1, :] * lp[1:2, :], axis=-1, keepdims=True))
           - jnp.exp(jnp.sum(lp[2:3, :] * lp[3:4, :], axis=-1, keepdims=True)) + lam_init)
    lane = lax.broadcasted_iota(jnp.int32, (blk, LANES), 1)
    lo = lane < DIFF_HEAD_DIM
    n_heads = q_ref.shape[2] // LANES

    def attend(n_keys):
        qhs, kv = [], []
        for h in range(n_heads):
            qp = q_ref[0, :, h * LANES:(h + 1) * LANES]
            zero = jnp.zeros_like(qp)
            qhs += [jnp.where(lo, qp, zero), jnp.where(lo, zero, qp)]
            kv += [h, h]
        ss = [_dot_nt(qh, k_ref[0, 0:n_keys, h * LANES:(h + 1) * LANES]) for qh, h in zip(qhs, kv)]
        es = [jnp.exp2(s - jnp.max(s, axis=-1, keepdims=True)) for s in ss]
        pvs = [_dot(e.astype(BF16), v_ref[0, 0:n_keys, 2 * h * LANES:2 * (h + 1) * LANES]) for e, h in zip(es, kv)]
        invs = [1.0 / pv[:, LANES:LANES + 1] for pv in pvs]
        for h in range(n_heads):
            o = pvs[2 * h][:, :LANES] * invs[2 * h] - (lam * invs[2 * h + 1]) * pvs[2 * h + 1][:, :LANES]
            o = o * lax.rsqrt(jnp.mean(o * o, axis=-1, keepdims=True) + RMS_EPS) * g_ref[...]
            o_ref[0, :, h * LANES:(h + 1) * LANES] = (o * (1.0 - lam_init)).astype(o_ref.dtype)

    @pl.when(j < ctx_blocks)
    def _():
        attend(ctx_len)

    @pl.when(j >= ctx_blocks)
    def _():
        attend(k_ref.shape[1])


def _diff_attention(q, k, v, lam_p, subln_g, ctx_len, lam_init):
    bsz, t, dq = q.shape
    blk = DIFF_Q_BLOCK
    nh = dq // LANES
    hs = DIFF_HEADS_PER_STEP
    qmap = lambda i, h, j: (i, j, h)
    kmap = lambda i, h, j: (i, 0, h)
    const2 = lambda i, h, j: (0, 0)
    return pl.pallas_call(
        functools.partial(_diff_kernel, ctx_len=ctx_len, lam_init=lam_init),
        grid=(bsz, nh // hs, t // blk),
        in_specs=[pl.BlockSpec((8, DIFF_HEAD_DIM), const2),
                  pl.BlockSpec((1, blk, hs * LANES), qmap),
                  pl.BlockSpec((1, t, hs * LANES), kmap),
                  pl.BlockSpec((1, t, 2 * hs * LANES), kmap),
                  pl.BlockSpec((1, LANES), const2)],
        out_specs=pl.BlockSpec((1, blk, hs * LANES), qmap),
        out_shape=jax.ShapeDtypeStruct((bsz, t, dq), BF16),
        compiler_params=_cparams("parallel", "parallel", "parallel"),
    )(jnp.pad(lam_p, ((0, 4), (0, 0))), q, k, v, subln_g.reshape(1, LANES))


def _delta_prep_kernel(x_ref, xp_ref, xn_ref, cw_ref, ba_ref, alog_ref, dtb_ref,
                       q_ref, k_ref, v_ref, gb_ref, ext_ref, *, ctx_blocks, n_blocks, nqk):
    j = pl.program_id(1)
    has_prev, has_next = _seg_edges(j, ctx_blocks, n_blocks)
    hd = DN_HEAD_DIM
    tm = x_ref.shape[1]
    for c0 in range(0, x_ref.shape[2], COL_CHUNK):
        cs = slice(c0, c0 + COL_CHUNK)
        conv = _centred_conv(ext_ref, x_ref[0, :, cs].astype(F32), xp_ref[0, :, cs].astype(F32),
                             xn_ref[0, :, cs].astype(F32), cw_ref.at[:, cs], DN_CONV, has_prev, has_next)
        a = _silu(conv)
        for h0 in range(0, COL_CHUNK, hd):
            col = c0 + h0
            ah = a[:, h0:h0 + hd]
            if col < 2 * nqk:
                ah = ah * lax.rsqrt(jnp.sum(ah * ah, axis=-1, keepdims=True) + RMS_EPS)
                if col < nqk:
                    q_ref[0, :, col:col + hd] = (ah * (hd ** -0.5)).astype(q_ref.dtype)
                else:
                    k_ref[0, :, col - nqk:col - nqk + hd] = ah.astype(k_ref.dtype)
            else:
                v_ref[0, :, col - 2 * nqk:col - 2 * nqk + hd] = ah.astype(v_ref.dtype)
    ba = ba_ref[0]
    lane = lax.broadcasted_iota(jnp.int32, ba.shape, 1)
    is_beta = (lane % 32) < DN_V_HEADS
    z = ba + dtb_ref[...]
    softplus = jnp.maximum(z, 0.0) + jnp.log(1.0 + jnp.exp(-jnp.abs(z)))
    g = -jnp.exp(alog_ref[...]) * softplus
    r = lax.broadcasted_iota(jnp.int32, (tm, tm), 0)
    c = lax.broadcasted_iota(jnp.int32, (tm, tm), 1)
    same = (r // DN_CHUNK) == (c // DN_CHUNK)
    tri_f = jnp.where(jnp.logical_and(same, c <= r), 1.0, 0.0).astype(BF16)
    tri_b = jnp.where(jnp.logical_and(same, c >= r), 1.0, 0.0).astype(BF16)
    parts = _split3(g)
    cum_f = _dot(tri_f, parts[0]) + _dot(tri_f, parts[1]) + _dot(tri_f, parts[2])
    cum_b = _dot(tri_b, parts[0]) + _dot(tri_b, parts[1]) + _dot(tri_b, parts[2])
    gcum = jnp.where(lane < 32, cum_f, cum_b)
    gb_ref[0] = jnp.where(is_beta, jax.nn.sigmoid(ba), gcum)


def _delta_prep(qkv_pre, conv_w, ba, alog_l, dtb_l, ctx_len, nqk, nv):
    bsz, t, nc = qkv_pre.shape
    tm = ROW_TILE
    nb = t // tm
    ctx_blocks = ctx_len // tm
    hb = tm // HALO_ROWS
    last_halo = t // HALO_ROWS - 1
    row = lambda i, j: (i, j, 0)
    const2 = lambda i, j: (0, 0)
    return pl.pallas_call(
        functools.partial(_delta_prep_kernel, ctx_blocks=ctx_blocks, n_blocks=nb, nqk=nqk),
        grid=(bsz, nb),
        in_specs=[pl.BlockSpec((1, tm, nc), row),
                  pl.BlockSpec((1, HALO_ROWS, nc), lambda i, j: (i, jnp.maximum(j * hb - 1, 0), 0)),
                  pl.BlockSpec((1, HALO_ROWS, nc), lambda i, j: (i, jnp.minimum((j + 1) * hb, last_halo), 0)),
                  pl.BlockSpec((8, nc), const2),
                  pl.BlockSpec((1, tm, LANES), row),
                  pl.BlockSpec((1, LANES), const2),
                  pl.BlockSpec((1, LANES), const2)],
        out_specs=[pl.BlockSpec((1, tm, nqk), row), pl.BlockSpec((1, tm, nqk), row),
                   pl.BlockSpec((1, tm, nv), row), pl.BlockSpec((1, tm, LANES), row)],
        out_shape=[jax.ShapeDtypeStruct((bsz, t, nqk), BF16), jax.ShapeDtypeStruct((bsz, t, nqk), BF16),
                   jax.ShapeDtypeStruct((bsz, t, nv), BF16), jax.ShapeDtypeStruct((bsz, t, LANES), F32)],
        scratch_shapes=[pltpu.VMEM((tm + 2 * HALO_ROWS, COL_CHUNK), F32)],
        compiler_params=_cparams("parallel", "parallel"),
    )(qkv_pre, qkv_pre, qkv_pre, conv_w, ba, alog_l, dtb_l)


DN_INV_BLOCK = 16


def _unit_triangular_inverses(lows, eye, r, c):
    n = lows[0].shape[0]
    b = DN_INV_BLOCK
    same = (r // b) == (c // b)
    pws = [jnp.where(same, low, 0.0) for low in lows]
    xs = [eye - pw for pw in pws]
    for _ in range(int(math.log2(b)) - 1):
        pbs = [pw.astype(BF16) for pw in pws]
        pws = [_dot(pb, pb) for pb in pbs]
        xs = [x + _dot(x.astype(BF16), pw.astype(BF16)) for x, pw in zip(xs, pws)]
    while b < n:
        pair = jnp.logical_and((r // (2 * b)) == (c // (2 * b)), (r // b) != (c // b))
        xbs = [x.astype(BF16) for x in xs]
        ys = [_dot(jnp.where(pair, low, 0.0).astype(BF16), xb).astype(BF16) for low, xb in zip(lows, xbs)]
        xs = [x - _dot(xb, y) for x, xb, y in zip(xs, xbs, ys)]
        b *= 2
    return xs


DN_CHUNK_QK_HEADS = 2
DN_SCAN_HEADS = 16


def _delta_chunk_kernel(q_ref, k_ref, v_ref, gb_ref, gt_ref, u_ref, w_ref, qg_ref, kd_ref, a_ref, *, rep, nhq):
    hq0 = pl.program_id(1) * nhq
    ck = DN_CHUNK
    hd = DN_HEAD_DIM
    r = lax.broadcasted_iota(jnp.int32, (ck, ck), 0)
    c = lax.broadcasted_iota(jnp.int32, (ck, ck), 1)
    eye = jnp.where(r == c, 1.0, 0.0)
    lane = lax.broadcasted_iota(jnp.int32, (ck, LANES), 1)
    lows, rhss, dests = [], [], []
    for sub, hh in [(sub, hh) for sub in range(q_ref.shape[1] // ck) for hh in range(nhq)]:
        rows = slice(sub * ck, (sub + 1) * ck)
        q = q_ref[0, rows, hh * hd:(hh + 1) * hd]
        k = k_ref[0, rows, hh * hd:(hh + 1) * hd]
        kk = _dot_nt(k, k)
        qk = _dot_nt(q, k)
        gb = gb_ref[0, rows, :]
        qf = q.astype(F32)
        kf = k.astype(F32)
        for d in range(2):
            incl = (r >= c) if d == 0 else (r <= c)
            strict = (r > c) if d == 0 else (r < c)
            for jh in range(rep):
                head = (hq0 + hh) * rep + jh
                cols = slice((hh * rep + jh) * hd, (hh * rep + jh + 1) * hd)
                bcol = jnp.sum(jnp.where(lane == d * 32 + head, gb, 0.0), axis=-1, keepdims=True)
                gcol = jnp.sum(jnp.where(lane == d * 32 + DN_V_HEADS + head, gb, 0.0), axis=-1, keepdims=True)
                grow = gt_ref[d, 0, hh, jh:jh + 1, rows]
                glast = grow[:, ck - 1:ck] if d == 0 else grow[:, 0:1]
                decay = jnp.where(incl, jnp.exp(jnp.where(incl, gcol - grow, 0.0)), 0.0)
                lows.append(jnp.where(strict, kk * decay, 0.0) * bcol)
                v = v_ref[0, rows, cols].astype(F32)
                eg = jnp.exp(gcol)
                rhss.append(jnp.concatenate([v * bcol, kf * (bcol * eg)], axis=1).astype(BF16))
                dests.append((d, rows, cols))
                qg_ref[d, 0, rows, cols] = (qf * eg).astype(qg_ref.dtype)
                kd_ref[d, 0, rows, cols] = (kf * jnp.exp(glast - gcol)).astype(kd_ref.dtype)
                a_ref[d, 0, rows, cols] = jnp.where(incl, qk * decay, 0.0).astype(a_ref.dtype)
    tinvs = _unit_triangular_inverses(lows, eye, r, c)
    uws = [_dot(tinv.astype(BF16), rhs) for tinv, rhs in zip(tinvs, rhss)]
    for uw, (d, rows, cols) in zip(uws, dests):
        u_ref[d, 0, rows, cols] = uw[:, :hd].astype(u_ref.dtype)
        w_ref[d, 0, rows, cols] = uw[:, hd:].astype(w_ref.dtype)


def _delta_chunks(q, k, v, gb, gt_a):
    bsz, t, nqk = q.shape
    nv = v.shape[2]
    hd = DN_HEAD_DIM
    nh = nqk // hd
    rep = nv // nqk
    tm = ROW_TILE
    nhq = DN_CHUNK_QK_HEADS
    qmap = lambda i, h, j: (i, j, h)
    omap = lambda i, h, j: (0, i, j, h)
    oshape = lambda dt: jax.ShapeDtypeStruct((2, bsz, t, nv), dt)
    ospec = pl.BlockSpec((2, 1, tm, nhq * rep * hd), omap)
    return pl.pallas_call(
        functools.partial(_delta_chunk_kernel, rep=rep, nhq=nhq),
        grid=(bsz, nh // nhq, t // tm),
        in_specs=[pl.BlockSpec((1, tm, nhq * hd), qmap), pl.BlockSpec((1, tm, nhq * hd), qmap),
                  pl.BlockSpec((1, tm, nhq * rep * hd), qmap),
                  pl.BlockSpec((1, tm, LANES), lambda i, h, j: (i, j, 0)),
                  pl.BlockSpec((2, 1, nhq, 8, tm), lambda i, h, j: (0, i, h, 0, j))],
        out_specs=[ospec] * 5,
        out_shape=[oshape(BF16)] * 5,
        compiler_params=_cparams("parallel", "parallel", "parallel"),
    )(q, k, v, gb, gt_a)


def _scan_chunk_index(d, i, ctx_chunks, n_chunks):
    back = jnp.where(i < ctx_chunks, ctx_chunks - 1 - i, n_chunks - 1 - (i - ctx_chunks))
    return jnp.where(d == 0, i, back)


def _delta_scan_kernel(u_ref, w_ref, qg_ref, kd_ref, a_ref, gt_ref, o_ref, s_ref, *, n_heads):
    d = pl.program_id(0)
    i = pl.program_id(3)
    ck = DN_CHUNK
    hd = DN_HEAD_DIM

    @pl.when(i == 0)
    def _():
        s_ref[...] = jnp.zeros_like(s_ref)

    heads = range(n_heads)
    cols = [slice(jh * hd, (jh + 1) * hd) for jh in heads]
    grows = gt_ref[0, 0, 0]
    glast = jnp.where(d == 0, grows[:, ck - 1:ck], grows[:, 0:1])
    carry = jnp.exp(glast)
    ss = [s_ref[jh] for jh in heads]
    wss = [_dot(jnp.concatenate([w_ref[0, 0, :, cs], qg_ref[0, 0, :, cs]], axis=0), s.astype(BF16))
           for cs, s in zip(cols, ss)]
    v_news = [(u_ref[0, 0, :, cs] - ws[:ck]).astype(BF16) for cs, ws in zip(cols, wss)]
    outs = [ws[ck:] + _dot(a_ref[0, 0, :, cs], vn) for cs, ws, vn in zip(cols, wss, v_news)]
    upds = [_dot_tn(kd_ref[0, 0, :, cs], vn) for cs, vn in zip(cols, v_news)]
    for jh in heads:
        o_ref[0, 0, :, cols[jh]] = outs[jh].astype(o_ref.dtype)
        s_ref[jh] = ss[jh] * carry[jh:jh + 1, :] + upds[jh]


def _delta_scan(u, w, qg, kd, a, gt_b, ctx_len):
    _, bsz, t, nv = u.shape
    ck = DN_CHUNK
    hd = DN_HEAD_DIM
    nh = DN_SCAN_HEADS
    groups = nv // (nh * hd)
    n_chunks = t // ck
    ctx_chunks = ctx_len // ck
    cmap = lambda d, b, g, i: (d, b, _scan_chunk_index(d, i, ctx_chunks, n_chunks), g)
    spec = pl.BlockSpec((1, 1, ck, nh * hd), cmap)
    return pl.pallas_call(
        functools.partial(_delta_scan_kernel, n_heads=nh),
        grid=(2, bsz, groups, n_chunks),
        in_specs=[spec] * 5 + [pl.BlockSpec((1, 1, 1, nh, ck),
                                            lambda d, b, g, i: (d, b, g, 0, _scan_chunk_index(d, i, ctx_chunks, n_chunks)))],
        out_specs=spec,
        out_shape=jax.ShapeDtypeStruct((2, bsz, t, nv), BF16),
        scratch_shapes=[pltpu.VMEM((nh, hd, hd), F32)],
        compiler_params=_cparams("parallel", "parallel", "parallel", "arbitrary"),
    )(u, w, qg, kd, a, gt_b)


def _pack_bf16_pairs(a):
    n = a.shape[1] // 2
    bits = lax.bitcast_convert_type(a, jnp.uint32)
    return (bits[:, :n] & jnp.uint32(0xFFFF0000)) | (bits[:, n:] >> 16)


def _unpack_bf16_pairs(p):
    hi = lax.bitcast_convert_type(p & jnp.uint32(0xFFFF0000), F32)
    lo = lax.bitcast_convert_type(p << 16, F32)
    return jnp.concatenate([hi, lo], axis=1)


def _router_kernel(x_ref, mod_ref, whi_ref, wlo_ref, br_ref, ha_ref, hb_ref, idx_ref, wt_ref, cnt_ref, run_ref):
    first = jnp.logical_and(pl.program_id(0) == 0, pl.program_id(1) == 0)

    @pl.when(first)
    def _():
        run_ref[...] = jnp.zeros_like(run_ref)

    shift = mod_ref[0, 0, 3:4, :]
    scale = mod_ref[0, 0, 4:5, :]
    h = x_ref[0] * (1.0 + scale) + shift
    h_hi = h.astype(BF16)
    packed = _pack_bf16_pairs(h_hi.astype(F32))
    ha_ref[0] = packed[:, :SC_ROW_WORDS]
    hb_ref[0] = packed[:, SC_ROW_WORDS:]
    h_lo = (h - h_hi.astype(F32)).astype(BF16)
    w_hi = whi_ref[...]
    w_lo = wlo_ref[...]
    logits = _dot(h_hi, w_hi) + _dot(h_hi, w_lo) + _dot(h_lo, w_hi) + br_ref[...]
    lane = lax.broadcasted_iota(jnp.int32, logits.shape, 1).astype(F32)
    cur = jnp.where(lane < N_EXPERTS, logits, -jnp.inf)
    idx_out = jnp.zeros(logits.shape, F32)
    wt_out = jnp.zeros(logits.shape, F32)
    tops, sels = [], []
    hits = jnp.zeros(logits.shape, F32)
    for kk in range(TOP_K):
        m = jnp.max(cur, axis=-1, keepdims=True)
        sel = jnp.min(jnp.where(cur == m, lane, float(LANES)), axis=-1, keepdims=True)
        tops.append(m)
        sels.append(sel)
        idx_out = jnp.where(lane == kk, sel, idx_out)
        hits = jnp.where(lane == sel, 1.0, hits)
        cur = jnp.where(lane == sel, -jnp.inf, cur)
    es = [jnp.exp(m - tops[0]) for m in tops]
    den = es[0] + es[1] + es[2] + es[3]
    for kk in range(TOP_K):
        wt_out = jnp.where(lane == kk, es[kk] / den, wt_out)
    tm = logits.shape[0]
    r = lax.broadcasted_iota(jnp.int32, (tm, tm), 0)
    c = lax.broadcasted_iota(jnp.int32, (tm, tm), 1)
    before = _dot(jnp.where(c < r, 1.0, 0.0).astype(BF16), hits.astype(BF16)) + run_ref[...]
    for kk in range(TOP_K):
        rank = jnp.sum(jnp.where(lane == sels[kk], before, 0.0), axis=-1, keepdims=True)
        idx_out = jnp.where(lane == TOP_K + kk, rank, idx_out)
    total = run_ref[...] + jnp.sum(hits, axis=0, keepdims=True)
    run_ref[...] = total
    cnt_ref[...] = jnp.broadcast_to(total, cnt_ref.shape)
    idx_ref[0] = idx_out.astype(jnp.int32)
    wt_ref[0] = wt_out


def _router(x, modtab, w_hi, w_lo, b_r, ctx_len, first_block):
    bsz, t, d = x.shape
    tm = ROW_TILE
    ctx_blocks = ctx_len // tm
    nb = t // tm - first_block
    row_in = lambda i, j: (i, j + first_block, 0)
    row = lambda i, j: (i, j, 0)
    const2 = lambda i, j: (0, 0)
    tout = nb * tm
    return pl.pallas_call(
        _router_kernel,
        grid=(bsz, nb),
        in_specs=[pl.BlockSpec((1, tm, d), row_in),
                  pl.BlockSpec((1, 1, 8, d), lambda i, j: (i, _seg_of_block(j + first_block, ctx_blocks), 0, 0)),
                  pl.BlockSpec((d, LANES), const2), pl.BlockSpec((d, LANES), const2),
                  pl.BlockSpec((1, LANES), const2)],
        out_specs=[pl.BlockSpec((1, tm, SC_ROW_WORDS), row), pl.BlockSpec((1, tm, SC_ROW_WORDS), row),
                   pl.BlockSpec((1, tm, LANES), row),
                   pl.BlockSpec((1, tm, LANES), row), pl.BlockSpec((8, LANES), const2)],
        out_shape=[jax.ShapeDtypeStruct((bsz, tout, SC_ROW_WORDS), jnp.uint32),
                   jax.ShapeDtypeStruct((bsz, tout, SC_ROW_WORDS), jnp.uint32),
                   jax.ShapeDtypeStruct((bsz, tout, LANES), jnp.int32),
                   jax.ShapeDtypeStruct((bsz, tout, LANES), F32),
                   jax.ShapeDtypeStruct((8, LANES), F32)],
        scratch_shapes=[pltpu.VMEM((1, LANES), F32)],
        compiler_params=_cparams("arbitrary", "arbitrary"),
    )(x, modtab, w_hi, w_lo, b_r)


def _expert_kernel(be_ref, nused_ref, xa_ref, xb_ref, w1_ref, b1_ref, w2_ref, b2_ref, ya_ref, yb_ref,
                   w1b_ref, w2b_ref):
    i = pl.program_id(0)
    dff = w2_ref.shape[1]
    prev = be_ref[jnp.maximum(i - 1, 0)]
    new_expert = jnp.logical_or(i == 0, be_ref[i] != prev)

    @pl.when(jnp.logical_and(new_expert, i < nused_ref[0]))
    def _():
        for r0 in range(0, w1_ref.shape[1], LANES):
            w1b_ref[r0:r0 + LANES, :] = w1_ref[0, r0:r0 + LANES, :].astype(BF16)
        for r0 in range(0, dff, LANES):
            w2b_ref[r0:r0 + LANES, :] = w2_ref[0, r0:r0 + LANES, :].astype(BF16)

    @pl.when(i < nused_ref[0])
    def _():
        x = _unpack_bf16_pairs(jnp.concatenate([xa_ref[...], xb_ref[...]], axis=1)).astype(BF16)
        chunks = range(0, dff, COL_CHUNK)
        gates = [_dot(x, w1b_ref[:, c0:c0 + COL_CHUNK]) + b1_ref[0, :, c0:c0 + COL_CHUNK] for c0 in chunks]
        ups = [_dot(x, w1b_ref[:, dff + c0:dff + c0 + COL_CHUNK]) + b1_ref[0, :, dff + c0:dff + c0 + COL_CHUNK]
               for c0 in chunks]
        gates = [jnp.minimum(gate, SWIGLU_LIMIT) for gate in gates]
        ups = [jnp.clip(up, -SWIGLU_LIMIT, SWIGLU_LIMIT) for up in ups]
        acts = [((up + 1.0) * gate * jax.nn.sigmoid(SWIGLU_ALPHA * gate)).astype(BF16) for gate, up in zip(gates, ups)]
        parts = [_dot(act, w2b_ref[c0:c0 + COL_CHUNK, :]) for act, c0 in zip(acts, chunks)]
        acc = parts[0]
        for part in parts[1:]:
            acc = acc + part
        y = (acc + b2_ref[0]).astype(BF16).astype(F32)
        packed = _pack_bf16_pairs(y)
        ya_ref[...] = packed[:, :SC_ROW_WORDS]
        yb_ref[...] = packed[:, SC_ROW_WORDS:]

    @pl.when(i >= nused_ref[0])
    def _():
        ya_ref[...] = jnp.zeros_like(ya_ref)
        yb_ref[...] = jnp.zeros_like(yb_ref)


def _experts(xa, xb, block_e, n_used, w1, b1, w2, b2, layer):
    n_rows, dq = xa.shape
    d = 4 * dq
    tm = EXPERT_TILE
    depth, ne, _, dff2 = w1.shape
    dff = dff2 // 2
    w1 = w1.reshape(depth * ne, d, dff2)
    w2 = w2.reshape(depth * ne, dff, d)
    b1 = b1.reshape(depth * ne, 1, dff2)
    b2 = b2.reshape(depth * ne, 1, d)
    emap = lambda i, be, nu: (layer * ne + be[i], 0, 0)
    return pl.pallas_call(
        _expert_kernel,
        grid_spec=pltpu.PrefetchScalarGridSpec(
            num_scalar_prefetch=2, grid=(n_rows // tm,),
            in_specs=[pl.BlockSpec((tm, dq), lambda i, be, nu: (i, 0)),
                      pl.BlockSpec((tm, dq), lambda i, be, nu: (i, 0)),
                      pl.BlockSpec((1, d, dff2), emap),
                      pl.BlockSpec((1, 1, dff2), emap),
                      pl.BlockSpec((1, dff, d), emap),
                      pl.BlockSpec((1, 1, d), emap)],
            out_specs=[pl.BlockSpec((tm, dq), lambda i, be, nu: (i, 0))] * 2,
            scratch_shapes=[pltpu.VMEM((d, dff2), BF16), pltpu.VMEM((dff, d), BF16)]),
        out_shape=[jax.ShapeDtypeStruct((n_rows, dq), jnp.uint32)] * 2,
        compiler_params=_cparams("arbitrary"),
    )(block_e, n_used, xa, xb, w1, b1, w2, b2)


def _combine_kernel(x_ref, mod_ref, za_ref, zb_ref, wt_ref, g_ref, b_ref, out_ref, *, alpha):
    wt = wt_ref[0]
    f = None
    for kk in range(TOP_K):
        term = _unpack_bf16_pairs(jnp.concatenate([za_ref[kk, 0], zb_ref[kk, 0]], axis=1)) * wt[:, kk:kk + 1]
        f = term if f is None else f + term
    out_ref[0] = _resid_ln(x_ref[0], f, mod_ref[0, 0, 5:6, :], g_ref[...], b_ref[...], alpha)


def _combine(x, modtab, za, zb, wt, ln_g, ln_b, alpha, ctx_len, first_block):
    bsz, t, d = x.shape
    tm = ROW_TILE
    ctx_blocks = ctx_len // tm
    nb = t // tm - first_block
    row_in = lambda i, j: (i, j + first_block, 0)
    row = lambda i, j: (i, j, 0)
    const2 = lambda i, j: (0, 0)
    return pl.pallas_call(
        functools.partial(_combine_kernel, alpha=alpha),
        grid=(bsz, nb),
        in_specs=[pl.BlockSpec((1, tm, d), row_in),
                  pl.BlockSpec((1, 1, 8, d), lambda i, j: (i, _seg_of_block(j + first_block, ctx_blocks), 0, 0)),
                  pl.BlockSpec((TOP_K, 1, tm, SC_ROW_WORDS), lambda i, j: (0, i, j, 0)),
                  pl.BlockSpec((TOP_K, 1, tm, SC_ROW_WORDS), lambda i, j: (0, i, j, 0)),
                  pl.BlockSpec((1, tm, LANES), row),
                  pl.BlockSpec((1, d), const2), pl.BlockSpec((1, d), const2)],
        out_specs=pl.BlockSpec((1, tm, d), row),
        out_shape=jax.ShapeDtypeStruct((bsz, nb * tm, d), F32),
        compiler_params=_cparams("parallel", "parallel"),
    )(x, modtab, za, zb, wt, ln_g, ln_b)


def _moe_dispatch(x, modtab, w_r, b_r, ctx_len, latent_only):
    bsz, t, d = x.shape
    first_block = ctx_len // ROW_TILE if latent_only else 0
    wr_pad = jnp.pad(w_r, ((0, 0), (0, LANES - N_EXPERTS)))
    br_pad = jnp.pad(b_r, (0, LANES - N_EXPERTS)).reshape(1, LANES)
    wr_hi = wr_pad.astype(BF16)
    wr_lo = (wr_pad - wr_hi.astype(F32)).astype(BF16)
    ha, hb, idx, wt, cnt = _router(x, modtab, wr_hi, wr_lo, br_pad, ctx_len, first_block)
    tr = ha.shape[1]
    dq = ha.shape[2]
    n_tok = bsz * tr
    n_assign = n_tok * TOP_K
    tm = EXPERT_TILE
    experts = jnp.arange(N_EXPERTS, dtype=jnp.int32)
    counts = cnt[0, :N_EXPERTS].astype(jnp.int32)
    padded = (counts + tm - 1) // tm * tm
    pends = jnp.cumsum(padded)
    pstarts = pends - padded
    e_tk = idx[:, :, :TOP_K].reshape(n_tok, TOP_K)
    rank = idx[:, :, TOP_K:2 * TOP_K].reshape(n_tok, TOP_K)
    dest = jnp.sum(jnp.where(e_tk[:, :, None] == experts, pstarts, 0), axis=-1) + rank
    n_rows = (-(-n_assign // tm) + N_EXPERTS) * tm
    n_blocks = n_rows // tm
    starts = jnp.arange(n_blocks, dtype=jnp.int32) * tm
    block_e = jnp.minimum(jnp.sum((pends[None, :] <= starts[:, None]).astype(jnp.int32), axis=1), N_EXPERTS - 1)
    n_used = (pends[-1] // tm).astype(jnp.int32).reshape(1)
    dest_t = dest.T
    xa = _sc_scatter_rows(ha.reshape(n_tok, dq), dest_t, n_rows)
    xb = _sc_scatter_rows(hb.reshape(n_tok, dq), dest_t, n_rows)
    return dict(xa=xa, xb=xb, block_e=block_e, n_used=n_used, dest_t=dest_t, wt=wt, first_block=first_block,
                shape=(bsz, tr, dq))


def _moe_experts(st, expert_params):
    bsz, tr, dq = st["shape"]
    ya, yb = _experts(st["xa"], st["xb"], st["block_e"], st["n_used"], *expert_params)
    flat = st["dest_t"].reshape(1, bsz * tr * TOP_K)
    za = _sc_gather_rows(ya, flat).reshape(TOP_K, bsz, tr, dq)
    zb = _sc_gather_rows(yb, flat).reshape(TOP_K, bsz, tr, dq)
    return za, zb


def _moe_combine(x, modtab, st, z, ln_g, ln_b, alpha, ctx_len):
    return _combine(x, modtab, z[0], z[1], st["wt"], ln_g, ln_b, alpha, ctx_len, st["first_block"])


SC_WINDOW = 128
SC_ROW_WORDS = 256


def _sc_mesh():
    return plsc.VectorSubcoreMesh(core_axis_name="c", subcore_axis_name="s")


def _sc_scatter_rows(src, idx, n_out):
    n_src, width = src.shape
    n_k = idx.shape[0]
    per_k = n_src // SC_WINDOW

    @functools.partial(pl.kernel, out_type=jax.ShapeDtypeStruct((n_out, width), src.dtype), mesh=_sc_mesh())
    def scatter(x_hbm, i_hbm, o_hbm):
        def body(x_vmem, i_vmem):
            pltpu.sync_copy(x_vmem, o_hbm.at[i_vmem.at[0]])

        pltpu.emit_pipeline(
            body, grid=(n_k * per_k,),
            in_specs=[pl.BlockSpec((SC_WINDOW, width), lambda i: (i // n_k, 0)),
                      pl.BlockSpec((1, SC_WINDOW), lambda i: (i % n_k, i // n_k))],
            out_specs=[],
            core_axis_name=("c", "s"),
            dimension_semantics=(pltpu.PARALLEL,),
        )(x_hbm, i_hbm)

    return scatter(src, idx)


def _sc_gather_rows(table, idx):
    width = table.shape[1]
    n_idx = idx.shape[1]

    @functools.partial(pl.kernel, out_type=jax.ShapeDtypeStruct((n_idx, width), table.dtype), mesh=_sc_mesh())
    def gather(x_hbm, i_hbm, o_hbm):
        def body(i_vmem, o_vmem):
            pltpu.sync_copy(x_hbm.at[i_vmem.at[0]], o_vmem)

        pltpu.emit_pipeline(
            body, grid=(n_idx // SC_WINDOW,),
            in_specs=[pl.BlockSpec((1, SC_WINDOW), lambda i: (0, i))],
            out_specs=[pl.BlockSpec((SC_WINDOW, width), lambda i: (i, 0))],
            core_axis_name=("c", "s"),
            dimension_semantics=(pltpu.PARALLEL,),
        )(i_hbm, o_hbm)

    return gather(table, idx)


def _rot_cols(w, head_dim):
    lead = w.shape[:-1]
    q = head_dim // 4
    wr = w.reshape(lead + (-1, 4, q))
    out = jnp.stack([-wr[..., 1, :], wr[..., 0, :], -wr[..., 3, :], wr[..., 2, :]], axis=-2)
    return out.reshape(w.shape)


def _dup_heads(w, head_dim):
    lead = w.shape[:-1]
    wr = w.reshape(lead + (-1, 1, head_dim))
    return jnp.concatenate([wr, wr], axis=-2).reshape(lead + (-1,))


def _rope_tables(n_lat, ctx_len, head_dim):
    rows = n_lat // GRID_W
    row = jnp.repeat(jnp.arange(rows, dtype=F32), GRID_W)
    col = jnp.tile(jnp.arange(GRID_W, dtype=F32), rows)
    half = head_dim // 2
    inv = ROPE_BASE ** (-jnp.arange(0, half, 2, dtype=F32) / half)
    ar = row[:, None] * inv
    ac = col[:, None] * inv
    ang = jnp.concatenate([ar, ar, ac, ac], axis=-1)
    cos = jnp.concatenate([jnp.ones((ctx_len, head_dim), F32), jnp.cos(ang)], axis=0)
    sin = jnp.concatenate([jnp.zeros((ctx_len, head_dim), F32), jnp.sin(ang)], axis=0)
    reps = LANES // head_dim
    return jnp.tile(cos, (1, reps)), jnp.tile(sin, (1, reps))


def _conv_mixer(x, modtab, w_in, w_conv, w_out, ln_g, ln_b, alpha, ctx_len):
    d = x.shape[2]
    plan = [("plain", 0, 0, d, 0, None, 1.0), ("mul", 1, 0, d, d, 2 * d, 1.0)]
    bg, p = _project(x, modtab, w_in.astype(BF16), jnp.zeros((1, 3 * d), F32), plan, (d, d), (BF16, BF16), ctx_len)
    cw = jnp.pad(w_conv, ((0, 8 - SC_WIDTH), (0, 0)))
    return _mixer_out("conv", x, modtab, (bg, p), w_out.astype(BF16), jnp.zeros((1, d), F32), ln_g, ln_b, alpha,
                      ctx_len, extra=cw)


def _swa_mixer(x, modtab, w_qkv, b_qkv, sink, w_o, b_o, ln_g, ln_b, alpha, ctx_len, rope):
    d = x.shape[2]
    nq = SWA_HEADS * SWA_HEAD_DIM
    nkv = SWA_KV_HEADS * SWA_HEAD_DIM
    hd = SWA_HEAD_DIM

    def arrange(a):
        q, k, v = a[..., :nq], a[..., nq:nq + nkv], a[..., nq + nkv:]
        kk = _dup_heads(k, hd)
        return jnp.concatenate([q, _rot_cols(q, hd), kk, _rot_cols(kk, hd), _dup_heads(v, hd)], axis=-1)

    w = arrange(w_qkv).astype(BF16)
    b = arrange(b_qkv.reshape(1, -1))
    log2e = math.log2(math.e)
    plan = [("rope", 0, 0, nq, 0, nq, SWA_HEAD_DIM ** -0.5 * log2e),
            ("rope", 1, 0, 2 * nkv, 2 * nq, 2 * nq + 2 * nkv, 1.0),
            ("plain", 2, 0, 2 * nkv, 2 * nq + 4 * nkv, None, 1.0)]
    q, kk, vv = _project(x, modtab, w, b, plan, (nq, 2 * nkv, 2 * nkv), (BF16, BF16, BF16), ctx_len, rope=rope)
    o = _swa_attention(q, kk, vv, sink.astype(F32) * log2e, ctx_len)
    return _mixer_out("plain", x, modtab, (o,), w_o.astype(BF16), b_o.reshape(1, d), ln_g, ln_b, alpha, ctx_len)


def _diff_mixer(x, modtab, w_qkv, lam_p, subln_g, w_o, lam_init, ln_g, ln_b, alpha, ctx_len, rope):
    d = x.shape[2]
    hd = DIFF_HEAD_DIM
    wq, wk, wv = w_qkv[:, :d], w_qkv[:, d:2 * d], w_qkv[:, 2 * d:]
    w = jnp.concatenate([wq, _rot_cols(wq, hd), wk, _rot_cols(wk, hd), wv], axis=-1).astype(BF16)
    plan = [("rope", 0, 0, d, 0, d, DIFF_HEAD_DIM ** -0.5 * math.log2(math.e)),
            ("rope", 1, 0, d, 2 * d, 3 * d, 1.0)]
    for h in range(DIFF_HEADS):
        plan += [("plain", 2, 2 * h * LANES, LANES, 4 * d + h * LANES, None, 1.0),
                 ("ones", 2, (2 * h + 1) * LANES, LANES, None, None, 1.0)]
    q, k, v = _project(x, modtab, w, jnp.zeros((1, 5 * d), F32), plan, (d, d, 2 * d), (BF16, BF16, BF16), ctx_len,
                       rope=rope)
    o = _diff_attention(q, k, v, lam_p.astype(F32), subln_g.astype(F32), ctx_len, lam_init)
    return _mixer_out("plain", x, modtab, (o,), w_o.astype(BF16), jnp.zeros((1, d), F32), ln_g, ln_b, alpha, ctx_len)


def _delta_mixer(x, modtab, w_qkvz, w_ba, a_log, dt_bias, w_conv, norm_g, w_o, ln_g, ln_b, alpha, ctx_len):
    bsz, t, d = x.shape
    nqk = DN_QK_HEADS * DN_HEAD_DIM
    nv = DN_V_HEADS * DN_HEAD_DIM
    nc = 2 * nqk + nv
    nba = w_ba.shape[1]
    w = jnp.concatenate([w_qkvz, jnp.pad(w_ba, ((0, 0), (0, LANES - nba)))], axis=-1).astype(BF16)
    ntot = w.shape[1]
    plan = [("plain", 0, 0, nc, 0, None, 1.0), ("plain", 1, 0, nv, nc, None, 1.0),
            ("plain", 2, 0, LANES, nc + nv, None, 1.0)]
    qkv_pre, z, ba = _project(x, modtab, w, jnp.zeros((1, ntot), F32), plan, (nc, nv, LANES), (BF16, BF16, F32),
                              ctx_len)
    zeros16 = jnp.zeros((DN_V_HEADS,), F32)
    lanes_of = lambda p: jnp.pad(jnp.concatenate([zeros16, p[0], zeros16, p[1]]), (0, LANES - 4 * DN_V_HEADS))
    alog_l = lanes_of(a_log.astype(F32)).reshape(1, LANES)
    dtb_l = lanes_of(dt_bias.astype(F32)).reshape(1, LANES)
    cw = jnp.pad(w_conv, ((0, 8 - DN_CONV), (0, 0)))
    q, k, v, gb = _delta_prep(qkv_pre, cw, ba, alog_l, dtb_l, ctx_len, nqk, nv)
    rep = DN_V_HEADS // DN_QK_HEADS
    gcum = jnp.stack([gb[:, :, DN_V_HEADS:2 * DN_V_HEADS], gb[:, :, 3 * DN_V_HEADS:4 * DN_V_HEADS]])
    gt = jnp.transpose(gcum, (0, 1, 3, 2))
    gt_a = jnp.pad(gt.reshape(2, bsz, DN_QK_HEADS, rep, t), ((0, 0), (0, 0), (0, 0), (0, 8 - rep), (0, 0)))
    gt_b = gt.reshape(2, bsz, DN_V_HEADS // DN_SCAN_HEADS, DN_SCAN_HEADS, t)
    u, w_, qg, kd, a = _delta_chunks(q, k, v, gb, gt_a)
    o2 = _delta_scan(u, w_, qg, kd, a, gt_b, ctx_len)
    return _mixer_out("delta", x, modtab, (o2[0], o2[1], z), w_o.astype(BF16), jnp.zeros((1, d), F32),
                      ln_g, ln_b, alpha, ctx_len, extra=norm_g.astype(F32).reshape(1, DN_HEAD_DIM))


def kernel(x, c, ctx, c_ctx, mod_w, mod_b, ln1_g, ln1_b, ln2_g, ln2_b, router_w, router_b, exp_w1, exp_b1, exp_w2, exp_b2, conv_in_w, conv_w, conv_out_w, swa_qkv_w, swa_qkv_b, swa_sink, swa_out_w, swa_out_b, diff_qkv_w, diff_lambda, diff_subln_g, diff_out_w, delta_qkvz_w, delta_ba_w, delta_a_log, delta_dt_bias, delta_conv_w, delta_norm_g, delta_out_w):
    bsz, n_lat, d = x.shape
    ctx_len = ctx.shape[1]
    depth = mod_w.shape[0]
    alpha = (2 * depth) ** 0.25
    cc = jnp.zeros((16, d), F32).at[:bsz].set(c).at[bsz].set(c_ctx)
    mods = _mod_vectors(cc, mod_w, mod_b).reshape(depth, 16, 6, d)
    mod_lat = mods[:, :bsz]
    mod_ctx = jnp.broadcast_to(mods[:, bsz:bsz + 1], mod_lat.shape)
    modtabs = jnp.pad(jnp.stack([mod_ctx, mod_lat], axis=2), ((0, 0), (0, 0), (0, 0), (0, 2), (0, 0)))
    rope = _rope_tables(n_lat, ctx_len, SWA_HEAD_DIM)
    row1 = lambda a: a.reshape(1, d)
    groups = [slice(g * bsz // BATCH_GROUPS, (g + 1) * bsz // BATCH_GROUPS) for g in range(BATCH_GROUPS)]
    xs = [jnp.concatenate([ctx[g], x[g]], axis=1) for g in groups]
    for i in range(depth):
        last = i == depth - 1
        kind, j = i % N_MIXERS, i // N_MIXERS
        mts = [modtabs[i, g] for g in groups]
        g1, b1 = row1(ln1_g[i]), row1(ln1_b[i])
        g2, b2 = row1(ln2_g[i]), row1(ln2_b[i])

        def mixer(xg, mt):
            if kind == 0:
                return _conv_mixer(xg, mt, conv_in_w[j], conv_w[j], conv_out_w[j], g1, b1, alpha, ctx_len)
            if kind == 1:
                return _swa_mixer(xg, mt, swa_qkv_w[j], swa_qkv_b[j], swa_sink[j], swa_out_w[j], swa_out_b[j],
                                  g1, b1, alpha, ctx_len, rope)
            if kind == 2:
                lam_init = 0.8 - 0.6 * math.exp(-0.3 * i)
                return _diff_mixer(xg, mt, diff_qkv_w[j], diff_lambda[j], diff_subln_g[j], diff_out_w[j], lam_init,
                                   g1, b1, alpha, ctx_len, rope)
            return _delta_mixer(xg, mt, delta_qkvz_w[j], delta_ba_w[j], delta_a_log[j], delta_dt_bias[j],
                                delta_conv_w[j], delta_norm_g[j], delta_out_w[j], g1, b1, alpha, ctx_len)

        xs = [mixer(xg, mt) for xg, mt in zip(xs, mts)]
        sts = [_moe_dispatch(xg, mt, router_w[i], router_b[i], ctx_len, last) for xg, mt in zip(xs, mts)]
        zs = [_moe_experts(st, (exp_w1, exp_b1, exp_w2, exp_b2, i)) for st in sts]
        xs = [_moe_combine(xg, mt, st, z, g2, b2, alpha, ctx_len) for xg, mt, st, z in zip(xs, mts, sts, zs)]
    return jnp.concatenate(xs, axis=0)
```
